```python
import math
import jax
import jax.numpy as jnp
from jax import lax
import numpy as np

D_MODEL = 2048
BATCH = 4
SEQ = 2048
DEPTH = 2
DEC_BATCH = 128
DEC_SEQ = 1
PAST_LEN = 8192
PAGE_SIZE = 128

CONV_CH = 1024
CONV_WIDTH = 31
CONV_STATE = CONV_WIDTH - 1
MLA_HEADS = 8
QK_NOPE = 128
QK_ROPE = 64
V_DIM = 128
MLA_WIDTH = MLA_HEADS * V_DIM
Q_LORA = 512
KV_LORA = 256
MLA_SCALE = (QK_NOPE + QK_ROPE) ** -0.5
ROPE_BASE = 10000.0
Q_BLOCK = 128
MIX_WIDTH = CONV_CH + MLA_WIDTH
IN_SPLITS = (CONV_CH, CONV_CH, CONV_CH, Q_LORA, KV_LORA, QK_ROPE, MLA_WIDTH)
IN_COLS = 3 * CONV_CH + Q_LORA + KV_LORA + QK_ROPE + MLA_WIDTH
N_MEM = 256
MEM_HEADS = 4
MEM_HEAD_DIM = 128
MEM_WIDTH = MEM_HEADS * MEM_HEAD_DIM
MEM_SCALE = MEM_HEAD_DIM ** -0.5
EPS = 1e-6

kernel_name = "hymba_conformer_mla_memxattn_step"


def rms_norm(x, g):
    xf = x.astype(jnp.float32)
    y = xf * lax.rsqrt(jnp.mean(xf * xf, axis=-1, keepdims=True) + EPS)
    return (y * g.astype(jnp.float32)).astype(x.dtype)


def layer_norm(x, g, b):
    xf = x.astype(jnp.float32)
    mu = jnp.mean(xf, axis=-1, keepdims=True)
    var = jnp.mean(jnp.square(xf - mu), axis=-1, keepdims=True)
    y = (xf - mu) * lax.rsqrt(var + EPS)
    return (y * g.astype(jnp.float32) + b.astype(jnp.float32)).astype(x.dtype)


def rope(x, pos):
    half = QK_ROPE // 2
    inv_freq = ROPE_BASE ** (-jnp.arange(half, dtype=jnp.float32) / half)
    ang = pos.astype(jnp.float32)[:, None] * inv_freq[None, :]
    ang = ang.reshape((ang.shape[0],) + (1,) * (x.ndim - 3) + (half,))
    cos, sin = jnp.cos(ang), jnp.sin(ang)
    xf = x.astype(jnp.float32)
    x1, x2 = xf[..., :half], xf[..., half:]
    return jnp.concatenate([x1 * cos - x2 * sin, x2 * cos + x1 * sin], axis=-1).astype(x.dtype)


def split_in_proj(hn, w_in):
    proj = jnp.einsum("bsd,dc->bsc", hn, w_in)
    idx = np.cumsum(IN_SPLITS)[:-1].tolist()
    return jnp.split(proj, idx, axis=-1)


def causal_depthwise_conv(u_ext, w, b):
    out = lax.conv_general_dilated(
        u_ext, w[:, None, :].astype(u_ext.dtype), window_strides=(1,), padding="VALID",
        dimension_numbers=("NWC", "WIO", "NWC"), feature_group_count=CONV_CH)
    return out + b.astype(out.dtype)


def mla_prompt_attention(q_nope, q_pe, c_kv, k_pe, w_uk, w_uv):
    bsz, s_len = c_kv.shape[:2]
    k_nope = jnp.einsum("bsr,rhe->bshe", c_kv, w_uk)
    v = jnp.einsum("bsr,rhe->bshe", c_kv, w_uv)
    n_blk = s_len // Q_BLOCK
    qn = q_nope.reshape(bsz, n_blk, Q_BLOCK, MLA_HEADS, QK_NOPE).transpose(1, 0, 2, 3, 4)
    qp = q_pe.reshape(bsz, n_blk, Q_BLOCK, MLA_HEADS, QK_ROPE).transpose(1, 0, 2, 3, 4)
    starts = jnp.arange(n_blk, dtype=jnp.int32) * Q_BLOCK
    key_pos = jnp.arange(s_len, dtype=jnp.int32)

    def block(args):
        qn_b, qp_b, start = args
        s = (jnp.einsum("bqhe,bkhe->bhqk", qn_b, k_nope)
             + jnp.einsum("bqhe,bke->bhqk", qp_b, k_pe)).astype(jnp.float32) * MLA_SCALE
        q_pos = start + jnp.arange(Q_BLOCK, dtype=jnp.int32)
        s = jnp.where(key_pos[None, :] <= q_pos[:, None], s, -jnp.inf)
        p = jax.nn.softmax(s, axis=-1).astype(v.dtype)
        return jnp.einsum("bhqk,bkhe->bqhe", p, v)

    out = lax.map(block, (qn, qp, starts))
    return out.transpose(1, 0, 2, 3, 4).reshape(bsz, s_len, MLA_WIDTH)


def mla_sample_attention(q_nope, q_pe, c_new, kpe_new, c_past, kpe_past, w_uk, w_uv):
    bsz, t_len = c_new.shape[:2]
    p_len = c_past.shape[1]
    q_lat = jnp.einsum("bqhe,rhe->bqhr", q_nope, w_uk)
    s_past = (jnp.einsum("bqhr,bkr->bhqk", q_lat, c_past)
              + jnp.einsum("bqhe,bke->bhqk", q_pe, kpe_past)).astype(jnp.float32)
    s_new = (jnp.einsum("bqhr,bkr->bhqk", q_lat, c_new)
             + jnp.einsum("bqhe,bke->bhqk", q_pe, kpe_new)).astype(jnp.float32)
    causal = jnp.tril(jnp.ones((t_len, t_len), dtype=bool))
    s_new = jnp.where(causal, s_new, -jnp.inf)
    s = jnp.concatenate([s_past, s_new], axis=-1) * MLA_SCALE
    p = jax.nn.softmax(s, axis=-1).astype(c_new.dtype)
    o_lat = (jnp.einsum("bhqk,bkr->bqhr", p[..., :p_len], c_past)
             + jnp.einsum("bhqk,bkr->bqhr", p[..., p_len:], c_new))
    o = jnp.einsum("bqhr,rhe->bqhe", o_lat, w_uv)
    return o.reshape(bsz, t_len, MLA_WIDTH)


def mixer_sublayer(x, pos, conv_ctx, lp, past):
    hn = rms_norm(x, lp["norm_mix_pre"])
    a, b_glu, g_conv, q_lat, kv_lat, k_pe, g_mla = split_in_proj(hn, lp["w_in"])
    u = a * jax.nn.sigmoid(b_glu)
    u_ext = jnp.concatenate([conv_ctx.astype(u.dtype), u], axis=1)
    conv_y = causal_depthwise_conv(u_ext, lp["conv_w"], lp["conv_b"])
    conv_out = jax.nn.silu(layer_norm(conv_y, lp["conv_ln_g"], lp["conv_ln_b"])) * jax.nn.silu(g_conv)
    q = jnp.einsum("bsr,rhe->bshe", rms_norm(q_lat, lp["q_norm_g"]), lp["w_uq"])
    q_nope, q_pe = q[..., :QK_NOPE], rope(q[..., QK_NOPE:], pos)
    c_kv = rms_norm(kv_lat, lp["kv_norm_g"])
    k_pe = rope(k_pe, pos)
    if past is None:
        attn = mla_prompt_attention(q_nope, q_pe, c_kv, k_pe, lp["w_uk"], lp["w_uv"])
    else:
        attn = mla_sample_attention(q_nope, q_pe, c_kv, k_pe, past[0], past[1], lp["w_uk"], lp["w_uv"])
    mla_out = attn * jax.nn.silu(g_mla)
    y = jnp.einsum("bsc,cd->bsd", jnp.concatenate([conv_out, mla_out], axis=-1), lp["w_out"])
    x = x + rms_norm(y, lp["norm_mix_post"])
    return x, c_kv, k_pe, u_ext[:, -CONV_STATE:]


def memory_kv(mem, lp):
    mn = rms_norm(mem, lp["norm_mem"])
    mem_k = jnp.einsum("bmd,dhe->bmhe", mn, lp["w_mk"])
    mem_v = jnp.einsum("bmd,dhe->bmhe", mn, lp["w_mv"])
    return mem_k, mem_v


def cross_sublayer(x, mem_k, mem_v, lp):
    hn = rms_norm(x, lp["norm_x_pre"])
    q = jnp.einsum("bsd,dhe->bshe", hn, lp["w_xq"])
    s = jnp.einsum("bshe,bmhe->bhsm", q, mem_k).astype(jnp.float32) * MEM_SCALE
    p = jax.nn.softmax(s, axis=-1).astype(mem_v.dtype)
    o = jnp.einsum("bhsm,bmhe->bshe", p, mem_v)
    y = jnp.einsum("bshe,hed->bsd", o, lp["w_xo"])
    return x + rms_norm(y, lp["norm_x_post"])


def setup_inputs(seed: int = 0) -> dict:
    key = jax.random.key(seed)
    keys = iter(list(jax.random.split(key, 40)))
    n_pages = PAST_LEN // PAGE_SIZE
    n_phys_pages = (DEC_BATCH * n_pages * 5) // 4

    def normal(shape, scale=1.0):
        return scale * jax.random.normal(next(keys), shape, jnp.float32)

    def gain(shape):
        return 1.0 + 0.01 * jax.random.normal(next(keys), shape, jnp.float32)

    page_table = jax.random.permutation(next(keys), n_phys_pages)[: DEC_BATCH * n_pages]
    page_table = page_table.reshape(DEC_BATCH, n_pages).astype(jnp.int32)
    return {
        "x_prompt": normal((BATCH, SEQ, D_MODEL)),
        "x_sample": normal((DEC_BATCH, DEC_SEQ, D_MODEL)),
        "mem_prompt": normal((BATCH, N_MEM, D_MODEL)),
        "cache_kv_latent": normal((n_phys_pages, DEPTH, PAGE_SIZE, KV_LORA)),
        "cache_k_rope": normal((n_phys_pages, DEPTH, PAGE_SIZE, QK_ROPE)),
        "state_conv": normal((DEC_BATCH, DEPTH, CONV_STATE, CONV_CH), 0.5),
        "cache_mem_k": normal((DEC_BATCH, DEPTH, N_MEM, MEM_HEADS, MEM_HEAD_DIM)),
        "cache_mem_v": normal((DEC_BATCH, DEPTH, N_MEM, MEM_HEADS, MEM_HEAD_DIM)),
        "page_table": page_table,
        "norm_mix_pre": gain((DEPTH, D_MODEL)),
        "w_in": normal((DEPTH, D_MODEL, IN_COLS), D_MODEL ** -0.5),
        "conv_w": normal((DEPTH, CONV_WIDTH, CONV_CH), CONV_WIDTH ** -0.5),
        "conv_b": normal((DEPTH, CONV_CH), 0.01),
        "conv_ln_g": gain((DEPTH, CONV_CH)),
        "conv_ln_b": normal((DEPTH, CONV_CH), 0.01),
        "q_norm_g": gain((DEPTH, Q_LORA)),
        "w_uq": normal((DEPTH, Q_LORA, MLA_HEADS, QK_NOPE + QK_ROPE), Q_LORA ** -0.5),
        "kv_norm_g": gain((DEPTH, KV_LORA)),
        "w_uk": normal((DEPTH, KV_LORA, MLA_HEADS, QK_NOPE), KV_LORA ** -0.5),
        "w_uv": normal((DEPTH, KV_LORA, MLA_HEADS, V_DIM), KV_LORA ** -0.5),
        "w_out": normal((DEPTH, MIX_WIDTH, D_MODEL), MIX_WIDTH ** -0.5),
        "norm_mix_post": gain((DEPTH, D_MODEL)),
        "norm_x_pre": gain((DEPTH, D_MODEL)),
        "norm_mem": gain((DEPTH, D_MODEL)),
        "w_xq": normal((DEPTH, D_MODEL, MEM_HEADS, MEM_HEAD_DIM), D_MODEL ** -0.5),
        "w_mk": normal((DEPTH, D_MODEL, MEM_HEADS, MEM_HEAD_DIM), D_MODEL ** -0.5),
        "w_mv": normal((DEPTH, D_MODEL, MEM_HEADS, MEM_HEAD_DIM), D_MODEL ** -0.5),
        "w_xo": normal((DEPTH, MEM_HEADS, MEM_HEAD_DIM, D_MODEL), MEM_WIDTH ** -0.5),
        "norm_x_post": gain((DEPTH, D_MODEL)),
    }


def reference(x_prompt, x_sample, mem_prompt, cache_kv_latent, cache_k_rope, state_conv,
              cache_mem_k, cache_mem_v, page_table,
              norm_mix_pre, w_in, conv_w, conv_b, conv_ln_g, conv_ln_b, q_norm_g, w_uq,
              kv_norm_g, w_uk, w_uv, w_out, norm_mix_post, norm_x_pre, norm_mem,
              w_xq, w_mk, w_mv, w_xo, norm_x_post):
    b_p, s_p = x_prompt.shape[:2]
    b_s, t_s = x_sample.shape[:2]
    n_pages = page_table.shape[1]
    past_len = n_pages * PAGE_SIZE
    pos_p = jnp.arange(s_p, dtype=jnp.int32)
    pos_s = past_len + jnp.arange(t_s, dtype=jnp.int32)
    zero_ctx = jnp.zeros((b_p, CONV_STATE, CONV_CH), x_prompt.dtype)
    xp, xs = x_prompt, x_sample
    lat_p, kpe_p, conv_p, mk_p, mv_p = [], [], [], [], []
    lat_s, kpe_s, conv_s = [], [], []
    for l in range(DEPTH):
        lp = {
            "norm_mix_pre": norm_mix_pre[l], "w_in": w_in[l], "conv_w": conv_w[l],
            "conv_b": conv_b[l], "conv_ln_g": conv_ln_g[l], "conv_ln_b": conv_ln_b[l],
            "q_norm_g": q_norm_g[l], "w_uq": w_uq[l], "kv_norm_g": kv_norm_g[l],
            "w_uk": w_uk[l], "w_uv": w_uv[l], "w_out": w_out[l],
            "norm_mix_post": norm_mix_post[l], "norm_x_pre": norm_x_pre[l],
            "norm_mem": norm_mem[l], "w_xq": w_xq[l], "w_mk": w_mk[l], "w_mv": w_mv[l],
            "w_xo": w_xo[l], "norm_x_post": norm_x_post[l],
        }
        xp, c_kv, k_pe, conv_state = mixer_sublayer(xp, pos_p, zero_ctx, lp, None)
        mem_k, mem_v = memory_kv(mem_prompt, lp)
        xp = cross_sublayer(xp, mem_k, mem_v, lp)
        lat_p.append(c_kv)
        kpe_p.append(k_pe)
        conv_p.append(conv_state)
        mk_p.append(mem_k)
        mv_p.append(mem_v)
        c_past = cache_kv_latent[page_table, l].reshape(b_s, past_len, KV_LORA)
        kpe_past = cache_k_rope[page_table, l].reshape(b_s, past_len, QK_ROPE)
        xs, c_kv, k_pe, conv_state = mixer_sublayer(xs, pos_s, state_conv[:, l], lp, (c_past, kpe_past))
        xs = cross_sublayer(xs, cache_mem_k[:, l], cache_mem_v[:, l], lp)
        lat_s.append(c_kv)
        kpe_s.append(k_pe)
        conv_s.append(conv_state)
    y_prompt = xp
    y_sample = xs
    new_kv_latent_prompt = jnp.stack(lat_p, axis=1)
    new_k_rope_prompt = jnp.stack(kpe_p, axis=1)
    new_conv_prompt = jnp.stack(conv_p, axis=1)
    new_mem_k_prompt = jnp.stack(mk_p, axis=1)
    new_mem_v_prompt = jnp.stack(mv_p, axis=1)
    new_kv_latent_sample = jnp.stack(lat_s, axis=1)
    new_k_rope_sample = jnp.stack(kpe_s, axis=1)
    new_conv_sample = jnp.stack(conv_s, axis=1)
    return (y_prompt, y_sample, new_kv_latent_prompt, new_k_rope_prompt, new_conv_prompt,
            new_mem_k_prompt, new_mem_v_prompt, new_kv_latent_sample, new_k_rope_sample,
            new_conv_sample)
```

```python
import functools

import jax
import jax.numpy as jnp
from jax import lax
from jax.experimental import pallas as pl
from jax.experimental.pallas import tpu as pltpu

F32 = jnp.float32
BF16 = jnp.bfloat16

D_MODEL = 2048
CONV_CH = 1024
CONV_WIDTH = 31
CONV_STATE = CONV_WIDTH - 1
MLA_HEADS = 8
QK_NOPE = 128
QK_ROPE = 64
V_DIM = 128
MLA_WIDTH = MLA_HEADS * V_DIM
Q_LORA = 512
KV_LORA = 256
MLA_SCALE = (QK_NOPE + QK_ROPE) ** -0.5
ROPE_BASE = 10000.0
PAGE_SIZE = 128
N_MEM = 256
MEM_HEADS = 4
MEM_HEAD_DIM = 128
MEM_WIDTH = MEM_HEADS * MEM_HEAD_DIM
MEM_SCALE = MEM_HEAD_DIM ** -0.5
EPS = 1e-6

LANE = 128
QK_PAD = 256
NEG_BIG = -1e30

PROJ_COLS = 5120
COL_A, COL_B, COL_GC, COL_GM, COL_QL, COL_KV, COL_KPE = 0, 1024, 2048, 3072, 4096, 4608, 4864

VMEM_LIMIT = 52 * 1024 * 1024


def _params(*sem):
    return pltpu.CompilerParams(dimension_semantics=sem, vmem_limit_bytes=VMEM_LIMIT)


def _rms(x, g):
    return x * lax.rsqrt(jnp.mean(x * x, axis=-1, keepdims=True) + EPS) * g


def _silu(x):
    return x * jax.nn.sigmoid(x)


def _dot(a, b):
    return jnp.dot(a, b, preferred_element_type=F32)


def _dot_nt(a, b):
    return lax.dot_general(a, b, (((1,), (1,)), ((), ())), preferred_element_type=F32)


def _rope128(x, cos_t, sin_n, sin_p):
    return x * cos_t + pltpu.roll(x, 96, 1) * sin_n + pltpu.roll(x, 32, 1) * sin_p


def _rms_matmul_kernel(x_ref, g_ref, w_ref, o_ref, hn_ref):
    @pl.when(pl.program_id(1) == 0)
    def _():
        hn_ref[...] = _rms(x_ref[...], g_ref[...]).astype(BF16)

    o_ref[...] = _dot(hn_ref[...], w_ref[...]).astype(o_ref.dtype)


def rms_matmul(x, g, w, *, tm, tn, name):
    t, k = x.shape
    n = w.shape[1]
    return pl.pallas_call(
        _rms_matmul_kernel,
        grid=(t // tm, n // tn),
        in_specs=[
            pl.BlockSpec((tm, k), lambda i, j: (i, 0)),
            pl.BlockSpec((1, k), lambda i, j: (0, 0)),
            pl.BlockSpec((k, tn), lambda i, j: (0, j)),
        ],
        out_specs=pl.BlockSpec((tm, tn), lambda i, j: (i, j)),
        out_shape=jax.ShapeDtypeStruct((t, n), F32),
        scratch_shapes=[pltpu.VMEM((tm, k), BF16)],
        compiler_params=_params("parallel", "arbitrary"),
        name=name,
    )(x, g.reshape(1, k), w)


CONV_ROWS = 32
CONV_HALO = 32


def _conv_prompt_kernel(a_ref, b_ref, gc_ref, w_ref, cb_ref, lg_ref, lb_ref,
                        out_ref, state_ref, uext_ref, *, ts):
    s = pl.program_id(1)

    @pl.when(s == 0)
    def _():
        uext_ref[0:CONV_HALO, :] = jnp.zeros((CONV_HALO, CONV_CH), F32)

    @pl.when(s > 0)
    def _():
        uext_ref[0:CONV_HALO, :] = uext_ref[ts:ts + CONV_HALO, :]

    uext_ref[CONV_HALO:CONV_HALO + ts, :] = a_ref[...] * jax.nn.sigmoid(b_ref[...])

    @pl.when(s == pl.num_programs(1) - 1)
    def _():
        state_ref[0] = uext_ref[CONV_HALO + ts - CONV_STATE:CONV_HALO + ts, :]

    shift = CONV_HALO - CONV_STATE

    def chunk(r, carry):
        r0 = pl.multiple_of(r * CONV_ROWS, CONV_ROWS)
        acc = jnp.zeros((CONV_ROWS, CONV_CH), F32) + cb_ref[...]
        win = uext_ref.at[pl.ds(r0, CONV_ROWS + CONV_HALO), :]
        for k in range(CONV_WIDTH):
            acc = acc + w_ref[k:k + 1, :] * win[k + shift:k + shift + CONV_ROWS, :]
        mu = jnp.mean(acc, axis=-1, keepdims=True)
        d = acc - mu
        var = jnp.mean(d * d, axis=-1, keepdims=True)
        y = d * lax.rsqrt(var + EPS) * lg_ref[...] + lb_ref[...]
        gate = gc_ref[pl.ds(r0, CONV_ROWS), :]
        out_ref[pl.ds(r0, CONV_ROWS), :] = (_silu(y) * _silu(gate)).astype(out_ref.dtype)
        return carry

    lax.fori_loop(0, ts // CONV_ROWS, chunk, 0)


def conv_prompt(proj, conv_w, conv_b, ln_g, ln_b, *, bsz, seq, ts):
    n_s = seq // ts
    row = lambda b, s: b * n_s + s
    vec = lambda: pl.BlockSpec((1, CONV_CH), lambda b, s: (0, 0))
    return pl.pallas_call(
        functools.partial(_conv_prompt_kernel, ts=ts),
        grid=(bsz, n_s),
        in_specs=[
            pl.BlockSpec((ts, CONV_CH), lambda b, s: (row(b, s), COL_A // CONV_CH)),
            pl.BlockSpec((ts, CONV_CH), lambda b, s: (row(b, s), COL_B // CONV_CH)),
            pl.BlockSpec((ts, CONV_CH), lambda b, s: (row(b, s), COL_GC // CONV_CH)),
            pl.BlockSpec((CONV_WIDTH, CONV_CH), lambda b, s: (0, 0)),
            vec(), vec(), vec(),
        ],
        out_specs=[
            pl.BlockSpec((ts, CONV_CH), lambda b, s: (row(b, s), 0)),
            pl.BlockSpec((1, CONV_STATE, CONV_CH), lambda b, s: (b, 0, 0)),
        ],
        out_shape=[
            jax.ShapeDtypeStruct((bsz * seq, CONV_CH), BF16),
            jax.ShapeDtypeStruct((bsz, CONV_STATE, CONV_CH), F32),
        ],
        scratch_shapes=[pltpu.VMEM((CONV_HALO + ts, CONV_CH), F32)],
        compiler_params=_params("parallel", "arbitrary"),
        name="conv_prompt",
    )(proj, proj, proj, conv_w, conv_b.reshape(1, -1), ln_g.reshape(1, -1), ln_b.reshape(1, -1))


def _conv_sample_kernel(a_ref, b_ref, gc_ref, st_ref, w_ref, cb_ref, lg_ref, lb_ref, out_ref, u_ref):
    st = st_ref[:, 0]
    u = a_ref[...] * jax.nn.sigmoid(b_ref[...])
    u_ref[...] = u
    y = jnp.sum(st * w_ref[0:CONV_STATE, :][None], axis=1, keepdims=True)
    y = y + u * w_ref[CONV_STATE:CONV_WIDTH, :][None] + cb_ref[...][None]
    mu = jnp.mean(y, axis=-1, keepdims=True)
    d = y - mu
    var = jnp.mean(d * d, axis=-1, keepdims=True)
    z = d * lax.rsqrt(var + EPS) * lg_ref[...][None] + lb_ref[...][None]
    out_ref[...] = (_silu(z) * _silu(gc_ref[...])).astype(out_ref.dtype)


def conv_sample(proj3, state_conv, layer, conv_w, conv_b, ln_g, ln_b, *, bt):
    bsz = state_conv.shape[0]
    vec = lambda: pl.BlockSpec((1, CONV_CH), lambda i: (0, 0))
    row = lambda: pl.BlockSpec((bt, 1, CONV_CH), lambda i: (i, 0, 0))
    return pl.pallas_call(
        _conv_sample_kernel,
        grid=(bsz // bt,),
        in_specs=[
            pl.BlockSpec((bt, 1, CONV_CH), lambda i: (i, 0, COL_A // CONV_CH)),
            pl.BlockSpec((bt, 1, CONV_CH), lambda i: (i, 0, COL_B // CONV_CH)),
            pl.BlockSpec((bt, 1, CONV_CH), lambda i: (i, 0, COL_GC // CONV_CH)),
            pl.BlockSpec((bt, 1, CONV_STATE, CONV_CH), lambda i: (i, layer, 0, 0)),
            pl.BlockSpec((CONV_WIDTH, CONV_CH), lambda i: (0, 0)),
            vec(), vec(), vec(),
        ],
        out_specs=[row(), row()],
        out_shape=[
            jax.ShapeDtypeStruct((bsz, 1, CONV_CH), BF16),
            jax.ShapeDtypeStruct((bsz, 1, CONV_CH), F32),
        ],
        compiler_params=_params("parallel"),
        name="conv_sample",
    )(proj3, proj3, proj3, state_conv, conv_w, conv_b.reshape(1, -1),
      ln_g.reshape(1, -1), ln_b.reshape(1, -1))


def _conv_state_kernel(st_ref, u_ref, o_ref):
    o_ref[:, :, 0:CONV_STATE - 1, :] = st_ref[:, :, 1:CONV_STATE, :]
    o_ref[:, :, CONV_STATE - 1:CONV_STATE, :] = u_ref[...]


def conv_state_update(state_conv, u_all, *, bt):
    bsz, depth = state_conv.shape[:2]
    blk = lambda r: pl.BlockSpec((bt, depth, r, CONV_CH), lambda i: (i, 0, 0, 0))
    return pl.pallas_call(
        _conv_state_kernel,
        grid=(bsz // bt,),
        in_specs=[blk(CONV_STATE), blk(1)],
        out_specs=blk(CONV_STATE),
        out_shape=jax.ShapeDtypeStruct(state_conv.shape, F32),
        compiler_params=_params("parallel"),
        name="conv_state_update",
    )(state_conv, u_all)


def _mla_common(ql_ref, kvl_ref, kpe_ref, cos_ref, sn_ref, sp_ref, qg_ref, kvg_ref, wqn_ref, wqp_ref):
    qn = _rms(ql_ref[...], qg_ref[...]).astype(BF16)
    q_nope = _dot(qn, wqn_ref[...]) * MLA_SCALE
    q_rope_raw = _dot(qn, wqp_ref[...]) * MLA_SCALE
    cos_t, sin_n, sin_p = cos_ref[...], sn_ref[...], sp_ref[...]
    q_rope = [_rope128(q_rope_raw[:, h * LANE:(h + 1) * LANE], cos_t, sin_n, sin_p)
              for h in range(MLA_HEADS)]
    c_kv = _rms(kvl_ref[...], kvg_ref[...])
    k_pe = _rope128(kpe_ref[...], cos_t, sin_n, sin_p)
    return q_nope, q_rope, c_kv, k_pe


def _mla_prompt_kernel(ql_ref, kvl_ref, kpe_ref, cos_ref, sn_ref, sp_ref, qg_ref, kvg_ref,
                       wqn_ref, wqp_ref, wuk_ref, wuv_ref,
                       q_ref, k_ref, v_ref, ckv_ref, kpeo_ref):
    q_nope, q_rope, c_kv, k_pe = _mla_common(ql_ref, kvl_ref, kpe_ref, cos_ref, sn_ref, sp_ref,
                                             qg_ref, kvg_ref, wqn_ref, wqp_ref)
    ckv_ref[...] = c_kv
    kpeo_ref[...] = k_pe[:, :QK_ROPE]
    c_bf = c_kv.astype(BF16)
    k_nope = _dot(c_bf, wuk_ref[...])
    v_ref[...] = _dot(c_bf, wuv_ref[...]).astype(BF16)
    k_pe_bf = k_pe.astype(BF16)
    for h in range(MLA_HEADS):
        lo = h * QK_PAD
        q_ref[:, lo:lo + LANE] = q_nope[:, h * LANE:(h + 1) * LANE].astype(BF16)
        q_ref[:, lo + LANE:lo + QK_PAD] = q_rope[h].astype(BF16)
        k_ref[:, lo:lo + LANE] = k_nope[:, h * LANE:(h + 1) * LANE].astype(BF16)
        k_ref[:, lo + LANE:lo + QK_PAD] = k_pe_bf


def _proj_specs(tm, n_tab):
    return [
        pl.BlockSpec((tm, Q_LORA), lambda i: (i, COL_QL // Q_LORA)),
        pl.BlockSpec((tm, KV_LORA), lambda i: (i, COL_KV // KV_LORA)),
        pl.BlockSpec((tm, LANE), lambda i: (i, COL_KPE // LANE)),
        pl.BlockSpec((tm, LANE), lambda i: (i % n_tab, 0)),
        pl.BlockSpec((tm, LANE), lambda i: (i % n_tab, 0)),
        pl.BlockSpec((tm, LANE), lambda i: (i % n_tab, 0)),
    ]


def _full(shape):
    return pl.BlockSpec(shape, lambda i: (0,) * len(shape))


def mla_prompt(proj, tabs, q_g, kv_g, wqn, wqp, wuk, wuv, *, tm):
    t = proj.shape[0]
    n_tab = tabs[0].shape[0] // tm
    h = MLA_HEADS
    return pl.pallas_call(
        _mla_prompt_kernel,
        grid=(t // tm,),
        in_specs=_proj_specs(tm, n_tab) + [
            _full((1, Q_LORA)), _full((1, KV_LORA)),
            _full((Q_LORA, h * LANE)), _full((Q_LORA, h * LANE)),
            _full((KV_LORA, h * LANE)), _full((KV_LORA, h * LANE)),
        ],
        out_specs=[
            pl.BlockSpec((tm, h * QK_PAD), lambda i: (i, 0)),
            pl.BlockSpec((tm, h * QK_PAD), lambda i: (i, 0)),
            pl.BlockSpec((tm, h * V_DIM), lambda i: (i, 0)),
            pl.BlockSpec((tm, KV_LORA), lambda i: (i, 0)),
            pl.BlockSpec((tm, QK_ROPE), lambda i: (i, 0)),
        ],
        out_shape=[
            jax.ShapeDtypeStruct((t, h * QK_PAD), BF16),
            jax.ShapeDtypeStruct((t, h * QK_PAD), BF16),
            jax.ShapeDtypeStruct((t, h * V_DIM), BF16),
            jax.ShapeDtypeStruct((t, KV_LORA), F32),
            jax.ShapeDtypeStruct((t, QK_ROPE), F32),
        ],
        compiler_params=_params("parallel"),
        name="mla_prompt",
    )(proj, proj, proj, *tabs, q_g.reshape(1, -1), kv_g.reshape(1, -1), wqn, wqp, wuk, wuv)


def _mla_sample_kernel(ql_ref, kvl_ref, kpe_ref, cos_ref, sn_ref, sp_ref, qg_ref, kvg_ref,
                       wqn_ref, wqp_ref, wukt_ref,
                       qa_ref, qp_ref, ckv_ref, kpeo_ref):
    q_nope, q_rope, c_kv, k_pe = _mla_common(ql_ref, kvl_ref, kpe_ref, cos_ref, sn_ref, sp_ref,
                                             qg_ref, kvg_ref, wqn_ref, wqp_ref)
    ckv_ref[...] = c_kv
    kpeo_ref[...] = k_pe[:, :QK_ROPE]
    for h in range(MLA_HEADS):
        qa_ref[h] = _dot(q_nope[:, h * LANE:(h + 1) * LANE].astype(BF16), wukt_ref[h])
        qp_ref[h] = q_rope[h]


def mla_sample(proj, tabs, q_g, kv_g, wqn, wqp, wukt):
    t = proj.shape[0]
    h = MLA_HEADS
    return pl.pallas_call(
        _mla_sample_kernel,
        grid=(1,),
        in_specs=_proj_specs(t, 1) + [
            _full((1, Q_LORA)), _full((1, KV_LORA)),
            _full((Q_LORA, h * LANE)), _full((Q_LORA, h * LANE)),
            _full((h, QK_NOPE, KV_LORA)),
        ],
        out_specs=[
            _full((h, t, KV_LORA)), _full((h, t, LANE)),
            _full((t, KV_LORA)), _full((t, QK_ROPE)),
        ],
        out_shape=[
            jax.ShapeDtypeStruct((h, t, KV_LORA), F32),
            jax.ShapeDtypeStruct((h, t, LANE), F32),
            jax.ShapeDtypeStruct((t, KV_LORA), F32),
            jax.ShapeDtypeStruct((t, QK_ROPE), F32),
        ],
        compiler_params=_params("arbitrary"),
        name="mla_sample",
    )(proj, proj, proj, *tabs, q_g.reshape(1, -1), kv_g.reshape(1, -1), wqn, wqp, wukt)


def _flash_kernel(q_ref, k_ref, v_ref, g_ref, o_ref, *, tq):
    i = pl.program_id(2)
    q = q_ref[...]

    def step(j, carry, masked):
        m, l, acc = carry
        start = pl.multiple_of(j * tq, tq)
        k = k_ref[pl.ds(start, tq), :]
        v = v_ref[pl.ds(start, tq), :]
        s = _dot_nt(q, k)
        if masked:
            rows = lax.broadcasted_iota(jnp.int32, (tq, tq), 0)
            cols = lax.broadcasted_iota(jnp.int32, (tq, tq), 1)
            s = jnp.where(cols <= rows, s, NEG_BIG)
        m_new = jnp.maximum(m, jnp.max(s, axis=-1, keepdims=True))
        alpha = jnp.exp(m - m_new)
        p = jnp.exp(s - m_new)
        l = alpha * l + jnp.sum(p, axis=-1, keepdims=True)
        acc = alpha * acc + _dot(p.astype(BF16), v)
        return m_new, l, acc

    init = (jnp.full((tq, 1), NEG_BIG, F32), jnp.zeros((tq, 1), F32), jnp.zeros((tq, V_DIM), F32))
    carry = lax.fori_loop(0, i, lambda j, c: step(j, c, False), init)
    _, l, acc = step(i, carry, True)
    o_ref[...] = (acc / l * _silu(g_ref[...])).astype(o_ref.dtype)


def flash_prompt(q, k, v, proj, *, bsz, seq, tq):
    n_q = seq // tq
    h = MLA_HEADS
    return pl.pallas_call(
        functools.partial(_flash_kernel, tq=tq),
        grid=(bsz, h, n_q),
        in_specs=[
            pl.BlockSpec((tq, QK_PAD), lambda b, hh, i: (b * n_q + i, hh)),
            pl.BlockSpec((seq, QK_PAD), lambda b, hh, i: (b, hh)),
            pl.BlockSpec((seq, V_DIM), lambda b, hh, i: (b, hh)),
            pl.BlockSpec((tq, V_DIM), lambda b, hh, i: (b * n_q + i, COL_GM // V_DIM + hh)),
        ],
        out_specs=pl.BlockSpec((tq, V_DIM), lambda b, hh, i: (b * n_q + i, hh)),
        out_shape=jax.ShapeDtypeStruct((bsz * seq, h * V_DIM), BF16),
        compiler_params=_params("parallel", "parallel", "arbitrary"),
        name="flash_prompt",
    )(q, k, v, proj)


PAGED_CHUNK = 1024


def _paged_kernel(pt_ref, qa_ref, qp_ref, cn_ref, kn_ref, ckv_hbm, kr_hbm, o_ref,
                  cbuf, kbuf, cbf, s_ref, sems, *, layer, n_pages):
    b = pl.program_id(0)
    nb = pl.num_programs(0)
    slot = b % 2

    def copies(bb, sl, p):
        page = pt_ref[bb * n_pages + p]
        rows = pl.ds(p * PAGE_SIZE, PAGE_SIZE)
        return (pltpu.make_async_copy(ckv_hbm.at[page, layer], cbuf.at[sl, rows], sems.at[0, sl]),
                pltpu.make_async_copy(kr_hbm.at[page, layer], kbuf.at[sl, rows], sems.at[1, sl]))

    def start_all(bb, sl):
        for p in range(n_pages):
            c1, c2 = copies(bb, sl, p)
            c1.start()
            c2.start()

    @pl.when(b == 0)
    def _():
        start_all(b, slot)

    @pl.when(b + 1 < nb)
    def _():
        start_all(b + 1, 1 - slot)

    for p in range(n_pages):
        c1, c2 = copies(b, slot, p)
        c1.wait()
        c2.wait()

    past = n_pages * PAGE_SIZE
    n_chunks = past // PAGED_CHUNK
    qa = qa_ref[0].astype(BF16)
    qp = qp_ref[0][:, :QK_ROPE].astype(BF16)

    def score_chunk(c, carry):
        rows = pl.ds(pl.multiple_of(c * PAGED_CHUNK, PAGED_CHUNK), PAGED_CHUNK)
        cc = cbuf[slot, rows, :].astype(BF16)
        kk = kbuf[slot, rows, :].astype(BF16)
        cbf[rows, :] = cc
        s_ref[c] = _dot_nt(qa, cc) + _dot_nt(qp, kk)
        return carry

    lax.fori_loop(0, n_chunks, score_chunk, 0)

    c_new = cn_ref[0].astype(BF16).astype(F32)
    k_new = kn_ref[0].astype(BF16).astype(F32)
    s_new = (jnp.sum(qa.astype(F32) * c_new, axis=-1, keepdims=True)
             + jnp.sum(qp.astype(F32) * k_new, axis=-1, keepdims=True))
    s = s_ref[...]
    m = jnp.max(jnp.max(s, axis=-1, keepdims=True), axis=0)
    m = jnp.maximum(m, s_new)
    p = jnp.exp(s - m[None])
    p_new = jnp.exp(s_new - m)
    l = jnp.sum(jnp.sum(p, axis=-1, keepdims=True), axis=0) + p_new
    s_ref[...] = p

    def pv_chunk(c, acc):
        rows = pl.ds(pl.multiple_of(c * PAGED_CHUNK, PAGED_CHUNK), PAGED_CHUNK)
        return acc + _dot(s_ref[c].astype(BF16), cbf[rows, :])

    acc = lax.fori_loop(0, n_chunks, pv_chunk, jnp.zeros((MLA_HEADS, KV_LORA), F32))
    acc = acc + p_new.astype(BF16).astype(F32) * c_new
    o_ref[0] = acc / l


def paged_attention(page_table, qa, qp, c_new, k_new, cache_kv, cache_kr, *, layer):
    bsz, n_pages = page_table.shape
    past = n_pages * PAGE_SIZE
    h = MLA_HEADS
    grid_spec = pltpu.PrefetchScalarGridSpec(
        num_scalar_prefetch=1,
        grid=(bsz,),
        in_specs=[
            pl.BlockSpec((1, h, KV_LORA), lambda b, pt: (b, 0, 0)),
            pl.BlockSpec((1, h, LANE), lambda b, pt: (b, 0, 0)),
            pl.BlockSpec((1, 1, KV_LORA), lambda b, pt: (b, 0, 0)),
            pl.BlockSpec((1, 1, QK_ROPE), lambda b, pt: (b, 0, 0)),
            pl.BlockSpec(memory_space=pl.ANY),
            pl.BlockSpec(memory_space=pl.ANY),
        ],
        out_specs=pl.BlockSpec((1, h, KV_LORA), lambda b, pt: (b, 0, 0)),
        scratch_shapes=[
            pltpu.VMEM((2, past, KV_LORA), F32),
            pltpu.VMEM((2, past, QK_ROPE), F32),
            pltpu.VMEM((past, KV_LORA), BF16),
            pltpu.VMEM((past // PAGED_CHUNK, h, PAGED_CHUNK), F32),
            pltpu.SemaphoreType.DMA((2, 2)),
        ],
    )
    return pl.pallas_call(
        functools.partial(_paged_kernel, layer=layer, n_pages=n_pages),
        grid_spec=grid_spec,
        out_shape=jax.ShapeDtypeStruct((bsz, h, KV_LORA), F32),
        compiler_params=_params("arbitrary"),
        name="paged_attention",
    )(page_table.reshape(-1), qa, qp, c_new, k_new, cache_kv, cache_kr)


def _uv_kernel(ol_ref, wuv_ref, g_ref, o_ref):
    for h in range(MLA_HEADS):
        o = _dot(ol_ref[h].astype(BF16), wuv_ref[h])
        sl = slice(h * V_DIM, (h + 1) * V_DIM)
        o_ref[:, sl] = (o * _silu(g_ref[:, sl])).astype(o_ref.dtype)


def uv_project(o_lat, wuv3, proj):
    h, t, _ = o_lat.shape
    return pl.pallas_call(
        _uv_kernel,
        grid=(1,),
        in_specs=[
            _full((h, t, KV_LORA)), _full((h, KV_LORA, V_DIM)),
            pl.BlockSpec((t, MLA_WIDTH), lambda i: (0, COL_GM // MLA_WIDTH)),
        ],
        out_specs=_full((t, MLA_WIDTH)),
        out_shape=jax.ShapeDtypeStruct((t, MLA_WIDTH), BF16),
        compiler_params=_params("arbitrary"),
        name="uv_project",
    )(o_lat, wuv3, proj)


def _proj_norm_res_kernel(*refs, n_in):
    ins, ws = refs[:n_in], refs[n_in:2 * n_in]
    x_ref, g_ref, o_ref = refs[2 * n_in:]
    y = _dot(ins[0][...], ws[0][...])
    for a, w in zip(ins[1:], ws[1:]):
        y = y + _dot(a[...], w[...])
    o_ref[...] = x_ref[...] + _rms(y, g_ref[...])


def proj_norm_res(ins, ws, x, g, *, tm, name):
    t, d = x.shape
    n_in = len(ins)
    return pl.pallas_call(
        functools.partial(_proj_norm_res_kernel, n_in=n_in),
        grid=(t // tm,),
        in_specs=([pl.BlockSpec((tm, a.shape[1]), lambda i: (i, 0)) for a in ins]
                  + [_full(w.shape) for w in ws]
                  + [pl.BlockSpec((tm, d), lambda i: (i, 0)), _full((1, d))]),
        out_specs=pl.BlockSpec((tm, d), lambda i: (i, 0)),
        out_shape=jax.ShapeDtypeStruct((t, d), F32),
        compiler_params=_params("parallel"),
        name=name,
    )(*ins, *ws, x, g.reshape(1, d))


def _cross_prompt_kernel(x_ref, gpre_ref, wq_ref, mk_ref, mv_ref, wo_ref, gpost_ref, o_ref):
    x = x_ref[...]
    hn = _rms(x, gpre_ref[...]).astype(BF16)
    q = (_dot(hn, wq_ref[...]) * MEM_SCALE).astype(BF16)
    outs = []
    for h in range(MEM_HEADS):
        sl = slice(h * MEM_HEAD_DIM, (h + 1) * MEM_HEAD_DIM)
        s = _dot_nt(q[:, sl], mk_ref[:, sl].astype(BF16))
        m = jnp.max(s, axis=-1, keepdims=True)
        p = jnp.exp(s - m)
        l = jnp.sum(p, axis=-1, keepdims=True)
        outs.append((_dot(p.astype(BF16), mv_ref[:, sl].astype(BF16)) / l).astype(BF16))
    o = jnp.concatenate(outs, axis=-1)
    y = _dot(o, wo_ref[...])
    o_ref[...] = x + _rms(y, gpost_ref[...])


def cross_prompt(x, g_pre, wq, mem_kv, wo, g_post, *, seq, tm):
    t, d = x.shape
    per_seq = seq // tm
    return pl.pallas_call(
        _cross_prompt_kernel,
        grid=(t // tm,),
        in_specs=[
            pl.BlockSpec((tm, d), lambda i: (i, 0)),
            _full((1, d)), _full((d, MEM_WIDTH)),
            pl.BlockSpec((N_MEM, MEM_WIDTH), lambda i: (i // per_seq, 0)),
            pl.BlockSpec((N_MEM, MEM_WIDTH), lambda i: (i // per_seq, 1)),
            _full((MEM_WIDTH, d)), _full((1, d)),
        ],
        out_specs=pl.BlockSpec((tm, d), lambda i: (i, 0)),
        out_shape=jax.ShapeDtypeStruct((t, d), F32),
        compiler_params=_params("parallel"),
        name="cross_prompt",
    )(x, g_pre.reshape(1, d), wq, mem_kv, mem_kv, wo, g_post.reshape(1, d))


def _cross_sample_kernel(q_ref, mk_ref, mv_ref, o_ref):
    bt = q_ref.shape[0]
    q = (q_ref[...] * MEM_SCALE).astype(BF16)
    for h in range(MEM_HEADS):
        sl = slice(h * MEM_HEAD_DIM, (h + 1) * MEM_HEAD_DIM)
        qh = jnp.broadcast_to(q[:, :, sl], (bt, 8, MEM_HEAD_DIM))
        kh = mk_ref[:, 0, :, sl].astype(BF16)
        vh = mv_ref[:, 0, :, sl].astype(BF16)
        s = jnp.einsum("bqe,bme->bqm", qh, kh, preferred_element_type=F32)
        m = jnp.max(s, axis=-1, keepdims=True)
        p = jnp.exp(s - m)
        l = jnp.sum(p, axis=-1, keepdims=True)
        o = jnp.einsum("bqm,bme->bqe", p.astype(BF16), vh, preferred_element_type=F32) / l
        o_ref[:, :, sl] = o[:, 0:1, :].astype(o_ref.dtype)


def cross_sample_core(q3, mem_k, mem_v, *, layer, bt):
    bsz = q3.shape[0]
    mem = lambda: pl.BlockSpec((bt, 1, N_MEM, MEM_WIDTH), lambda i: (i, layer, 0, 0))
    return pl.pallas_call(
        _cross_sample_kernel,
        grid=(bsz // bt,),
        in_specs=[pl.BlockSpec((bt, 1, MEM_WIDTH), lambda i: (i, 0, 0)), mem(), mem()],
        out_specs=pl.BlockSpec((bt, 1, MEM_WIDTH), lambda i: (i, 0, 0)),
        out_shape=jax.ShapeDtypeStruct((bsz, 1, MEM_WIDTH), BF16),
        compiler_params=_params("parallel"),
        name="cross_sample",
    )(q3, mem_k, mem_v)


def _rope_tables(pos):
    half = QK_ROPE // 2
    inv_freq = ROPE_BASE ** (-jnp.arange(half, dtype=F32) / half)
    ang = pos.astype(F32)[:, None] * inv_freq[None, :]
    cos, sin = jnp.cos(ang), jnp.sin(ang)
    z = jnp.zeros_like(cos)
    return (jnp.concatenate([cos, cos, z, z], axis=-1),
            jnp.concatenate([-sin, z, z, z], axis=-1),
            jnp.concatenate([z, sin, z, z], axis=-1))


def _prep_layer(l, w_in, w_uq, w_uk, w_uv, w_out, w_xq, w_mk, w_mv, w_xo):
    d = w_in.shape[1]
    wi = w_in[l]
    c0 = 3 * CONV_CH
    q_l = wi[:, c0:c0 + Q_LORA]
    kv_l = wi[:, c0 + Q_LORA:c0 + Q_LORA + KV_LORA]
    kpe = wi[:, c0 + Q_LORA + KV_LORA:c0 + Q_LORA + KV_LORA + QK_ROPE]
    g_m = wi[:, c0 + Q_LORA + KV_LORA + QK_ROPE:]
    pad = jnp.zeros((d, PROJ_COLS - COL_KPE - QK_ROPE), wi.dtype)
    w_in_r = jnp.concatenate([wi[:, :c0], g_m, q_l, kv_l, kpe, pad], axis=1).astype(BF16)
    uq = w_uq[l]
    wqn = uq[:, :, :QK_NOPE].reshape(Q_LORA, -1).astype(BF16)
    wqp = jnp.pad(uq[:, :, QK_NOPE:], ((0, 0), (0, 0), (0, LANE - QK_ROPE))).reshape(Q_LORA, -1).astype(BF16)
    return dict(
        w_in=w_in_r, wqn=wqn, wqp=wqp,
        wuk=w_uk[l].reshape(KV_LORA, -1).astype(BF16),
        wuv=w_uv[l].reshape(KV_LORA, -1).astype(BF16),
        wukt=w_uk[l].transpose(1, 2, 0).astype(BF16),
        wuv3=w_uv[l].transpose(1, 0, 2).astype(BF16),
        w_out_c=w_out[l][:CONV_CH].astype(BF16),
        w_out_m=w_out[l][CONV_CH:].astype(BF16),
        w_xq=w_xq[l].reshape(d, -1).astype(BF16),
        w_mkv=jnp.concatenate([w_mk[l].reshape(d, -1), w_mv[l].reshape(d, -1)], axis=1).astype(BF16),
        w_xo=w_xo[l].reshape(-1, d).astype(BF16),
    )


def kernel(x_prompt, x_sample, mem_prompt, cache_kv_latent, cache_k_rope, state_conv, cache_mem_k, cache_mem_v, page_table, norm_mix_pre, w_in, conv_w, conv_b, conv_ln_g, conv_ln_b, q_norm_g, w_uq, kv_norm_g, w_uk, w_uv, w_out, norm_mix_post, norm_x_pre, norm_mem, w_xq, w_mk, w_mv, w_xo, norm_x_post):
    b_p, s_p, d = x_prompt.shape
    b_s = x_sample.shape[0]
    depth = w_in.shape[0]
    n_pages = page_table.shape[1]
    t_p = b_p * s_p
    tm_p = min(512, s_p)
    tq = min(256, s_p)

    tabs_p = _rope_tables(jnp.arange(s_p, dtype=jnp.int32))
    tabs_s = _rope_tables(jnp.full((b_s,), n_pages * PAGE_SIZE, jnp.int32))
    mem2 = mem_prompt.reshape(b_p * N_MEM, d)
    mem_k_s = cache_mem_k.reshape(b_s, depth, N_MEM, MEM_WIDTH)
    mem_v_s = cache_mem_v.reshape(b_s, depth, N_MEM, MEM_WIDTH)

    xp = x_prompt.reshape(t_p, d)
    xs = x_sample.reshape(b_s, d)
    lat_p, kpe_p, conv_p, mk_p, mv_p, lat_s, kpe_s = [], [], [], [], [], [], []
    u_s = []
    for l in range(depth):
        w = _prep_layer(l, w_in, w_uq, w_uk, w_uv, w_out, w_xq, w_mk, w_mv, w_xo)

        proj = rms_matmul(xp, norm_mix_pre[l], w["w_in"], tm=tm_p, tn=1024, name="in_proj_p")
        conv_out, conv_state = conv_prompt(proj, conv_w[l], conv_b[l], conv_ln_g[l], conv_ln_b[l],
                                           bsz=b_p, seq=s_p, ts=tm_p)
        q, k, v, c_kv, k_pe = mla_prompt(proj, tabs_p, q_norm_g[l], kv_norm_g[l],
                                         w["wqn"], w["wqp"], w["wuk"], w["wuv"], tm=tm_p)
        attn = flash_prompt(q, k, v, proj, bsz=b_p, seq=s_p, tq=tq)
        xp = proj_norm_res([conv_out, attn], [w["w_out_c"], w["w_out_m"]], xp, norm_mix_post[l],
                           tm=tm_p, name="out_proj_p")
        mem_kv = rms_matmul(mem2, norm_mem[l], w["w_mkv"], tm=N_MEM, tn=2 * MEM_WIDTH, name="mem_kv")
        xp = cross_prompt(xp, norm_x_pre[l], w["w_xq"], mem_kv, w["w_xo"], norm_x_post[l],
                          seq=s_p, tm=tm_p)
        lat_p.append(c_kv.reshape(b_p, s_p, KV_LORA))
        kpe_p.append(k_pe.reshape(b_p, s_p, QK_ROPE))
        conv_p.append(conv_state)
        mk_p.append(mem_kv[:, :MEM_WIDTH].reshape(b_p, N_MEM, MEM_HEADS, MEM_HEAD_DIM))
        mv_p.append(mem_kv[:, MEM_WIDTH:].reshape(b_p, N_MEM, MEM_HEADS, MEM_HEAD_DIM))

        proj_s = rms_matmul(xs, norm_mix_pre[l], w["w_in"], tm=b_s, tn=1024, name="in_proj_s")
        conv_out_s, u_l = conv_sample(proj_s.reshape(b_s, 1, PROJ_COLS), state_conv, l,
                                      conv_w[l], conv_b[l], conv_ln_g[l], conv_ln_b[l], bt=8)
        u_s.append(u_l)
        qa, qp, c_new, k_new = mla_sample(proj_s, tabs_s, q_norm_g[l], kv_norm_g[l],
                                          w["wqn"], w["wqp"], w["wukt"])
        o_lat = paged_attention(page_table, qa.transpose(1, 0, 2), qp.transpose(1, 0, 2),
                                c_new.reshape(b_s, 1, KV_LORA), k_new.reshape(b_s, 1, QK_ROPE),
                                cache_kv_latent, cache_k_rope, layer=l)
        attn_s = uv_project(o_lat.transpose(1, 0, 2), w["wuv3"], proj_s)
        xs = proj_norm_res([conv_out_s.reshape(b_s, CONV_CH), attn_s], [w["w_out_c"], w["w_out_m"]],
                           xs, norm_mix_post[l], tm=b_s, name="out_proj_s")
        q_x = rms_matmul(xs, norm_x_pre[l], w["w_xq"], tm=b_s, tn=MEM_WIDTH, name="cross_q_s")
        o_x = cross_sample_core(q_x.reshape(b_s, 1, MEM_WIDTH), mem_k_s, mem_v_s, layer=l, bt=8)
        xs = proj_norm_res([o_x.reshape(b_s, MEM_WIDTH)], [w["w_xo"]], xs, norm_x_post[l],
                           tm=b_s, name="cross_out_s")
        lat_s.append(c_new.reshape(b_s, 1, KV_LORA))
        kpe_s.append(k_new.reshape(b_s, 1, QK_ROPE))

    return (xp.reshape(b_p, s_p, d), xs.reshape(b_s, 1, d),
            jnp.stack(lat_p, axis=1), jnp.stack(kpe_p, axis=1), jnp.stack(conv_p, axis=1),
            jnp.stack(mk_p, axis=1), jnp.stack(mv_p, axis=1),
            jnp.stack(lat_s, axis=1), jnp.stack(kpe_s, axis=1),
            conv_state_update(state_conv, jnp.stack(u_s, axis=1), bt=8))
```

```python
import functools

import jax
import jax.numpy as jnp
from jax import lax
from jax.experimental import pallas as pl
from jax.experimental.pallas import tpu as pltpu

F32 = jnp.float32
BF16 = jnp.bfloat16

D_MODEL = 2048
CONV_CH = 1024
CONV_WIDTH = 31
CONV_STATE = CONV_WIDTH - 1
MLA_HEADS = 8
QK_NOPE = 128
QK_ROPE = 64
V_DIM = 128
MLA_WIDTH = MLA_HEADS * V_DIM
Q_LORA = 512
KV_LORA = 256
MLA_SCALE = (QK_NOPE + QK_ROPE) ** -0.5
ROPE_BASE = 10000.0
PAGE_SIZE = 128
N_MEM = 256
MEM_HEADS = 4
MEM_HEAD_DIM = 128
MEM_WIDTH = MEM_HEADS * MEM_HEAD_DIM
MEM_SCALE = MEM_HEAD_DIM ** -0.5
EPS = 1e-6
LOG2E = 1.4426950408889634

LANE = 128
QK_PAD = 256
NEG_BIG = -1e30

PROJ_COLS = 5120
COL_A, COL_B, COL_GC, COL_GM, COL_QL, COL_KV, COL_KPE = 0, 1024, 2048, 3072, 4096, 4608, 4864

VMEM_LIMIT = 52 * 1024 * 1024


def _params(*sem):
    return pltpu.CompilerParams(dimension_semantics=sem, vmem_limit_bytes=VMEM_LIMIT)


def _rms(x, g):
    return x * lax.rsqrt(jnp.mean(x * x, axis=-1, keepdims=True) + EPS) * g


def _silu(x):
    return x * jax.nn.sigmoid(x)


def _dot(a, b):
    return jnp.dot(a, b, preferred_element_type=F32)


def _dot_nt(a, b):
    return lax.dot_general(a, b, (((1,), (1,)), ((), ())), preferred_element_type=F32)


def _rope128(x, cos_t, sin_n, sin_p):
    return x * cos_t + pltpu.roll(x, 96, 1) * sin_n + pltpu.roll(x, 32, 1) * sin_p


def _rms_matmul_kernel(x_ref, g_ref, w_ref, o_ref, hn_ref):
    @pl.when(pl.program_id(1) == 0)
    def _():
        hn_ref[...] = _rms(x_ref[...], g_ref[...]).astype(BF16)

    o_ref[...] = _dot(hn_ref[...], w_ref[...]).astype(o_ref.dtype)


def rms_matmul(x, g, w, *, tm, tn, name):
    t, k = x.shape
    n = w.shape[1]
    return pl.pallas_call(
        _rms_matmul_kernel,
        grid=(t // tm, n // tn),
        in_specs=[
            pl.BlockSpec((tm, k), lambda i, j: (i, 0)),
            pl.BlockSpec((1, k), lambda i, j: (0, 0)),
            pl.BlockSpec((k, tn), lambda i, j: (0, j)),
        ],
        out_specs=pl.BlockSpec((tm, tn), lambda i, j: (i, j)),
        out_shape=jax.ShapeDtypeStruct((t, n), F32),
        scratch_shapes=[pltpu.VMEM((tm, k), BF16)],
        compiler_params=_params("parallel", "arbitrary"),
        name=name,
    )(x, g.reshape(1, k), w)


CONV_ROWS = 32
CONV_HALO = 32
CONV_LANES = 512
CONV_WIN = CONV_ROWS + CONV_HALO + 8
CONV_SHIFT = CONV_HALO - CONV_STATE


def _conv_prompt_kernel(a_ref, b_ref, gc_ref, w_ref, cb_ref, lg_ref, lb_ref,
                        out_ref, state_ref, uext_ref, *, ts):
    s = pl.program_id(1)

    @pl.when(s == 0)
    def _():
        uext_ref[0:CONV_HALO, :] = jnp.zeros((CONV_HALO, CONV_CH), F32)
        uext_ref[CONV_HALO + ts:, :] = jnp.zeros((CONV_WIN - CONV_ROWS - CONV_HALO, CONV_CH), F32)

    @pl.when(s > 0)
    def _():
        uext_ref[0:CONV_HALO, :] = uext_ref[ts:ts + CONV_HALO, :]

    uext_ref[CONV_HALO:CONV_HALO + ts, :] = a_ref[...] * jax.nn.sigmoid(b_ref[...])

    @pl.when(s == pl.num_programs(1) - 1)
    def _():
        state_ref[0] = uext_ref[CONV_HALO + ts - CONV_STATE:CONV_HALO + ts, :]

    def chunk(r, carry):
        r0 = pl.multiple_of(r * CONV_ROWS, CONV_ROWS)
        accs = []
        for c0 in range(0, CONV_CH, CONV_LANES):
            cols = slice(c0, c0 + CONV_LANES)
            win = uext_ref[pl.ds(r0, CONV_WIN), cols]
            acc = jnp.zeros((CONV_ROWS, CONV_LANES), F32) + cb_ref[:, cols]
            for j in range(8):
                v = win if j == 0 else pltpu.roll(win, CONV_WIN - j, 0)
                for o in range(CONV_SHIFT + (j - CONV_SHIFT) % 8, CONV_SHIFT + CONV_WIDTH, 8):
                    k = o - CONV_SHIFT
                    acc = acc + w_ref[k:k + 1, cols] * v[o - j:o - j + CONV_ROWS, :]
            accs.append(acc)
        mu = sum(jnp.sum(a, axis=-1, keepdims=True) for a in accs) * (1.0 / CONV_CH)
        ds = [a - mu for a in accs]
        var = sum(jnp.sum(d * d, axis=-1, keepdims=True) for d in ds) * (1.0 / CONV_CH)
        inv = lax.rsqrt(var + EPS)
        for i, d in enumerate(ds):
            cols = slice(i * CONV_LANES, (i + 1) * CONV_LANES)
            y = d * inv * lg_ref[:, cols] + lb_ref[:, cols]
            gate = gc_ref[pl.ds(r0, CONV_ROWS), cols]
            out_ref[pl.ds(r0, CONV_ROWS), cols] = (_silu(y) * _silu(gate)).astype(out_ref.dtype)
        return carry

    lax.fori_loop(0, ts // CONV_ROWS, chunk, 0)


def conv_prompt(proj, conv_w, conv_b, ln_g, ln_b, *, bsz, seq, ts):
    n_s = seq // ts
    row = lambda b, s: b * n_s + s
    vec = lambda: pl.BlockSpec((1, CONV_CH), lambda b, s: (0, 0))
    return pl.pallas_call(
        functools.partial(_conv_prompt_kernel, ts=ts),
        grid=(bsz, n_s),
        in_specs=[
            pl.BlockSpec((ts, CONV_CH), lambda b, s: (row(b, s), COL_A // CONV_CH)),
            pl.BlockSpec((ts, CONV_CH), lambda b, s: (row(b, s), COL_B // CONV_CH)),
            pl.BlockSpec((ts, CONV_CH), lambda b, s: (row(b, s), COL_GC // CONV_CH)),
            pl.BlockSpec((CONV_WIDTH, CONV_CH), lambda b, s: (0, 0)),
            vec(), vec(), vec(),
        ],
        out_specs=[
            pl.BlockSpec((ts, CONV_CH), lambda b, s: (row(b, s), 0)),
            pl.BlockSpec((1, CONV_STATE, CONV_CH), lambda b, s: (b, 0, 0)),
        ],
        out_shape=[
            jax.ShapeDtypeStruct((bsz * seq, CONV_CH), BF16),
            jax.ShapeDtypeStruct((bsz, CONV_STATE, CONV_CH), F32),
        ],
        scratch_shapes=[pltpu.VMEM((ts + CONV_WIN - CONV_ROWS, CONV_CH), F32)],
        compiler_params=_params("parallel", "arbitrary"),
        name="conv_prompt",
    )(proj, proj, proj, conv_w, conv_b.reshape(1, -1), ln_g.reshape(1, -1), ln_b.reshape(1, -1))


def _conv_state_kernel(st_ref, w_ref, newst_ref, part_ref):
    st = st_ref[...]
    newst_ref[:, 0:CONV_STATE - 1] = st[:, 1:CONV_STATE]
    newst_ref[:, CONV_STATE - 1:CONV_STATE] = jnp.zeros_like(st[:, 0:1])
    part_ref[...] = jnp.sum(st * w_ref[0:CONV_STATE][None], axis=1)


def conv_state_shift(state_t, conv_w_t, *, bt):
    bsz, _, depth, _ = state_t.shape
    blk = pl.BlockSpec((bt, CONV_STATE, depth, CONV_CH), lambda i: (i, 0, 0, 0))
    return pl.pallas_call(
        _conv_state_kernel,
        grid=(bsz // bt,),
        in_specs=[blk, pl.BlockSpec((CONV_WIDTH, depth, CONV_CH), lambda i: (0, 0, 0))],
        out_specs=[blk, pl.BlockSpec((bt, depth, CONV_CH), lambda i: (i, 0, 0))],
        out_shape=[jax.ShapeDtypeStruct(state_t.shape, F32),
                   jax.ShapeDtypeStruct((bsz, depth, CONV_CH), F32)],
        compiler_params=_params("parallel"),
        name="conv_state_shift",
    )(state_t, conv_w_t)


def _conv_state_finish_kernel(u_ref, st_hbm, o_ref):
    del st_hbm
    o_ref[...] = u_ref[...]


def conv_state_finish(new_state, u_all):
    bsz, _, depth, _ = new_state.shape
    return pl.pallas_call(
        _conv_state_finish_kernel,
        grid=(1,),
        in_specs=[pl.BlockSpec((bsz, 1, depth, CONV_CH), lambda i: (0, 0, 0, 0)),
                  pl.BlockSpec(memory_space=pl.ANY)],
        out_specs=pl.BlockSpec((bsz, 1, depth, CONV_CH), lambda i: (0, CONV_STATE - 1, 0, 0)),
        out_shape=jax.ShapeDtypeStruct(new_state.shape, F32),
        input_output_aliases={1: 0},
        compiler_params=_params("arbitrary"),
        name="conv_state_finish",
    )(u_all, new_state)


def _conv_sample_kernel(a_ref, b_ref, gc_ref, part_ref, w_ref, cb_ref, lg_ref, lb_ref, out_ref, u_ref, *, layer):
    u = a_ref[...] * jax.nn.sigmoid(b_ref[...])
    u_ref[...] = u
    y = part_ref[:, layer, :] + u * w_ref[CONV_STATE:CONV_WIDTH, :] + cb_ref[...]
    mu = jnp.mean(y, axis=-1, keepdims=True)
    d = y - mu
    var = jnp.mean(d * d, axis=-1, keepdims=True)
    z = d * lax.rsqrt(var + EPS) * lg_ref[...] + lb_ref[...]
    out_ref[...] = (_silu(z) * _silu(gc_ref[...])).astype(out_ref.dtype)


def conv_sample(proj, part, layer, conv_w, conv_b, ln_g, ln_b):
    bsz = proj.shape[0]
    vec = lambda: pl.BlockSpec((1, CONV_CH), lambda i: (0, 0))
    row = lambda: pl.BlockSpec((bsz, CONV_CH), lambda i: (0, 0))
    return pl.pallas_call(
        functools.partial(_conv_sample_kernel, layer=layer),
        grid=(1,),
        in_specs=[
            pl.BlockSpec((bsz, CONV_CH), lambda i: (0, COL_A // CONV_CH)),
            pl.BlockSpec((bsz, CONV_CH), lambda i: (0, COL_B // CONV_CH)),
            pl.BlockSpec((bsz, CONV_CH), lambda i: (0, COL_GC // CONV_CH)),
            pl.BlockSpec(part.shape, lambda i: (0, 0, 0)),
            pl.BlockSpec((CONV_WIDTH, CONV_CH), lambda i: (0, 0)),
            vec(), vec(), vec(),
        ],
        out_specs=[row(), row()],
        out_shape=[
            jax.ShapeDtypeStruct((bsz, CONV_CH), BF16),
            jax.ShapeDtypeStruct((bsz, CONV_CH), F32),
        ],
        compiler_params=_params("arbitrary"),
        name="conv_sample",
    )(proj, proj, proj, part, conv_w, conv_b.reshape(1, -1), ln_g.reshape(1, -1), ln_b.reshape(1, -1))


def _mla_common(ql_ref, kvl_ref, kpe_ref, cos_ref, sn_ref, sp_ref, qg_ref, kvg_ref, wqn_ref, wqp_ref):
    qn = _rms(ql_ref[...], qg_ref[...]).astype(BF16)
    q_nope = _dot(qn, wqn_ref[...]) * (MLA_SCALE * LOG2E)
    q_rope_raw = _dot(qn, wqp_ref[...]) * (MLA_SCALE * LOG2E)
    cos_t, sin_n, sin_p = cos_ref[...], sn_ref[...], sp_ref[...]
    q_rope = [_rope128(q_rope_raw[:, h * LANE:(h + 1) * LANE], cos_t, sin_n, sin_p)
              for h in range(MLA_HEADS)]
    c_kv = _rms(kvl_ref[...], kvg_ref[...])
    k_pe = _rope128(kpe_ref[...], cos_t, sin_n, sin_p)
    return q_nope, q_rope, c_kv, k_pe


def _mla_prompt_kernel(ql_ref, kvl_ref, kpe_ref, cos_ref, sn_ref, sp_ref, qg_ref, kvg_ref,
                       wqn_ref, wqp_ref, wuk_ref, wuv_ref,
                       q_ref, k_ref, v_ref, ckv_ref, kpeo_ref):
    q_nope, q_rope, c_kv, k_pe = _mla_common(ql_ref, kvl_ref, kpe_ref, cos_ref, sn_ref, sp_ref,
                                             qg_ref, kvg_ref, wqn_ref, wqp_ref)
    ckv_ref[...] = c_kv
    kpeo_ref[...] = k_pe[:, :QK_ROPE]
    c_bf = c_kv.astype(BF16)
    k_nope = _dot(c_bf, wuk_ref[...])
    v_ref[...] = _dot(c_bf, wuv_ref[...]).astype(BF16)
    k_pe_bf = k_pe.astype(BF16)
    for h in range(MLA_HEADS):
        lo = h * QK_PAD
        q_ref[:, lo:lo + LANE] = q_nope[:, h * LANE:(h + 1) * LANE].astype(BF16)
        q_ref[:, lo + LANE:lo + QK_PAD] = q_rope[h].astype(BF16)
        k_ref[:, lo:lo + LANE] = k_nope[:, h * LANE:(h + 1) * LANE].astype(BF16)
        k_ref[:, lo + LANE:lo + QK_PAD] = k_pe_bf


def _proj_specs(tm, n_tab):
    return [
        pl.BlockSpec((tm, Q_LORA), lambda i: (i, COL_QL // Q_LORA)),
        pl.BlockSpec((tm, KV_LORA), lambda i: (i, COL_KV // KV_LORA)),
        pl.BlockSpec((tm, LANE), lambda i: (i, COL_KPE // LANE)),
        pl.BlockSpec((tm, LANE), lambda i: (i % n_tab, 0)),
        pl.BlockSpec((tm, LANE), lambda i: (i % n_tab, 0)),
        pl.BlockSpec((tm, LANE), lambda i: (i % n_tab, 0)),
    ]


def _full(shape):
    return pl.BlockSpec(shape, lambda i: (0,) * len(shape))


def mla_prompt(proj, tabs, q_g, kv_g, wqn, wqp, wuk, wuv, *, tm):
    t = proj.shape[0]
    n_tab = tabs[0].shape[0] // tm
    h = MLA_HEADS
    return pl.pallas_call(
        _mla_prompt_kernel,
        grid=(t // tm,),
        in_specs=_proj_specs(tm, n_tab) + [
            _full((1, Q_LORA)), _full((1, KV_LORA)),
            _full((Q_LORA, h * LANE)), _full((Q_LORA, h * LANE)),
            _full((KV_LORA, h * LANE)), _full((KV_LORA, h * LANE)),
        ],
        out_specs=[
            pl.BlockSpec((tm, h * QK_PAD), lambda i: (i, 0)),
            pl.BlockSpec((tm, h * QK_PAD), lambda i: (i, 0)),
            pl.BlockSpec((tm, h * V_DIM), lambda i: (i, 0)),
            pl.BlockSpec((tm, KV_LORA), lambda i: (i, 0)),
            pl.BlockSpec((tm, QK_ROPE), lambda i: (i, 0)),
        ],
        out_shape=[
            jax.ShapeDtypeStruct((t, h * QK_PAD), BF16),
            jax.ShapeDtypeStruct((t, h * QK_PAD), BF16),
            jax.ShapeDtypeStruct((t, h * V_DIM), BF16),
            jax.ShapeDtypeStruct((t, KV_LORA), F32),
            jax.ShapeDtypeStruct((t, QK_ROPE), F32),
        ],
        compiler_params=_params("parallel"),
        name="mla_prompt",
    )(proj, proj, proj, *tabs, q_g.reshape(1, -1), kv_g.reshape(1, -1), wqn, wqp, wuk, wuv)


def _mla_sample_kernel(ql_ref, kvl_ref, kpe_ref, cos_ref, sn_ref, sp_ref, qg_ref, kvg_ref,
                       wqn_ref, wqp_ref, wukt_ref,
                       qa_ref, qp_ref, ckv_ref, kpeo_ref):
    q_nope, q_rope, c_kv, k_pe = _mla_common(ql_ref, kvl_ref, kpe_ref, cos_ref, sn_ref, sp_ref,
                                             qg_ref, kvg_ref, wqn_ref, wqp_ref)
    ckv_ref[...] = c_kv
    kpeo_ref[...] = k_pe[:, :QK_ROPE]
    for h in range(MLA_HEADS):
        qa_ref[h] = _dot(q_nope[:, h * LANE:(h + 1) * LANE].astype(BF16), wukt_ref[h])
        qp_ref[h] = q_rope[h]


def mla_sample(proj, tabs, q_g, kv_g, wqn, wqp, wukt):
    t = proj.shape[0]
    h = MLA_HEADS
    return pl.pallas_call(
        _mla_sample_kernel,
        grid=(1,),
        in_specs=_proj_specs(t, 1) + [
            _full((1, Q_LORA)), _full((1, KV_LORA)),
            _full((Q_LORA, h * LANE)), _full((Q_LORA, h * LANE)),
            _full((h, QK_NOPE, KV_LORA)),
        ],
        out_specs=[
            _full((h, t, KV_LORA)), _full((h, t, LANE)),
            _full((t, KV_LORA)), _full((t, QK_ROPE)),
        ],
        out_shape=[
            jax.ShapeDtypeStruct((h, t, KV_LORA), F32),
            jax.ShapeDtypeStruct((h, t, LANE), F32),
            jax.ShapeDtypeStruct((t, KV_LORA), F32),
            jax.ShapeDtypeStruct((t, QK_ROPE), F32),
        ],
        compiler_params=_params("arbitrary"),
        name="mla_sample",
    )(proj, proj, proj, *tabs, q_g.reshape(1, -1), kv_g.reshape(1, -1), wqn, wqp, wukt)


FLASH_TQ = 256


def _flash_kernel(q_ref, k_ref, v_ref, g_ref, o_ref, s_ref, *, seq):
    tq = FLASH_TQ
    n_lane = tq // LANE
    causal = (lax.broadcasted_iota(jnp.int32, (tq, tq), 1)
              <= lax.broadcasted_iota(jnp.int32, (tq, tq), 0))
    for i in range(seq // tq):
        q = q_ref[i * tq:(i + 1) * tq, :]
        mx = jnp.full((tq, LANE), NEG_BIG, F32)
        for j in range(i + 1):
            s = _dot_nt(q, k_ref[j * tq:(j + 1) * tq, :])
            if j == i:
                s = jnp.where(causal, s, NEG_BIG)
            s_ref[:, j * tq:(j + 1) * tq] = s
            for c in range(n_lane):
                mx = jnp.maximum(mx, s[:, c * LANE:(c + 1) * LANE])
        m = jnp.max(mx, axis=-1, keepdims=True)
        lsum = jnp.zeros((tq, LANE), F32)
        acc = jnp.zeros((tq, V_DIM), F32)
        for j in range(i + 1):
            p = jnp.exp2(s_ref[:, j * tq:(j + 1) * tq] - m)
            for c in range(n_lane):
                lsum = lsum + p[:, c * LANE:(c + 1) * LANE]
            acc = acc + _dot(p.astype(BF16), v_ref[j * tq:(j + 1) * tq, :])
        l = jnp.sum(lsum, axis=-1, keepdims=True)
        gate = _silu(g_ref[i * tq:(i + 1) * tq, :])
        o_ref[i * tq:(i + 1) * tq, :] = (acc / l * gate).astype(o_ref.dtype)


def flash_prompt(q, k, v, proj, *, bsz, seq):
    h = MLA_HEADS
    return pl.pallas_call(
        functools.partial(_flash_kernel, seq=seq),
        grid=(bsz, h),
        in_specs=[
            pl.BlockSpec((seq, QK_PAD), lambda b, hh: (b, hh)),
            pl.BlockSpec((seq, QK_PAD), lambda b, hh: (b, hh)),
            pl.BlockSpec((seq, V_DIM), lambda b, hh: (b, hh)),
            pl.BlockSpec((seq, V_DIM), lambda b, hh: (b, COL_GM // V_DIM + hh)),
        ],
        out_specs=pl.BlockSpec((seq, V_DIM), lambda b, hh: (b, hh)),
        out_shape=jax.ShapeDtypeStruct((bsz * seq, h * V_DIM), BF16),
        scratch_shapes=[pltpu.VMEM((FLASH_TQ, seq), F32)],
        compiler_params=_params("parallel", "parallel"),
        name="flash_prompt",
    )(q, k, v, proj)


PAGED_CHUNK = 1024


def _paged_kernel(pt_ref, qa_ref, qp_ref, cn_ref, kn_ref, ckv_hbm, krt_hbm, o_ref,
                  cbuf, kbuf, cbf, s_ref, sems, *, layer, n_pages):
    b = pl.program_id(0)
    nb = pl.num_programs(0)
    slot = b % 2

    def copies(bb, sl, p):
        page = pt_ref[bb * n_pages + p]
        toks = pl.ds(p * PAGE_SIZE, PAGE_SIZE)
        return (pltpu.make_async_copy(ckv_hbm.at[page, layer], cbuf.at[sl, toks, :], sems.at[0, sl]),
                pltpu.make_async_copy(krt_hbm.at[page, layer], kbuf.at[sl, :, toks], sems.at[1, sl]))

    def start_all(bb, sl):
        for p in range(n_pages):
            c1, c2 = copies(bb, sl, p)
            c1.start()
            c2.start()

    @pl.when(b == 0)
    def _():
        start_all(b, slot)

    @pl.when(b + 1 < nb)
    def _():
        start_all(b + 1, 1 - slot)

    for p in range(n_pages):
        c1, c2 = copies(b, slot, p)
        c1.wait()
        c2.wait()

    past = n_pages * PAGE_SIZE
    n_chunks = past // PAGED_CHUNK
    qa = qa_ref[0].astype(BF16)
    qp = qp_ref[0][:, :QK_ROPE].astype(BF16)

    for c in range(n_chunks):
        toks = slice(c * PAGED_CHUNK, (c + 1) * PAGED_CHUNK)
        cc = cbuf[slot, toks, :].astype(BF16)
        kk = kbuf[slot, :, toks].astype(BF16)
        cbf[toks, :] = cc
        s_ref[:, toks] = _dot_nt(qa, cc) + _dot(qp, kk)

    c_new = cn_ref[0].astype(BF16).astype(F32)
    k_new = kn_ref[0].astype(BF16).astype(F32)
    s_new = (jnp.sum(qa.astype(F32) * c_new, axis=-1, keepdims=True)
             + jnp.sum(qp.astype(F32) * k_new, axis=-1, keepdims=True))
    m = jnp.maximum(jnp.max(s_ref[...], axis=-1, keepdims=True), s_new)
    p_new = jnp.exp2(s_new - m)
    l = p_new
    acc = p_new.astype(BF16).astype(F32) * c_new
    for c in range(n_chunks):
        toks = slice(c * PAGED_CHUNK, (c + 1) * PAGED_CHUNK)
        p = jnp.exp2(s_ref[:, toks] - m)
        l = l + jnp.sum(p, axis=-1, keepdims=True)
        acc = acc + _dot(p.astype(BF16), cbf[toks, :])
    o_ref[0] = acc / l


def paged_attention(page_table, qa, qp, c_new, k_new, cache_kv, cache_krt, *, layer):
    bsz, n_pages = page_table.shape
    past = n_pages * PAGE_SIZE
    h = MLA_HEADS
    grid_spec = pltpu.PrefetchScalarGridSpec(
        num_scalar_prefetch=1,
        grid=(bsz,),
        in_specs=[
            pl.BlockSpec((1, h, KV_LORA), lambda b, pt: (b, 0, 0)),
            pl.BlockSpec((1, h, LANE), lambda b, pt: (b, 0, 0)),
            pl.BlockSpec((1, 1, KV_LORA), lambda b, pt: (b, 0, 0)),
            pl.BlockSpec((1, 1, QK_ROPE), lambda b, pt: (b, 0, 0)),
            pl.BlockSpec(memory_space=pl.ANY),
            pl.BlockSpec(memory_space=pl.ANY),
        ],
        out_specs=pl.BlockSpec((1, h, KV_LORA), lambda b, pt: (b, 0, 0)),
        scratch_shapes=[
            pltpu.VMEM((2, past, KV_LORA), F32),
            pltpu.VMEM((2, QK_ROPE, past), F32),
            pltpu.VMEM((past, KV_LORA), BF16),
            pltpu.VMEM((h, past), F32),
            pltpu.SemaphoreType.DMA((2, 2)),
        ],
    )
    return pl.pallas_call(
        functools.partial(_paged_kernel, layer=layer, n_pages=n_pages),
        grid_spec=grid_spec,
        out_shape=jax.ShapeDtypeStruct((bsz, h, KV_LORA), F32),
        compiler_params=_params("arbitrary"),
        name="paged_attention",
    )(page_table.reshape(-1), qa, qp, c_new, k_new, cache_kv, cache_krt)


def _uv_kernel(ol_ref, wuv_ref, g_ref, o_ref):
    for h in range(MLA_HEADS):
        o = _dot(ol_ref[h].astype(BF16), wuv_ref[h])
        sl = slice(h * V_DIM, (h + 1) * V_DIM)
        o_ref[:, sl] = (o * _silu(g_ref[:, sl])).astype(o_ref.dtype)


def uv_project(o_lat, wuv3, proj):
    h, t, _ = o_lat.shape
    return pl.pallas_call(
        _uv_kernel,
        grid=(1,),
        in_specs=[
            _full((h, t, KV_LORA)), _full((h, KV_LORA, V_DIM)),
            pl.BlockSpec((t, MLA_WIDTH), lambda i: (0, COL_GM // MLA_WIDTH)),
        ],
        out_specs=_full((t, MLA_WIDTH)),
        out_shape=jax.ShapeDtypeStruct((t, MLA_WIDTH), BF16),
        compiler_params=_params("arbitrary"),
        name="uv_project",
    )(o_lat, wuv3, proj)


def _proj_norm_res_kernel(*refs, n_in):
    ins, ws = refs[:n_in], refs[n_in:2 * n_in]
    x_ref, g_ref, o_ref = refs[2 * n_in:]
    y = _dot(ins[0][...], ws[0][...])
    for a, w in zip(ins[1:], ws[1:]):
        y = y + _dot(a[...], w[...])
    o_ref[...] = x_ref[...] + _rms(y, g_ref[...])


def proj_norm_res(ins, ws, x, g, *, tm, name):
    t, d = x.shape
    n_in = len(ins)
    return pl.pallas_call(
        functools.partial(_proj_norm_res_kernel, n_in=n_in),
        grid=(t // tm,),
        in_specs=([pl.BlockSpec((tm, a.shape[1]), lambda i: (i, 0)) for a in ins]
                  + [_full(w.shape) for w in ws]
                  + [pl.BlockSpec((tm, d), lambda i: (i, 0)), _full((1, d))]),
        out_specs=pl.BlockSpec((tm, d), lambda i: (i, 0)),
        out_shape=jax.ShapeDtypeStruct((t, d), F32),
        compiler_params=_params("parallel"),
        name=name,
    )(*ins, *ws, x, g.reshape(1, d))


def _cross_prompt_kernel(x_ref, gpre_ref, wq_ref, mk_ref, mv_ref, wo_ref, gpost_ref, o_ref):
    x = x_ref[...]
    hn = _rms(x, gpre_ref[...]).astype(BF16)
    q = (_dot(hn, wq_ref[...]) * MEM_SCALE).astype(BF16)
    outs = []
    for h in range(MEM_HEADS):
        sl = slice(h * MEM_HEAD_DIM, (h + 1) * MEM_HEAD_DIM)
        s = _dot_nt(q[:, sl], mk_ref[:, sl].astype(BF16))
        m = jnp.max(s, axis=-1, keepdims=True)
        p = jnp.exp(s - m)
        l = jnp.sum(p, axis=-1, keepdims=True)
        outs.append((_dot(p.astype(BF16), mv_ref[:, sl].astype(BF16)) / l).astype(BF16))
    o = jnp.concatenate(outs, axis=-1)
    y = _dot(o, wo_ref[...])
    o_ref[...] = x + _rms(y, gpost_ref[...])


def cross_prompt(x, g_pre, wq, mem_kv, wo, g_post, *, seq, tm):
    t, d = x.shape
    per_seq = seq // tm
    return pl.pallas_call(
        _cross_prompt_kernel,
        grid=(t // tm,),
        in_specs=[
            pl.BlockSpec((tm, d), lambda i: (i, 0)),
            _full((1, d)), _full((d, MEM_WIDTH)),
            pl.BlockSpec((N_MEM, MEM_WIDTH), lambda i: (i // per_seq, 0)),
            pl.BlockSpec((N_MEM, MEM_WIDTH), lambda i: (i // per_seq, 1)),
            _full((MEM_WIDTH, d)), _full((1, d)),
        ],
        out_specs=pl.BlockSpec((tm, d), lambda i: (i, 0)),
        out_shape=jax.ShapeDtypeStruct((t, d), F32),
        compiler_params=_params("parallel"),
        name="cross_prompt",
    )(x, g_pre.reshape(1, d), wq, mem_kv, mem_kv, wo, g_post.reshape(1, d))


def _cross_sample_kernel(q_ref, mk_ref, mv_ref, o_ref):
    bt = q_ref.shape[0]
    q = (q_ref[...] * MEM_SCALE).astype(BF16)
    for h in range(MEM_HEADS):
        sl = slice(h * MEM_HEAD_DIM, (h + 1) * MEM_HEAD_DIM)
        rows = pl.ds(h, N_MEM, stride=MEM_HEADS)
        qh = jnp.broadcast_to(q[:, :, sl], (bt, 8, MEM_HEAD_DIM))
        kh = mk_ref[:, 0, rows, :].astype(BF16)
        vh = mv_ref[:, 0, rows, :].astype(BF16)
        s = jnp.einsum("bqe,bme->bqm", qh, kh, preferred_element_type=F32)
        m = jnp.max(s, axis=-1, keepdims=True)
        p = jnp.exp(s - m)
        l = jnp.sum(p, axis=-1, keepdims=True)
        o = jnp.einsum("bqm,bme->bqe", p.astype(BF16), vh, preferred_element_type=F32) / l
        o_ref[:, :, sl] = o[:, 0:1, :].astype(o_ref.dtype)


def cross_sample_core(q3, mem_k, mem_v, *, layer, bt):
    bsz = q3.shape[0]
    mem = lambda: pl.BlockSpec((bt, 1, N_MEM * MEM_HEADS, MEM_HEAD_DIM), lambda i: (i, layer, 0, 0))
    return pl.pallas_call(
        _cross_sample_kernel,
        grid=(bsz // bt,),
        in_specs=[pl.BlockSpec((bt, 1, MEM_WIDTH), lambda i: (i, 0, 0)), mem(), mem()],
        out_specs=pl.BlockSpec((bt, 1, MEM_WIDTH), lambda i: (i, 0, 0)),
        out_shape=jax.ShapeDtypeStruct((bsz, 1, MEM_WIDTH), BF16),
        compiler_params=_params("parallel"),
        name="cross_sample",
    )(q3, mem_k, mem_v)


def _rope_tables(pos):
    half = QK_ROPE // 2
    inv_freq = ROPE_BASE ** (-jnp.arange(half, dtype=F32) / half)
    ang = pos.astype(F32)[:, None] * inv_freq[None, :]
    cos, sin = jnp.cos(ang), jnp.sin(ang)
    z = jnp.zeros_like(cos)
    return (jnp.concatenate([cos, cos, z, z], axis=-1),
            jnp.concatenate([-sin, z, z, z], axis=-1),
            jnp.concatenate([z, sin, z, z], axis=-1))


def _prep_layer(l, w_in, w_uq, w_uk, w_uv, w_out, w_xq, w_mk, w_mv, w_xo):
    d = w_in.shape[1]
    wi = w_in[l]
    c0 = 3 * CONV_CH
    q_l = wi[:, c0:c0 + Q_LORA]
    kv_l = wi[:, c0 + Q_LORA:c0 + Q_LORA + KV_LORA]
    kpe = wi[:, c0 + Q_LORA + KV_LORA:c0 + Q_LORA + KV_LORA + QK_ROPE]
    g_m = wi[:, c0 + Q_LORA + KV_LORA + QK_ROPE:]
    pad = jnp.zeros((d, PROJ_COLS - COL_KPE - QK_ROPE), wi.dtype)
    w_in_r = jnp.concatenate([wi[:, :c0], g_m, q_l, kv_l, kpe, pad], axis=1).astype(BF16)
    uq = w_uq[l]
    wqn = uq[:, :, :QK_NOPE].reshape(Q_LORA, -1).astype(BF16)
    wqp = jnp.pad(uq[:, :, QK_NOPE:], ((0, 0), (0, 0), (0, LANE - QK_ROPE))).reshape(Q_LORA, -1).astype(BF16)
    return dict(
        w_in=w_in_r, wqn=wqn, wqp=wqp,
        wuk=w_uk[l].reshape(KV_LORA, -1).astype(BF16),
        wuv=w_uv[l].reshape(KV_LORA, -1).astype(BF16),
        wukt=w_uk[l].transpose(1, 2, 0).astype(BF16),
        wuv3=w_uv[l].transpose(1, 0, 2).astype(BF16),
        w_out_c=w_out[l][:CONV_CH].astype(BF16),
        w_out_m=w_out[l][CONV_CH:].astype(BF16),
        w_xq=w_xq[l].reshape(d, -1).astype(BF16),
        w_mkv=jnp.concatenate([w_mk[l].reshape(d, -1), w_mv[l].reshape(d, -1)], axis=1).astype(BF16),
        w_xo=w_xo[l].reshape(-1, d).astype(BF16),
    )


def kernel(x_prompt, x_sample, mem_prompt, cache_kv_latent, cache_k_rope, state_conv, cache_mem_k, cache_mem_v, page_table, norm_mix_pre, w_in, conv_w, conv_b, conv_ln_g, conv_ln_b, q_norm_g, w_uq, kv_norm_g, w_uk, w_uv, w_out, norm_mix_post, norm_x_pre, norm_mem, w_xq, w_mk, w_mv, w_xo, norm_x_post):
    b_p, s_p, d = x_prompt.shape
    b_s = x_sample.shape[0]
    depth = w_in.shape[0]
    n_pages = page_table.shape[1]
    t_p = b_p * s_p
    tm_p = min(512, s_p)

    tabs_p = _rope_tables(jnp.arange(s_p, dtype=jnp.int32))
    tabs_s = _rope_tables(jnp.full((b_s,), n_pages * PAGE_SIZE, jnp.int32))
    mem2 = mem_prompt.reshape(b_p * N_MEM, d)
    mem_k_s = cache_mem_k.reshape(b_s, depth, N_MEM * MEM_HEADS, MEM_HEAD_DIM)
    mem_v_s = cache_mem_v.reshape(b_s, depth, N_MEM * MEM_HEADS, MEM_HEAD_DIM)
    cache_krt = cache_k_rope.transpose(0, 1, 3, 2)
    state_t = state_conv.transpose(0, 2, 1, 3)
    conv_w_t = conv_w.transpose(1, 0, 2)
    new_state_t, conv_part = conv_state_shift(state_t, conv_w_t, bt=8)

    xp = x_prompt.reshape(t_p, d)
    xs = x_sample.reshape(b_s, d)
    lat_p, kpe_p, conv_p, mk_p, mv_p, lat_s, kpe_s = [], [], [], [], [], [], []
    u_s = []
    for l in range(depth):
        w = _prep_layer(l, w_in, w_uq, w_uk, w_uv, w_out, w_xq, w_mk, w_mv, w_xo)

        proj = rms_matmul(xp, norm_mix_pre[l], w["w_in"], tm=min(1024, s_p), tn=1024, name="in_proj_p")
        conv_out, conv_state = conv_prompt(proj, conv_w[l], conv_b[l], conv_ln_g[l], conv_ln_b[l],
                                           bsz=b_p, seq=s_p, ts=tm_p)
        q, k, v, c_kv, k_pe = mla_prompt(proj, tabs_p, q_norm_g[l], kv_norm_g[l],
                                         w["wqn"], w["wqp"], w["wuk"], w["wuv"], tm=tm_p)
        attn = flash_prompt(q, k, v, proj, bsz=b_p, seq=s_p)
        xp = proj_norm_res([conv_out, attn], [w["w_out_c"], w["w_out_m"]], xp, norm_mix_post[l],
                           tm=tm_p, name="out_proj_p")
        mem_kv = rms_matmul(mem2, norm_mem[l], w["w_mkv"], tm=N_MEM, tn=2 * MEM_WIDTH, name="mem_kv")
        xp = cross_prompt(xp, norm_x_pre[l], w["w_xq"], mem_kv, w["w_xo"], norm_x_post[l],
                          seq=s_p, tm=tm_p)
        lat_p.append(c_kv.reshape(b_p, s_p, KV_LORA))
        kpe_p.append(k_pe.reshape(b_p, s_p, QK_ROPE))
        conv_p.append(conv_state)
        mk_p.append(mem_kv[:, :MEM_WIDTH].reshape(b_p, N_MEM, MEM_HEADS, MEM_HEAD_DIM))
        mv_p.append(mem_kv[:, MEM_WIDTH:].reshape(b_p, N_MEM, MEM_HEADS, MEM_HEAD_DIM))

        proj_s = rms_matmul(xs, norm_mix_pre[l], w["w_in"], tm=b_s, tn=1024, name="in_proj_s")
        conv_out_s, u_l = conv_sample(proj_s, conv_part, l, conv_w[l], conv_b[l], conv_ln_g[l], conv_ln_b[l])
        u_s.append(u_l)
        qa, qp, c_new, k_new = mla_sample(proj_s, tabs_s, q_norm_g[l], kv_norm_g[l],
                                          w["wqn"], w["wqp"], w["wukt"])
        o_lat = paged_attention(page_table, qa.transpose(1, 0, 2), qp.transpose(1, 0, 2),
                                c_new.reshape(b_s, 1, KV_LORA), k_new.reshape(b_s, 1, QK_ROPE),
                                cache_kv_latent, cache_krt, layer=l)
        attn_s = uv_project(o_lat.transpose(1, 0, 2), w["wuv3"], proj_s)
        xs = proj_norm_res([conv_out_s, attn_s], [w["w_out_c"], w["w_out_m"]],
                           xs, norm_mix_post[l], tm=b_s, name="out_proj_s")
        q_x = rms_matmul(xs, norm_x_pre[l], w["w_xq"], tm=b_s, tn=MEM_WIDTH, name="cross_q_s")
        o_x = cross_sample_core(q_x.reshape(b_s, 1, MEM_WIDTH), mem_k_s, mem_v_s, layer=l, bt=8)
        xs = proj_norm_res([o_x.reshape(b_s, MEM_WIDTH)], [w["w_xo"]], xs, norm_x_post[l],
                           tm=b_s, name="cross_out_s")
        lat_s.append(c_new.reshape(b_s, 1, KV_LORA))
        kpe_s.append(k_new.reshape(b_s, 1, QK_ROPE))

    return (xp.reshape(b_p, s_p, d), xs.reshape(b_s, 1, d),
            jnp.stack(lat_p, axis=1), jnp.stack(kpe_p, axis=1), jnp.stack(conv_p, axis=1),
            jnp.stack(mk_p, axis=1), jnp.stack(mv_p, axis=1),
            jnp.stack(lat_s, axis=1), jnp.stack(kpe_s, axis=1),
            conv_state_finish(new_state_t, jnp.stack(u_s, axis=1)[:, None]).transpose(0, 2, 1, 3))
```

```python
import functools

import jax
import jax.numpy as jnp
from jax import lax
from jax.experimental import pallas as pl
from jax.experimental.pallas import tpu as pltpu

F32 = jnp.float32
BF16 = jnp.bfloat16

D_MODEL = 2048
CONV_CH = 1024
CONV_WIDTH = 31
CONV_STATE = CONV_WIDTH - 1
MLA_HEADS = 8
QK_NOPE = 128
QK_ROPE = 64
V_DIM = 128
MLA_WIDTH = MLA_HEADS * V_DIM
Q_LORA = 512
KV_LORA = 256
MLA_SCALE = (QK_NOPE + QK_ROPE) ** -0.5
ROPE_BASE = 10000.0
PAGE_SIZE = 128
N_MEM = 256
MEM_HEADS = 4
MEM_HEAD_DIM = 128
MEM_WIDTH = MEM_HEADS * MEM_HEAD_DIM
MEM_SCALE = MEM_HEAD_DIM ** -0.5
EPS = 1e-6
LOG2E = 1.4426950408889634

LANE = 128
QK_PAD = 256
NEG_BIG = -1e30

PROJ_COLS = 5120
COL_A, COL_B, COL_GC, COL_GM, COL_QL, COL_KV, COL_KPE = 0, 1024, 2048, 3072, 4096, 4608, 4864

VMEM_LIMIT = 52 * 1024 * 1024


def _params(*sem):
    return pltpu.CompilerParams(dimension_semantics=sem, vmem_limit_bytes=VMEM_LIMIT)


def _rms(x, g):
    return x * lax.rsqrt(jnp.mean(x * x, axis=-1, keepdims=True) + EPS) * g


def _silu(x):
    return x * jax.nn.sigmoid(x)


def _dot(a, b):
    return jnp.dot(a, b, preferred_element_type=F32)


def _dot_nt(a, b):
    return lax.dot_general(a, b, (((1,), (1,)), ((), ())), preferred_element_type=F32)


def _rope128(x, cos_t, sin_n, sin_p):
    return x * cos_t + pltpu.roll(x, 96, 1) * sin_n + pltpu.roll(x, 32, 1) * sin_p


def _rms_matmul_kernel(x_ref, g_ref, w_ref, o_ref, hn_ref, *, w_transposed):
    @pl.when(pl.program_id(1) == 0)
    def _():
        hn_ref[...] = _rms(x_ref[...], g_ref[...]).astype(BF16)

    dot = _dot_nt if w_transposed else _dot
    o_ref[...] = dot(hn_ref[...], w_ref[...]).astype(o_ref.dtype)


def rms_matmul(x, g, w, *, tm, tn, name, w_transposed=False, layer=None):
    t, k = x.shape
    n = w.shape[-2] if w_transposed else w.shape[-1]
    lead = () if layer is None else (None,)
    pick = () if layer is None else (layer,)
    w_spec = (pl.BlockSpec(lead + (tn, k), lambda i, j: pick + (j, 0)) if w_transposed
              else pl.BlockSpec(lead + (k, tn), lambda i, j: pick + (0, j)))
    return pl.pallas_call(
        functools.partial(_rms_matmul_kernel, w_transposed=w_transposed),
        grid=(t // tm, n // tn),
        in_specs=[
            pl.BlockSpec((tm, k), lambda i, j: (i, 0)),
            pl.BlockSpec((1, k), lambda i, j: (0, 0)),
            w_spec,
        ],
        out_specs=pl.BlockSpec((tm, tn), lambda i, j: (i, j)),
        out_shape=jax.ShapeDtypeStruct((t, n), F32),
        scratch_shapes=[pltpu.VMEM((tm, k), BF16)],
        compiler_params=_params("parallel", "arbitrary"),
        name=name,
    )(x, g.reshape(1, k), w)


_W_IN_GROUPS = (
    (COL_A, 0, 3 * CONV_CH),
    (COL_GM, 3 * CONV_CH + Q_LORA + KV_LORA + QK_ROPE, MLA_WIDTH),
    (COL_QL, 3 * CONV_CH, Q_LORA),
    (COL_KV, 3 * CONV_CH + Q_LORA, KV_LORA),
    (COL_KPE, 3 * CONV_CH + Q_LORA + KV_LORA, QK_ROPE),
)
W_PREP_TK = 512


def _w_in_prep_kernel(w_ref, o_ref):
    for dst, src, rows in _W_IN_GROUPS:
        o_ref[0, dst:dst + rows, :] = w_ref[0, src:src + rows, :].astype(BF16)
    pad0 = COL_KPE + QK_ROPE
    o_ref[0, pad0:, :] = jnp.zeros((PROJ_COLS - pad0, o_ref.shape[2]), BF16)


def w_in_prep(w_in_t):
    depth, n_in, k = w_in_t.shape
    return pl.pallas_call(
        _w_in_prep_kernel,
        grid=(depth, k // W_PREP_TK),
        in_specs=[pl.BlockSpec((1, n_in, W_PREP_TK), lambda l, j: (l, 0, j))],
        out_specs=pl.BlockSpec((1, PROJ_COLS, W_PREP_TK), lambda l, j: (l, 0, j)),
        out_shape=jax.ShapeDtypeStruct((depth, PROJ_COLS, k), BF16),
        compiler_params=_params("parallel", "parallel"),
        name="w_in_prep",
    )(w_in_t)


CONV_SEGS = 8
CONV_SEG_ROWS = 128
CONV_HALO = 32
CONV_BLOCK = 8
CONV_TAPS_A = 16
CONV_UNROLL = 16
CONV_PAD = 32
CONV_TILE = CONV_SEGS * CONV_SEG_ROWS


def _conv_prompt_kernel(proj_hbm, w_ref, cb_ref, lg_ref, lb_ref, out_hbm, state_ref,
                        xa, xb, xg, y_ref, o_ref, wb_ref, lgb_ref, in_sems, out_sem, *, tiles_per_seq):
    n = pl.program_id(0)
    n_tiles = pl.num_programs(0)
    slot = n % 2
    lrows, halo = CONV_SEG_ROWS, CONV_HALO

    def in_copies(tile, sl, seg, first):
        row = tile * CONV_TILE + seg * lrows
        if first:
            src_rows, dst_rows = pl.ds(row, lrows), pl.ds(halo, lrows)
        else:
            src_rows, dst_rows = pl.ds(row - halo, lrows + halo), pl.ds(0, lrows + halo)
        return (
            pltpu.make_async_copy(proj_hbm.at[src_rows, pl.ds(COL_A, CONV_CH)],
                                  xa.at[sl, dst_rows, seg, :], in_sems.at[sl, 0]),
            pltpu.make_async_copy(proj_hbm.at[src_rows, pl.ds(COL_B, CONV_CH)],
                                  xb.at[sl, dst_rows, seg, :], in_sems.at[sl, 1]),
            pltpu.make_async_copy(proj_hbm.at[pl.ds(row, lrows), pl.ds(COL_GC, CONV_CH)],
                                  xg.at[sl, :, seg, :], in_sems.at[sl, 2]),
        )

    def for_each_in_copy(tile, sl, fn):
        seq_start = tile % tiles_per_seq == 0

        @pl.when(seq_start)
        def _():
            for c in in_copies(tile, sl, 0, True):
                fn(c)

        @pl.when(jnp.logical_not(seq_start))
        def _():
            for c in in_copies(tile, sl, 0, False):
                fn(c)

        for seg in range(1, CONV_SEGS):
            for c in in_copies(tile, sl, seg, False):
                fn(c)

    def out_copies(tile):
        return [pltpu.make_async_copy(o_ref.at[:, seg, :],
                                      out_hbm.at[pl.ds(tile * CONV_TILE + seg * lrows, lrows), :], out_sem.at[0])
                for seg in range(CONV_SEGS)]

    @pl.when(n == 0)
    def _():
        for k in range(CONV_WIDTH):
            wb_ref[k] = jnp.broadcast_to(w_ref[k:k + 1, :], (CONV_SEGS, CONV_CH))
        lgb_ref[0] = jnp.broadcast_to(lg_ref[...], (CONV_SEGS, CONV_CH))
        lgb_ref[1] = jnp.broadcast_to(lb_ref[...], (CONV_SEGS, CONV_CH))
        for_each_in_copy(n, slot, lambda c: c.start())

    @pl.when(n + 1 < n_tiles)
    def _():
        for_each_in_copy(n + 1, 1 - slot, lambda c: c.start())

    for_each_in_copy(n, slot, lambda c: c.wait())

    def glu(i, carry):
        xa[slot, i] = xa[slot, i] * jax.nn.sigmoid(xb[slot, i])
        return carry

    lax.fori_loop(0, lrows + halo, glu, 0, unroll=4)

    @pl.when(n % tiles_per_seq == 0)
    def _():
        xa[slot, 0:halo, 0:1, :] = jnp.zeros((halo, 1, CONV_CH), F32)

    @pl.when(n % tiles_per_seq == tiles_per_seq - 1)
    def _():
        state_ref[0] = xa[slot, halo + lrows - CONV_STATE:halo + lrows, CONV_SEGS - 1, :]

    y_ref[0:CONV_PAD - CONV_TAPS_A - 1] = jnp.zeros((CONV_PAD - CONV_TAPS_A - 1, CONV_SEGS, CONV_CH), F32)
    shift = halo - CONV_STATE
    zero = jnp.zeros((CONV_SEGS, LANE), F32)
    for k0, n_taps in ((0, CONV_TAPS_A), (CONV_TAPS_A, CONV_WIDTH - CONV_TAPS_A)):
        first = k0 == 0
        for c0 in range(0, CONV_CH, LANE):
            cols = slice(c0, c0 + LANE)
            w = [wb_ref[k0 + t, :, cols] for t in range(n_taps)]
            fresh = (cb_ref[:, cols] + zero) if first else zero
            row_off = CONV_PAD - shift - k0 - (n_taps - 1)

            def body(it, carry, w=w, fresh=fresh, cols=cols, row_off=row_off, first=first):
                acc = list(carry)
                j0 = it * CONV_UNROLL
                for jj in range(CONV_UNROLL):
                    u = xa[slot, j0 + jj, :, cols]
                    acc = [a + wt * u for wt, a in zip(w, [fresh] + acc)]
                    done = acc.pop()
                    if first:
                        y_ref[j0 + (jj + row_off), :, cols] = done
                    else:
                        y_ref[j0 + (jj + row_off), :, cols] = y_ref[j0 + (jj + row_off), :, cols] + done
                return tuple(acc)

            lax.fori_loop(0, (lrows + halo) // CONV_UNROLL, body, tuple(zero for _ in range(n_taps - 1)))

    @pl.when(n > 0)
    def _():
        for c in out_copies(n - 1):
            c.wait()

    def norm_gate(r, carry):
        for ii in range(CONV_BLOCK):
            i = r * CONV_BLOCK + ii
            acc = y_ref[i + CONV_PAD]
            mu = jnp.mean(acc, axis=-1, keepdims=True)
            d = acc - mu
            var = jnp.mean(d * d, axis=-1, keepdims=True)
            z = d * lax.rsqrt(var + EPS) * lgb_ref[0] + lgb_ref[1]
            o_ref[i] = _silu(z) * _silu(xg[slot, i])
        return carry

    lax.fori_loop(0, lrows // CONV_BLOCK, norm_gate, 0)

    for c in out_copies(n):
        c.start()

    @pl.when(n == n_tiles - 1)
    def _():
        for c in out_copies(n):
            c.wait()


def conv_prompt(proj, conv_w, conv_b, ln_g, ln_b, *, bsz, seq):
    tiles_per_seq = seq // CONV_TILE
    vec = lambda: pl.BlockSpec((1, CONV_CH), lambda n: (0, 0))
    seg_buf = lambda rows: pltpu.VMEM((2, rows, CONV_SEGS, CONV_CH), F32)
    return pl.pallas_call(
        functools.partial(_conv_prompt_kernel, tiles_per_seq=tiles_per_seq),
        grid=(bsz * tiles_per_seq,),
        in_specs=[
            pl.BlockSpec(memory_space=pl.ANY),
            pl.BlockSpec((CONV_WIDTH, CONV_CH), lambda n: (0, 0)),
            vec(), vec(), vec(),
        ],
        out_specs=[
            pl.BlockSpec(memory_space=pl.ANY),
            pl.BlockSpec((1, CONV_STATE, CONV_CH), lambda n: (n // tiles_per_seq, 0, 0)),
        ],
        out_shape=[
            jax.ShapeDtypeStruct((bsz * seq, CONV_CH), F32),
            jax.ShapeDtypeStruct((bsz, CONV_STATE, CONV_CH), F32),
        ],
        scratch_shapes=[
            seg_buf(CONV_SEG_ROWS + CONV_HALO), seg_buf(CONV_SEG_ROWS + CONV_HALO), seg_buf(CONV_SEG_ROWS),
            pltpu.VMEM((CONV_SEG_ROWS + 2 * CONV_PAD, CONV_SEGS, CONV_CH), F32),
            pltpu.VMEM((CONV_SEG_ROWS, CONV_SEGS, CONV_CH), F32),
            pltpu.VMEM((CONV_WIDTH, CONV_SEGS, CONV_CH), F32),
            pltpu.VMEM((2, CONV_SEGS, CONV_CH), F32),
            pltpu.SemaphoreType.DMA((2, 3)), pltpu.SemaphoreType.DMA((1,)),
        ],
        compiler_params=_params("arbitrary"),
        name="conv_prompt",
    )(proj, conv_w, conv_b.reshape(1, -1), ln_g.reshape(1, -1), ln_b.reshape(1, -1))


def _conv_state_kernel(st_ref, w_ref, newst_ref, part_ref):
    st = st_ref[...]
    newst_ref[:, 0:CONV_STATE - 1] = st[:, 1:CONV_STATE]
    newst_ref[:, CONV_STATE - 1:CONV_STATE] = jnp.zeros_like(st[:, 0:1])
    part_ref[...] = jnp.sum(st * w_ref[0:CONV_STATE][None], axis=1)


def conv_state_shift(state_t, conv_w_t, *, bt):
    bsz, _, depth, _ = state_t.shape
    blk = pl.BlockSpec((bt, CONV_STATE, depth, CONV_CH), lambda i: (i, 0, 0, 0))
    return pl.pallas_call(
        _conv_state_kernel,
        grid=(bsz // bt,),
        in_specs=[blk, pl.BlockSpec((CONV_WIDTH, depth, CONV_CH), lambda i: (0, 0, 0))],
        out_specs=[blk, pl.BlockSpec((bt, depth, CONV_CH), lambda i: (i, 0, 0))],
        out_shape=[jax.ShapeDtypeStruct(state_t.shape, F32),
                   jax.ShapeDtypeStruct((bsz, depth, CONV_CH), F32)],
        compiler_params=_params("parallel"),
        name="conv_state_shift",
    )(state_t, conv_w_t)


def _conv_state_finish_kernel(u_ref, st_hbm, o_ref):
    del st_hbm
    o_ref[...] = u_ref[...]


def conv_state_finish(new_state, u_all):
    bsz, _, depth, _ = new_state.shape
    return pl.pallas_call(
        _conv_state_finish_kernel,
        grid=(1,),
        in_specs=[pl.BlockSpec((bsz, 1, depth, CONV_CH), lambda i: (0, 0, 0, 0)),
                  pl.BlockSpec(memory_space=pl.ANY)],
        out_specs=pl.BlockSpec((bsz, 1, depth, CONV_CH), lambda i: (0, CONV_STATE - 1, 0, 0)),
        out_shape=jax.ShapeDtypeStruct(new_state.shape, F32),
        input_output_aliases={1: 0},
        compiler_params=_params("arbitrary"),
        name="conv_state_finish",
    )(u_all, new_state)


def _conv_sample_kernel(a_ref, b_ref, gc_ref, part_ref, w_ref, cb_ref, lg_ref, lb_ref, out_ref, u_ref, *, layer):
    u = a_ref[...] * jax.nn.sigmoid(b_ref[...])
    u_ref[...] = u
    y = part_ref[:, layer, :] + u * w_ref[CONV_STATE:CONV_WIDTH, :] + cb_ref[...]
    mu = jnp.mean(y, axis=-1, keepdims=True)
    d = y - mu
    var = jnp.mean(d * d, axis=-1, keepdims=True)
    z = d * lax.rsqrt(var + EPS) * lg_ref[...] + lb_ref[...]
    out_ref[...] = (_silu(z) * _silu(gc_ref[...])).astype(out_ref.dtype)


def conv_sample(proj, part, layer, conv_w, conv_b, ln_g, ln_b):
    bsz = proj.shape[0]
    vec = lambda: pl.BlockSpec((1, CONV_CH), lambda i: (0, 0))
    row = lambda: pl.BlockSpec((bsz, CONV_CH), lambda i: (0, 0))
    return pl.pallas_call(
        functools.partial(_conv_sample_kernel, layer=layer),
        grid=(1,),
        in_specs=[
            pl.BlockSpec((bsz, CONV_CH), lambda i: (0, COL_A // CONV_CH)),
            pl.BlockSpec((bsz, CONV_CH), lambda i: (0, COL_B // CONV_CH)),
            pl.BlockSpec((bsz, CONV_CH), lambda i: (0, COL_GC // CONV_CH)),
            pl.BlockSpec(part.shape, lambda i: (0, 0, 0)),
            pl.BlockSpec((CONV_WIDTH, CONV_CH), lambda i: (0, 0)),
            vec(), vec(), vec(),
        ],
        out_specs=[row(), row()],
        out_shape=[
            jax.ShapeDtypeStruct((bsz, CONV_CH), BF16),
            jax.ShapeDtypeStruct((bsz, CONV_CH), F32),
        ],
        compiler_params=_params("arbitrary"),
        name="conv_sample",
    )(proj, proj, proj, part, conv_w, conv_b.reshape(1, -1), ln_g.reshape(1, -1), ln_b.reshape(1, -1))


def _mla_common(ql_ref, kvl_ref, kpe_ref, cos_ref, sn_ref, sp_ref, qg_ref, kvg_ref, wqn_ref, wqp_ref):
    qn = _rms(ql_ref[...], qg_ref[...]).astype(BF16)
    q_nope = _dot(qn, wqn_ref[...]) * (MLA_SCALE * LOG2E)
    q_rope_raw = _dot(qn, wqp_ref[...]) * (MLA_SCALE * LOG2E)
    cos_t, sin_n, sin_p = cos_ref[...], sn_ref[...], sp_ref[...]
    q_rope = [_rope128(q_rope_raw[:, h * LANE:(h + 1) * LANE], cos_t, sin_n, sin_p)
              for h in range(MLA_HEADS)]
    c_kv = _rms(kvl_ref[...], kvg_ref[...])
    k_pe = _rope128(kpe_ref[...], cos_t, sin_n, sin_p)
    return q_nope, q_rope, c_kv, k_pe


def _mla_prompt_kernel(ql_ref, kvl_ref, kpe_ref, cos_ref, sn_ref, sp_ref, qg_ref, kvg_ref,
                       wqn_ref, wqp_ref, wuk_ref, wuv_ref,
                       q_ref, k_ref, v_ref, ckv_ref, kpeo_ref):
    q_nope, q_rope, c_kv, k_pe = _mla_common(ql_ref, kvl_ref, kpe_ref, cos_ref, sn_ref, sp_ref,
                                             qg_ref, kvg_ref, wqn_ref, wqp_ref)
    ckv_ref[...] = c_kv
    kpeo_ref[...] = k_pe[:, :QK_ROPE]
    c_bf = c_kv.astype(BF16)
    k_nope = _dot(c_bf, wuk_ref[...])
    v_ref[...] = _dot(c_bf, wuv_ref[...]).astype(BF16)
    k_pe_bf = k_pe.astype(BF16)
    for h in range(MLA_HEADS):
        lo = h * QK_PAD
        q_ref[:, lo:lo + LANE] = q_nope[:, h * LANE:(h + 1) * LANE].astype(BF16)
        q_ref[:, lo + LANE:lo + QK_PAD] = q_rope[h].astype(BF16)
        k_ref[:, lo:lo + LANE] = k_nope[:, h * LANE:(h + 1) * LANE].astype(BF16)
        k_ref[:, lo + LANE:lo + QK_PAD] = k_pe_bf


def _proj_specs(tm, n_tab):
    return [
        pl.BlockSpec((tm, Q_LORA), lambda i: (i, COL_QL // Q_LORA)),
        pl.BlockSpec((tm, KV_LORA), lambda i: (i, COL_KV // KV_LORA)),
        pl.BlockSpec((tm, LANE), lambda i: (i, COL_KPE // LANE)),
        pl.BlockSpec((tm, LANE), lambda i: (i % n_tab, 0)),
        pl.BlockSpec((tm, LANE), lambda i: (i % n_tab, 0)),
        pl.BlockSpec((tm, LANE), lambda i: (i % n_tab, 0)),
    ]


def _full(shape):
    return pl.BlockSpec(shape, lambda i: (0,) * len(shape))


def mla_prompt(proj, tabs, q_g, kv_g, wqn, wqp, wuk, wuv, *, tm):
    t = proj.shape[0]
    n_tab = tabs[0].shape[0] // tm
    h = MLA_HEADS
    return pl.pallas_call(
        _mla_prompt_kernel,
        grid=(t // tm,),
        in_specs=_proj_specs(tm, n_tab) + [
            _full((1, Q_LORA)), _full((1, KV_LORA)),
            _full((Q_LORA, h * LANE)), _full((Q_LORA, h * LANE)),
            _full((KV_LORA, h * LANE)), _full((KV_LORA, h * LANE)),
        ],
        out_specs=[
            pl.BlockSpec((tm, h * QK_PAD), lambda i: (i, 0)),
            pl.BlockSpec((tm, h * QK_PAD), lambda i: (i, 0)),
            pl.BlockSpec((tm, h * V_DIM), lambda i: (i, 0)),
            pl.BlockSpec((tm, KV_LORA), lambda i: (i, 0)),
            pl.BlockSpec((tm, QK_ROPE), lambda i: (i, 0)),
        ],
        out_shape=[
            jax.ShapeDtypeStruct((t, h * QK_PAD), BF16),
            jax.ShapeDtypeStruct((t, h * QK_PAD), BF16),
            jax.ShapeDtypeStruct((t, h * V_DIM), BF16),
            jax.ShapeDtypeStruct((t, KV_LORA), F32),
            jax.ShapeDtypeStruct((t, QK_ROPE), F32),
        ],
        compiler_params=_params("parallel"),
        name="mla_prompt",
    )(proj, proj, proj, *tabs, q_g.reshape(1, -1), kv_g.reshape(1, -1), wqn, wqp, wuk, wuv)


def _mla_sample_kernel(ql_ref, kvl_ref, kpe_ref, cos_ref, sn_ref, sp_ref, qg_ref, kvg_ref,
                       wqn_ref, wqp_ref, wukt_ref,
                       qa_ref, qp_ref, ckv_ref, kpeo_ref):
    q_nope, q_rope, c_kv, k_pe = _mla_common(ql_ref, kvl_ref, kpe_ref, cos_ref, sn_ref, sp_ref,
                                             qg_ref, kvg_ref, wqn_ref, wqp_ref)
    ckv_ref[...] = c_kv
    kpeo_ref[...] = k_pe[:, :QK_ROPE]
    for h in range(MLA_HEADS):
        qa_ref[h] = _dot(q_nope[:, h * LANE:(h + 1) * LANE].astype(BF16), wukt_ref[h])
        qp_ref[h] = q_rope[h]


def mla_sample(proj, tabs, q_g, kv_g, wqn, wqp, wukt):
    t = proj.shape[0]
    h = MLA_HEADS
    return pl.pallas_call(
        _mla_sample_kernel,
        grid=(1,),
        in_specs=_proj_specs(t, 1) + [
            _full((1, Q_LORA)), _full((1, KV_LORA)),
            _full((Q_LORA, h * LANE)), _full((Q_LORA, h * LANE)),
            _full((h, QK_NOPE, KV_LORA)),
        ],
        out_specs=[
            _full((h, t, KV_LORA)), _full((h, t, LANE)),
            _full((t, KV_LORA)), _full((t, QK_ROPE)),
        ],
        out_shape=[
            jax.ShapeDtypeStruct((h, t, KV_LORA), F32),
            jax.ShapeDtypeStruct((h, t, LANE), F32),
            jax.ShapeDtypeStruct((t, KV_LORA), F32),
            jax.ShapeDtypeStruct((t, QK_ROPE), F32),
        ],
        compiler_params=_params("arbitrary"),
        name="mla_sample",
    )(proj, proj, proj, *tabs, q_g.reshape(1, -1), kv_g.reshape(1, -1), wqn, wqp, wukt)


FLASH_TQ = 256


def _flash_kernel(q_ref, k_ref, v_ref, g_ref, o_ref, s_ref, *, seq):
    tq = FLASH_TQ
    n_lane = tq // LANE
    causal = (lax.broadcasted_iota(jnp.int32, (tq, tq), 1)
              <= lax.broadcasted_iota(jnp.int32, (tq, tq), 0))
    for i in range(seq // tq):
        q = q_ref[i * tq:(i + 1) * tq, :]
        mx = jnp.full((tq, LANE), NEG_BIG, F32)
        for j in range(i + 1):
            s = _dot_nt(q, k_ref[j * tq:(j + 1) * tq, :])
            if j == i:
                s = jnp.where(causal, s, NEG_BIG)
            s_ref[:, j * tq:(j + 1) * tq] = s
            for c in range(n_lane):
                mx = jnp.maximum(mx, s[:, c * LANE:(c + 1) * LANE])
        m = jnp.max(mx, axis=-1, keepdims=True)
        lsum = jnp.zeros((tq, LANE), F32)
        acc = jnp.zeros((tq, V_DIM), F32)
        for j in range(i + 1):
            p = jnp.exp2(s_ref[:, j * tq:(j + 1) * tq] - m)
            for c in range(n_lane):
                lsum = lsum + p[:, c * LANE:(c + 1) * LANE]
            acc = acc + _dot(p.astype(BF16), v_ref[j * tq:(j + 1) * tq, :])
        l = jnp.sum(lsum, axis=-1, keepdims=True)
        gate = _silu(g_ref[i * tq:(i + 1) * tq, :])
        o_ref[i * tq:(i + 1) * tq, :] = (acc / l * gate).astype(o_ref.dtype)


def flash_prompt(q, k, v, proj, *, bsz, seq):
    h = MLA_HEADS
    return pl.pallas_call(
        functools.partial(_flash_kernel, seq=seq),
        grid=(bsz, h),
        in_specs=[
            pl.BlockSpec((seq, QK_PAD), lambda b, hh: (b, hh)),
            pl.BlockSpec((seq, QK_PAD), lambda b, hh: (b, hh)),
            pl.BlockSpec((seq, V_DIM), lambda b, hh: (b, hh)),
            pl.BlockSpec((seq, V_DIM), lambda b, hh: (b, COL_GM // V_DIM + hh)),
        ],
        out_specs=pl.BlockSpec((seq, V_DIM), lambda b, hh: (b, hh)),
        out_shape=jax.ShapeDtypeStruct((bsz * seq, h * V_DIM), BF16),
        scratch_shapes=[pltpu.VMEM((FLASH_TQ, seq), F32)],
        compiler_params=_params("parallel", "parallel"),
        name="flash_prompt",
    )(q, k, v, proj)


PAGED_CHUNK = 1024
PAGED_GROUPS = 4


def _paged_kernel(pt_ref, qa_ref, qp_ref, cn_ref, kn_ref, ckv_hbm, krt_hbm, o_ref,
                  cbuf, kbuf, cbf, s_ref, sems, *, layer, n_pages):
    b = pl.program_id(0)
    nb = pl.num_programs(0)
    slot = b % 2

    def copies(bb, sl, p):
        page = pt_ref[bb * n_pages + p]
        toks = pl.ds(p * PAGE_SIZE, PAGE_SIZE)
        return (pltpu.make_async_copy(ckv_hbm.at[page, layer], cbuf.at[sl, toks, :], sems.at[0, sl]),
                pltpu.make_async_copy(krt_hbm.at[page, layer], kbuf.at[sl, :, toks], sems.at[1, sl]))

    def start_all(bb, sl):
        for p in range(n_pages):
            c1, c2 = copies(bb, sl, p)
            c1.start()
            c2.start()

    @pl.when(b == 0)
    def _():
        start_all(b, slot)

    @pl.when(b + 1 < nb)
    def _():
        start_all(b + 1, 1 - slot)

    for p in range(n_pages):
        c1, c2 = copies(b, slot, p)
        c1.wait()
        c2.wait()

    past = n_pages * PAGE_SIZE
    n_chunks = past // PAGED_CHUNK
    per_group = n_chunks // PAGED_GROUPS
    qa = qa_ref[0].astype(BF16)
    qp = qp_ref[0][:, :QK_ROPE].astype(BF16)

    def qk_chunk(c):
        toks = slice(c * PAGED_CHUNK, (c + 1) * PAGED_CHUNK)
        cc = cbuf[slot, toks, :].astype(BF16)
        kk = kbuf[slot, :, toks].astype(BF16)
        cbf[toks, :] = cc
        s_ref[:, toks] = _dot_nt(qa, cc) + _dot(qp, kk)

    def pv_chunk(c, m_g, l_g, acc_g):
        toks = slice(c * PAGED_CHUNK, (c + 1) * PAGED_CHUNK)
        p = jnp.exp2(s_ref[:, toks] - m_g)
        return l_g + jnp.sum(p, axis=-1, keepdims=True), acc_g + _dot(p.astype(BF16), cbf[toks, :])

    def group_max(g):
        g_toks = slice(g * per_group * PAGED_CHUNK, (g + 1) * per_group * PAGED_CHUNK)
        return jnp.max(s_ref[:, g_toks], axis=-1, keepdims=True)

    stats = []
    for c in range(per_group):
        qk_chunk(c)
    for g in range(PAGED_GROUPS):
        m_g = group_max(g)
        l_g = jnp.zeros((MLA_HEADS, 1), F32)
        acc_g = jnp.zeros((MLA_HEADS, KV_LORA), F32)
        for c in range(per_group):
            if g + 1 < PAGED_GROUPS:
                qk_chunk((g + 1) * per_group + c)
            l_g, acc_g = pv_chunk(g * per_group + c, m_g, l_g, acc_g)
        stats.append((m_g, l_g, acc_g))

    c_new = cn_ref[0].astype(BF16).astype(F32)
    k_new = kn_ref[0].astype(BF16).astype(F32)
    s_new = (jnp.sum(qa.astype(F32) * c_new, axis=-1, keepdims=True)
             + jnp.sum(qp.astype(F32) * k_new, axis=-1, keepdims=True))
    m = s_new
    for m_g, _, _ in stats:
        m = jnp.maximum(m, m_g)
    p_new = jnp.exp2(s_new - m)
    l = p_new
    acc = p_new.astype(BF16).astype(F32) * c_new
    for m_g, l_g, acc_g in stats:
        w_g = jnp.exp2(m_g - m)
        l = l + w_g * l_g
        acc = acc + w_g * acc_g
    o_ref[0] = acc / l


def paged_attention(page_table, qa, qp, c_new, k_new, cache_kv, cache_krt, *, layer):
    bsz, n_pages = page_table.shape
    past = n_pages * PAGE_SIZE
    h = MLA_HEADS
    grid_spec = pltpu.PrefetchScalarGridSpec(
        num_scalar_prefetch=1,
        grid=(bsz,),
        in_specs=[
            pl.BlockSpec((1, h, KV_LORA), lambda b, pt: (b, 0, 0)),
            pl.BlockSpec((1, h, LANE), lambda b, pt: (b, 0, 0)),
            pl.BlockSpec((1, 1, KV_LORA), lambda b, pt: (b, 0, 0)),
            pl.BlockSpec((1, 1, QK_ROPE), lambda b, pt: (b, 0, 0)),
            pl.BlockSpec(memory_space=pl.ANY),
            pl.BlockSpec(memory_space=pl.ANY),
        ],
        out_specs=pl.BlockSpec((1, h, KV_LORA), lambda b, pt: (b, 0, 0)),
        scratch_shapes=[
            pltpu.VMEM((2, past, KV_LORA), F32),
            pltpu.VMEM((2, QK_ROPE, past), F32),
            pltpu.VMEM((past, KV_LORA), BF16),
            pltpu.VMEM((h, past), F32),
            pltpu.SemaphoreType.DMA((2, 2)),
        ],
    )
    return pl.pallas_call(
        functools.partial(_paged_kernel, layer=layer, n_pages=n_pages),
        grid_spec=grid_spec,
        out_shape=jax.ShapeDtypeStruct((bsz, h, KV_LORA), F32),
        compiler_params=_params("arbitrary"),
        name="paged_attention",
    )(page_table.reshape(-1), qa, qp, c_new, k_new, cache_kv, cache_krt)


def _uv_kernel(ol_ref, wuv_ref, g_ref, o_ref):
    for h in range(MLA_HEADS):
        o = _dot(ol_ref[h].astype(BF16), wuv_ref[h])
        sl = slice(h * V_DIM, (h + 1) * V_DIM)
        o_ref[:, sl] = (o * _silu(g_ref[:, sl])).astype(o_ref.dtype)


def uv_project(o_lat, wuv3, proj):
    h, t, _ = o_lat.shape
    return pl.pallas_call(
        _uv_kernel,
        grid=(1,),
        in_specs=[
            _full((h, t, KV_LORA)), _full((h, KV_LORA, V_DIM)),
            pl.BlockSpec((t, MLA_WIDTH), lambda i: (0, COL_GM // MLA_WIDTH)),
        ],
        out_specs=_full((t, MLA_WIDTH)),
        out_shape=jax.ShapeDtypeStruct((t, MLA_WIDTH), BF16),
        compiler_params=_params("arbitrary"),
        name="uv_project",
    )(o_lat, wuv3, proj)


def _proj_norm_res_kernel(*refs, n_in):
    ins, ws = refs[:n_in], refs[n_in:2 * n_in]
    x_ref, g_ref, o_ref = refs[2 * n_in:]
    y = _dot(ins[0][...].astype(BF16), ws[0][...])
    for a, w in zip(ins[1:], ws[1:]):
        y = y + _dot(a[...].astype(BF16), w[...])
    o_ref[...] = x_ref[...] + _rms(y, g_ref[...])


def proj_norm_res(ins, ws, x, g, *, tm, name):
    t, d = x.shape
    n_in = len(ins)
    return pl.pallas_call(
        functools.partial(_proj_norm_res_kernel, n_in=n_in),
        grid=(t // tm,),
        in_specs=([pl.BlockSpec((tm, a.shape[1]), lambda i: (i, 0)) for a in ins]
                  + [_full(w.shape) for w in ws]
                  + [pl.BlockSpec((tm, d), lambda i: (i, 0)), _full((1, d))]),
        out_specs=pl.BlockSpec((tm, d), lambda i: (i, 0)),
        out_shape=jax.ShapeDtypeStruct((t, d), F32),
        compiler_params=_params("parallel"),
        name=name,
    )(*ins, *ws, x, g.reshape(1, d))


def _cross_prompt_kernel(x_ref, gpre_ref, wq_ref, mk_ref, mv_ref, wo_ref, gpost_ref, o_ref):
    x = x_ref[...]
    hn = _rms(x, gpre_ref[...]).astype(BF16)
    q = (_dot(hn, wq_ref[...]) * MEM_SCALE).astype(BF16)
    outs = []
    for h in range(MEM_HEADS):
        sl = slice(h * MEM_HEAD_DIM, (h + 1) * MEM_HEAD_DIM)
        s = _dot_nt(q[:, sl], mk_ref[:, sl].astype(BF16))
        m = jnp.max(s, axis=-1, keepdims=True)
        p = jnp.exp(s - m)
        l = jnp.sum(p, axis=-1, keepdims=True)
        outs.append((_dot(p.astype(BF16), mv_ref[:, sl].astype(BF16)) / l).astype(BF16))
    o = jnp.concatenate(outs, axis=-1)
    y = _dot(o, wo_ref[...])
    o_ref[...] = x + _rms(y, gpost_ref[...])


def cross_prompt(x, g_pre, wq, mem_kv, wo, g_post, *, seq, tm):
    t, d = x.shape
    per_seq = seq // tm
    return pl.pallas_call(
        _cross_prompt_kernel,
        grid=(t // tm,),
        in_specs=[
            pl.BlockSpec((tm, d), lambda i: (i, 0)),
            _full((1, d)), _full((d, MEM_WIDTH)),
            pl.BlockSpec((N_MEM, MEM_WIDTH), lambda i: (i // per_seq, 0)),
            pl.BlockSpec((N_MEM, MEM_WIDTH), lambda i: (i // per_seq, 1)),
            _full((MEM_WIDTH, d)), _full((1, d)),
        ],
        out_specs=pl.BlockSpec((tm, d), lambda i: (i, 0)),
        out_shape=jax.ShapeDtypeStruct((t, d), F32),
        compiler_params=_params("parallel"),
        name="cross_prompt",
    )(x, g_pre.reshape(1, d), wq, mem_kv, mem_kv, wo, g_post.reshape(1, d))


def _cross_sample_kernel(q_ref, mk_ref, mv_ref, o_ref):
    bt = q_ref.shape[0]
    q = (q_ref[...] * MEM_SCALE).astype(BF16)
    for h in range(MEM_HEADS):
        sl = slice(h * MEM_HEAD_DIM, (h + 1) * MEM_HEAD_DIM)
        rows = pl.ds(h, N_MEM, stride=MEM_HEADS)
        qh = jnp.broadcast_to(q[:, :, sl], (bt, 8, MEM_HEAD_DIM))
        kh = mk_ref[:, 0, rows, :].astype(BF16)
        vh = mv_ref[:, 0, rows, :].astype(BF16)
        s = jnp.einsum("bqe,bme->bqm", qh, kh, preferred_element_type=F32)
        m = jnp.max(s, axis=-1, keepdims=True)
        p = jnp.exp(s - m)
        l = jnp.sum(p, axis=-1, keepdims=True)
        o = jnp.einsum("bqm,bme->bqe", p.astype(BF16), vh, preferred_element_type=F32) / l
        o_ref[:, :, sl] = o[:, 0:1, :].astype(o_ref.dtype)


def cross_sample_core(q3, mem_k, mem_v, *, layer, bt):
    bsz = q3.shape[0]
    mem = lambda: pl.BlockSpec((bt, 1, N_MEM * MEM_HEADS, MEM_HEAD_DIM), lambda i: (i, layer, 0, 0))
    return pl.pallas_call(
        _cross_sample_kernel,
        grid=(bsz // bt,),
        in_specs=[pl.BlockSpec((bt, 1, MEM_WIDTH), lambda i: (i, 0, 0)), mem(), mem()],
        out_specs=pl.BlockSpec((bt, 1, MEM_WIDTH), lambda i: (i, 0, 0)),
        out_shape=jax.ShapeDtypeStruct((bsz, 1, MEM_WIDTH), BF16),
        compiler_params=_params("parallel"),
        name="cross_sample",
    )(q3, mem_k, mem_v)


def _rope_tables(pos):
    half = QK_ROPE // 2
    inv_freq = ROPE_BASE ** (-jnp.arange(half, dtype=F32) / half)
    ang = pos.astype(F32)[:, None] * inv_freq[None, :]
    cos, sin = jnp.cos(ang), jnp.sin(ang)
    z = jnp.zeros_like(cos)
    return (jnp.concatenate([cos, cos, z, z], axis=-1),
            jnp.concatenate([-sin, z, z, z], axis=-1),
            jnp.concatenate([z, sin, z, z], axis=-1))


def _prep_layer(l, w_uq, w_uk, w_uv, w_out, w_xq, w_mk, w_mv, w_xo):
    d = w_out.shape[2]
    uq = w_uq[l]
    wqn = uq[:, :, :QK_NOPE].reshape(Q_LORA, -1).astype(BF16)
    wqp = jnp.pad(uq[:, :, QK_NOPE:], ((0, 0), (0, 0), (0, LANE - QK_ROPE))).reshape(Q_LORA, -1).astype(BF16)
    return dict(
        wqn=wqn, wqp=wqp,
        wuk=w_uk[l].reshape(KV_LORA, -1).astype(BF16),
        wuv=w_uv[l].reshape(KV_LORA, -1).astype(BF16),
        wukt=w_uk[l].transpose(1, 2, 0).astype(BF16),
        wuv3=w_uv[l].transpose(1, 0, 2).astype(BF16),
        w_out_c=w_out[l][:CONV_CH].astype(BF16),
        w_out_m=w_out[l][CONV_CH:].astype(BF16),
        w_xq=w_xq[l].reshape(d, -1).astype(BF16),
        w_mkv=jnp.concatenate([w_mk[l].reshape(d, -1), w_mv[l].reshape(d, -1)], axis=1).astype(BF16),
        w_xo=w_xo[l].reshape(-1, d).astype(BF16),
    )


def kernel(x_prompt, x_sample, mem_prompt, cache_kv_latent, cache_k_rope, state_conv, cache_mem_k, cache_mem_v, page_table, norm_mix_pre, w_in, conv_w, conv_b, conv_ln_g, conv_ln_b, q_norm_g, w_uq, kv_norm_g, w_uk, w_uv, w_out, norm_mix_post, norm_x_pre, norm_mem, w_xq, w_mk, w_mv, w_xo, norm_x_post):
    b_p, s_p, d = x_prompt.shape
    b_s = x_sample.shape[0]
    depth = w_in.shape[0]
    n_pages = page_table.shape[1]
    t_p = b_p * s_p
    tm_p = min(512, s_p)

    tabs_p = _rope_tables(jnp.arange(s_p, dtype=jnp.int32))
    tabs_s = _rope_tables(jnp.full((b_s,), n_pages * PAGE_SIZE, jnp.int32))
    mem2 = mem_prompt.reshape(b_p * N_MEM, d)
    mem_k_s = cache_mem_k.reshape(b_s, depth, N_MEM * MEM_HEADS, MEM_HEAD_DIM)
    mem_v_s = cache_mem_v.reshape(b_s, depth, N_MEM * MEM_HEADS, MEM_HEAD_DIM)
    cache_krt = cache_k_rope.transpose(0, 1, 3, 2)
    state_t = state_conv.transpose(0, 2, 1, 3)
    conv_w_t = conv_w.transpose(1, 0, 2)
    new_state_t, conv_part = conv_state_shift(state_t, conv_w_t, bt=8)
    w_in_t = w_in_prep(w_in.transpose(0, 2, 1))

    xp = x_prompt.reshape(t_p, d)
    xs = x_sample.reshape(b_s, d)
    lat_p, kpe_p, conv_p, mk_p, mv_p, lat_s, kpe_s = [], [], [], [], [], [], []
    u_s = []
    for l in range(depth):
        w = _prep_layer(l, w_uq, w_uk, w_uv, w_out, w_xq, w_mk, w_mv, w_xo)

        proj = rms_matmul(xp, norm_mix_pre[l], w_in_t, tm=min(1024, s_p), tn=1024, name="in_proj_p",
                          w_transposed=True, layer=l)
        conv_out, conv_state = conv_prompt(proj, conv_w[l], conv_b[l], conv_ln_g[l], conv_ln_b[l],
                                           bsz=b_p, seq=s_p)
        q, k, v, c_kv, k_pe = mla_prompt(proj, tabs_p, q_norm_g[l], kv_norm_g[l],
                                         w["wqn"], w["wqp"], w["wuk"], w["wuv"], tm=tm_p)
        attn = flash_prompt(q, k, v, proj, bsz=b_p, seq=s_p)
        xp = proj_norm_res([conv_out, attn], [w["w_out_c"], w["w_out_m"]], xp, norm_mix_post[l],
                           tm=tm_p, name="out_proj_p")
        mem_kv = rms_matmul(mem2, norm_mem[l], w["w_mkv"], tm=N_MEM, tn=2 * MEM_WIDTH, name="mem_kv")
        xp = cross_prompt(xp, norm_x_pre[l], w["w_xq"], mem_kv, w["w_xo"], norm_x_post[l],
                          seq=s_p, tm=tm_p)
        lat_p.append(c_kv.reshape(b_p, s_p, KV_LORA))
        kpe_p.append(k_pe.reshape(b_p, s_p, QK_ROPE))
        conv_p.append(conv_state)
        mk_p.append(mem_kv[:, :MEM_WIDTH].reshape(b_p, N_MEM, MEM_HEADS, MEM_HEAD_DIM))
        mv_p.append(mem_kv[:, MEM_WIDTH:].reshape(b_p, N_MEM, MEM_HEADS, MEM_HEAD_DIM))

        proj_s = rms_matmul(xs, norm_mix_pre[l], w_in_t, tm=b_s, tn=1024, name="in_proj_s",
                            w_transposed=True, layer=l)
        conv_out_s, u_l = conv_sample(proj_s, conv_part, l, conv_w[l], conv_b[l], conv_ln_g[l], conv_ln_b[l])
        u_s.append(u_l)
        qa, qp, c_new, k_new = mla_sample(proj_s, tabs_s, q_norm_g[l], kv_norm_g[l],
                                          w["wqn"], w["wqp"], w["wukt"])
        o_lat = paged_attention(page_table, qa.transpose(1, 0, 2), qp.transpose(1, 0, 2),
                                c_new.reshape(b_s, 1, KV_LORA), k_new.reshape(b_s, 1, QK_ROPE),
                                cache_kv_latent, cache_krt, layer=l)
        attn_s = uv_project(o_lat.transpose(1, 0, 2), w["wuv3"], proj_s)
        xs = proj_norm_res([conv_out_s, attn_s], [w["w_out_c"], w["w_out_m"]],
                           xs, norm_mix_post[l], tm=b_s, name="out_proj_s")
        q_x = rms_matmul(xs, norm_x_pre[l], w["w_xq"], tm=b_s, tn=MEM_WIDTH, name="cross_q_s")
        o_x = cross_sample_core(q_x.reshape(b_s, 1, MEM_WIDTH), mem_k_s, mem_v_s, layer=l, bt=8)
        xs = proj_norm_res([o_x.reshape(b_s, MEM_WIDTH)], [w["w_xo"]], xs, norm_x_post[l],
                           tm=b_s, name="cross_out_s")
        lat_s.append(c_new.reshape(b_s, 1, KV_LORA))
        kpe_s.append(k_new.reshape(b_s, 1, QK_ROPE))

    return (xp.reshape(b_p, s_p, d), xs.reshape(b_s, 1, d),
            jnp.stack(lat_p, axis=1), jnp.stack(kpe_p, axis=1), jnp.stack(conv_p, axis=1),
            jnp.stack(mk_p, axis=1), jnp.stack(mv_p, axis=1),
            jnp.stack(lat_s, axis=1), jnp.stack(kpe_s, axis=1),
            conv_state_finish(new_state_t, jnp.stack(u_s, axis=1)[:, None]).transpose(0, 2, 1, 3))
```

```python
import functools

import jax
import jax.numpy as jnp
from jax import lax
from jax.experimental import pallas as pl
from jax.experimental.pallas import tpu as pltpu

F32 = jnp.float32
BF16 = jnp.bfloat16

D_MODEL = 2048
CONV_CH = 1024
CONV_WIDTH = 31
CONV_STATE = CONV_WIDTH - 1
MLA_HEADS = 8
QK_NOPE = 128
QK_ROPE = 64
V_DIM = 128
MLA_WIDTH = MLA_HEADS * V_DIM
Q_LORA = 512
KV_LORA = 256
MLA_SCALE = (QK_NOPE + QK_ROPE) ** -0.5
ROPE_BASE = 10000.0
PAGE_SIZE = 128
N_MEM = 256
MEM_HEADS = 4
MEM_HEAD_DIM = 128
MEM_WIDTH = MEM_HEADS * MEM_HEAD_DIM
MEM_SCALE = MEM_HEAD_DIM ** -0.5
EPS = 1e-6
LOG2E = 1.4426950408889634

LANE = 128
QK_PAD = 256
NEG_BIG = -1e30

PROJ_COLS = 5120
COL_A, COL_B, COL_GC, COL_GM, COL_QL, COL_KV, COL_KPE = 0, 1024, 2048, 3072, 4096, 4608, 4864

VMEM_LIMIT = 52 * 1024 * 1024


def _params(*sem):
    return pltpu.CompilerParams(dimension_semantics=sem, vmem_limit_bytes=VMEM_LIMIT)


def _rms(x, g):
    return x * lax.rsqrt(jnp.mean(x * x, axis=-1, keepdims=True) + EPS) * g


def _silu(x):
    return x * jax.nn.sigmoid(x)


def _dot(a, b):
    return jnp.dot(a, b, preferred_element_type=F32)


def _dot_nt(a, b):
    return lax.dot_general(a, b, (((1,), (1,)), ((), ())), preferred_element_type=F32)


def _rope128(x, cos_t, sin_n, sin_p):
    return x * cos_t + pltpu.roll(x, 96, 1) * sin_n + pltpu.roll(x, 32, 1) * sin_p


def _rms_matmul_kernel(x_ref, g_ref, w_ref, o_ref, hn_ref, *, w_transposed):
    @pl.when(pl.program_id(1) == 0)
    def _():
        hn_ref[...] = _rms(x_ref[...], g_ref[...]).astype(BF16)

    dot = _dot_nt if w_transposed else _dot
    o_ref[...] = dot(hn_ref[...], w_ref[...]).astype(o_ref.dtype)


def rms_matmul(x, g, w, *, tm, tn, name, w_transposed=False, layer=None):
    t, k = x.shape
    n = w.shape[-2] if w_transposed else w.shape[-1]
    lead = () if layer is None else (None,)
    pick = () if layer is None else (layer,)
    w_spec = (pl.BlockSpec(lead + (tn, k), lambda i, j: pick + (j, 0)) if w_transposed
              else pl.BlockSpec(lead + (k, tn), lambda i, j: pick + (0, j)))
    return pl.pallas_call(
        functools.partial(_rms_matmul_kernel, w_transposed=w_transposed),
        grid=(t // tm, n // tn),
        in_specs=[
            pl.BlockSpec((tm, k), lambda i, j: (i, 0)),
            pl.BlockSpec((1, k), lambda i, j: (0, 0)),
            w_spec,
        ],
        out_specs=pl.BlockSpec((tm, tn), lambda i, j: (i, j)),
        out_shape=jax.ShapeDtypeStruct((t, n), F32),
        scratch_shapes=[pltpu.VMEM((tm, k), BF16)],
        compiler_params=_params("parallel", "arbitrary"),
        name=name,
    )(x, g.reshape(1, k), w)


_W_IN_GROUPS = (
    (COL_A, 0, 3 * CONV_CH),
    (COL_GM, 3 * CONV_CH + Q_LORA + KV_LORA + QK_ROPE, MLA_WIDTH),
    (COL_QL, 3 * CONV_CH, Q_LORA),
    (COL_KV, 3 * CONV_CH + Q_LORA, KV_LORA),
    (COL_KPE, 3 * CONV_CH + Q_LORA + KV_LORA, QK_ROPE),
)
W_PREP_TK = 512


def _w_in_prep_kernel(w_ref, o_ref):
    for dst, src, rows in _W_IN_GROUPS:
        o_ref[0, dst:dst + rows, :] = w_ref[0, src:src + rows, :].astype(BF16)
    pad0 = COL_KPE + QK_ROPE
    o_ref[0, pad0:, :] = jnp.zeros((PROJ_COLS - pad0, o_ref.shape[2]), BF16)


def w_in_prep(w_in_t):
    depth, n_in, k = w_in_t.shape
    return pl.pallas_call(
        _w_in_prep_kernel,
        grid=(depth, k // W_PREP_TK),
        in_specs=[pl.BlockSpec((1, n_in, W_PREP_TK), lambda l, j: (l, 0, j))],
        out_specs=pl.BlockSpec((1, PROJ_COLS, W_PREP_TK), lambda l, j: (l, 0, j)),
        out_shape=jax.ShapeDtypeStruct((depth, PROJ_COLS, k), BF16),
        compiler_params=_params("parallel", "parallel"),
        name="w_in_prep",
    )(w_in_t)


CONV_SEGS = 8
CONV_SEG_ROWS = 128
CONV_HALO = 32
CONV_BLOCK = 16
CONV_TAPS_A = 16
CONV_UNROLL = 16
CONV_PAD = 32
CONV_TILE = CONV_SEGS * CONV_SEG_ROWS


def _conv_prompt_kernel(proj_hbm, w_ref, cb_ref, lg_ref, lb_ref, out_hbm, state_ref,
                        xa, xb, xg, y_ref, o_ref, wb_ref, lgb_ref, in_sems, out_sem, *, tiles_per_seq):
    n = pl.program_id(0)
    n_tiles = pl.num_programs(0)
    slot = n % 2
    lrows, halo = CONV_SEG_ROWS, CONV_HALO

    def in_copies(tile, sl, seg, first):
        row = tile * CONV_TILE + seg * lrows
        if first:
            src_rows, dst_rows = pl.ds(row, lrows), pl.ds(halo, lrows)
        else:
            src_rows, dst_rows = pl.ds(row - halo, lrows + halo), pl.ds(0, lrows + halo)
        return (
            pltpu.make_async_copy(proj_hbm.at[src_rows, pl.ds(COL_A, CONV_CH)],
                                  xa.at[sl, dst_rows, seg, :], in_sems.at[sl, 0]),
            pltpu.make_async_copy(proj_hbm.at[src_rows, pl.ds(COL_B, CONV_CH)],
                                  xb.at[sl, dst_rows, seg, :], in_sems.at[sl, 1]),
            pltpu.make_async_copy(proj_hbm.at[pl.ds(row, lrows), pl.ds(COL_GC, CONV_CH)],
                                  xg.at[sl, :, seg, :], in_sems.at[sl, 2]),
        )

    def for_each_in_copy(tile, sl, fn):
        seq_start = tile % tiles_per_seq == 0

        @pl.when(seq_start)
        def _():
            for c in in_copies(tile, sl, 0, True):
                fn(c)

        @pl.when(jnp.logical_not(seq_start))
        def _():
            for c in in_copies(tile, sl, 0, False):
                fn(c)

        for seg in range(1, CONV_SEGS):
            for c in in_copies(tile, sl, seg, False):
                fn(c)

    def out_copies(tile):
        return [pltpu.make_async_copy(o_ref.at[:, seg, :],
                                      out_hbm.at[pl.ds(tile * CONV_TILE + seg * lrows, lrows), :], out_sem.at[0])
                for seg in range(CONV_SEGS)]

    @pl.when(n == 0)
    def _():
        for k in range(CONV_WIDTH):
            wb_ref[k] = jnp.broadcast_to(w_ref[k:k + 1, :], (CONV_SEGS, CONV_CH))
        lgb_ref[0] = jnp.broadcast_to(lg_ref[...], (CONV_SEGS, CONV_CH))
        lgb_ref[1] = jnp.broadcast_to(lb_ref[...], (CONV_SEGS, CONV_CH))
        for_each_in_copy(n, slot, lambda c: c.start())

    @pl.when(n + 1 < n_tiles)
    def _():
        for_each_in_copy(n + 1, 1 - slot, lambda c: c.start())

    for_each_in_copy(n, slot, lambda c: c.wait())

    def glu(i, carry):
        xa[slot, i] = xa[slot, i] * jax.nn.sigmoid(xb[slot, i])
        return carry

    lax.fori_loop(0, lrows + halo, glu, 0, unroll=4)

    @pl.when(n % tiles_per_seq == 0)
    def _():
        xa[slot, 0:halo, 0:1, :] = jnp.zeros((halo, 1, CONV_CH), F32)

    @pl.when(n % tiles_per_seq == tiles_per_seq - 1)
    def _():
        state_ref[0] = xa[slot, halo + lrows - CONV_STATE:halo + lrows, CONV_SEGS - 1, :]

    y_ref[0:CONV_PAD - CONV_TAPS_A - 1] = jnp.zeros((CONV_PAD - CONV_TAPS_A - 1, CONV_SEGS, CONV_CH), F32)
    shift = halo - CONV_STATE
    zero = jnp.zeros((CONV_SEGS, LANE), F32)
    for k0, n_taps in ((0, CONV_TAPS_A), (CONV_TAPS_A, CONV_WIDTH - CONV_TAPS_A)):
        first = k0 == 0
        for c0 in range(0, CONV_CH, LANE):
            cols = slice(c0, c0 + LANE)
            w = [wb_ref[k0 + t, :, cols] for t in range(n_taps)]
            fresh = (cb_ref[:, cols] + zero) if first else zero
            row_off = CONV_PAD - shift - k0 - (n_taps - 1)

            def body(it, carry, w=w, fresh=fresh, cols=cols, row_off=row_off, first=first):
                acc = list(carry)
                j0 = it * CONV_UNROLL
                for jj in range(CONV_UNROLL):
                    u = xa[slot, j0 + jj, :, cols]
                    acc = [a + wt * u for wt, a in zip(w, [fresh] + acc)]
                    done = acc.pop()
                    if first:
                        y_ref[j0 + (jj + row_off), :, cols] = done
                    else:
                        y_ref[j0 + (jj + row_off), :, cols] = y_ref[j0 + (jj + row_off), :, cols] + done
                return tuple(acc)

            lax.fori_loop(0, (lrows + halo) // CONV_UNROLL, body, tuple(zero for _ in range(n_taps - 1)))

    @pl.when(n > 0)
    def _():
        for c in out_copies(n - 1):
            c.wait()

    def norm_gate(r, carry):
        for ii in range(CONV_BLOCK):
            i = r * CONV_BLOCK + ii
            acc = y_ref[i + CONV_PAD]
            mu = jnp.mean(acc, axis=-1, keepdims=True)
            d = acc - mu
            var = jnp.mean(d * d, axis=-1, keepdims=True)
            z = d * lax.rsqrt(var + EPS) * lgb_ref[0] + lgb_ref[1]
            o_ref[i] = _silu(z) * _silu(xg[slot, i])
        return carry

    lax.fori_loop(0, lrows // CONV_BLOCK, norm_gate, 0)

    for c in out_copies(n):
        c.start()

    @pl.when(n == n_tiles - 1)
    def _():
        for c in out_copies(n):
            c.wait()


def conv_prompt(proj, conv_w, conv_b, ln_g, ln_b, *, bsz, seq):
    tiles_per_seq = seq // CONV_TILE
    vec = lambda: pl.BlockSpec((1, CONV_CH), lambda n: (0, 0))
    seg_buf = lambda rows: pltpu.VMEM((2, rows, CONV_SEGS, CONV_CH), F32)
    return pl.pallas_call(
        functools.partial(_conv_prompt_kernel, tiles_per_seq=tiles_per_seq),
        grid=(bsz * tiles_per_seq,),
        in_specs=[
            pl.BlockSpec(memory_space=pl.ANY),
            pl.BlockSpec((CONV_WIDTH, CONV_CH), lambda n: (0, 0)),
            vec(), vec(), vec(),
        ],
        out_specs=[
            pl.BlockSpec(memory_space=pl.ANY),
            pl.BlockSpec((1, CONV_STATE, CONV_CH), lambda n: (n // tiles_per_seq, 0, 0)),
        ],
        out_shape=[
            jax.ShapeDtypeStruct((bsz * seq, CONV_CH), F32),
            jax.ShapeDtypeStruct((bsz, CONV_STATE, CONV_CH), F32),
        ],
        scratch_shapes=[
            seg_buf(CONV_SEG_ROWS + CONV_HALO), seg_buf(CONV_SEG_ROWS + CONV_HALO), seg_buf(CONV_SEG_ROWS),
            pltpu.VMEM((CONV_SEG_ROWS + 2 * CONV_PAD, CONV_SEGS, CONV_CH), F32),
            pltpu.VMEM((CONV_SEG_ROWS, CONV_SEGS, CONV_CH), F32),
            pltpu.VMEM((CONV_WIDTH, CONV_SEGS, CONV_CH), F32),
            pltpu.VMEM((2, CONV_SEGS, CONV_CH), F32),
            pltpu.SemaphoreType.DMA((2, 3)), pltpu.SemaphoreType.DMA((1,)),
        ],
        compiler_params=_params("arbitrary"),
        name="conv_prompt",
    )(proj, conv_w, conv_b.reshape(1, -1), ln_g.reshape(1, -1), ln_b.reshape(1, -1))


def _conv_state_kernel(st_ref, w_ref, newst_ref, part_ref):
    st = st_ref[...]
    newst_ref[:, 0:CONV_STATE - 1] = st[:, 1:CONV_STATE]
    newst_ref[:, CONV_STATE - 1:CONV_STATE] = jnp.zeros_like(st[:, 0:1])
    part_ref[...] = jnp.sum(st * w_ref[0:CONV_STATE][None], axis=1)


def conv_state_shift(state_t, conv_w_t, *, bt):
    bsz, _, depth, _ = state_t.shape
    blk = pl.BlockSpec((bt, CONV_STATE, depth, CONV_CH), lambda i: (i, 0, 0, 0))
    return pl.pallas_call(
        _conv_state_kernel,
        grid=(bsz // bt,),
        in_specs=[blk, pl.BlockSpec((CONV_WIDTH, depth, CONV_CH), lambda i: (0, 0, 0))],
        out_specs=[blk, pl.BlockSpec((bt, depth, CONV_CH), lambda i: (i, 0, 0))],
        out_shape=[jax.ShapeDtypeStruct(state_t.shape, F32),
                   jax.ShapeDtypeStruct((bsz, depth, CONV_CH), F32)],
        compiler_params=_params("parallel"),
        name="conv_state_shift",
    )(state_t, conv_w_t)


def _conv_state_finish_kernel(u_ref, st_hbm, o_ref):
    del st_hbm
    o_ref[...] = u_ref[...]


def conv_state_finish(new_state, u_all):
    bsz, _, depth, _ = new_state.shape
    return pl.pallas_call(
        _conv_state_finish_kernel,
        grid=(1,),
        in_specs=[pl.BlockSpec((bsz, 1, depth, CONV_CH), lambda i: (0, 0, 0, 0)),
                  pl.BlockSpec(memory_space=pl.ANY)],
        out_specs=pl.BlockSpec((bsz, 1, depth, CONV_CH), lambda i: (0, CONV_STATE - 1, 0, 0)),
        out_shape=jax.ShapeDtypeStruct(new_state.shape, F32),
        input_output_aliases={1: 0},
        compiler_params=_params("arbitrary"),
        name="conv_state_finish",
    )(u_all, new_state)


def _conv_sample_kernel(a_ref, b_ref, gc_ref, part_ref, w_ref, cb_ref, lg_ref, lb_ref, out_ref, u_ref, *, layer):
    u = a_ref[...] * jax.nn.sigmoid(b_ref[...])
    u_ref[...] = u
    y = part_ref[:, layer, :] + u * w_ref[CONV_STATE:CONV_WIDTH, :] + cb_ref[...]
    mu = jnp.mean(y, axis=-1, keepdims=True)
    d = y - mu
    var = jnp.mean(d * d, axis=-1, keepdims=True)
    z = d * lax.rsqrt(var + EPS) * lg_ref[...] + lb_ref[...]
    out_ref[...] = (_silu(z) * _silu(gc_ref[...])).astype(out_ref.dtype)


def conv_sample(proj, part, layer, conv_w, conv_b, ln_g, ln_b):
    bsz = proj.shape[0]
    vec = lambda: pl.BlockSpec((1, CONV_CH), lambda i: (0, 0))
    row = lambda: pl.BlockSpec((bsz, CONV_CH), lambda i: (0, 0))
    return pl.pallas_call(
        functools.partial(_conv_sample_kernel, layer=layer),
        grid=(1,),
        in_specs=[
            pl.BlockSpec((bsz, CONV_CH), lambda i: (0, COL_A // CONV_CH)),
            pl.BlockSpec((bsz, CONV_CH), lambda i: (0, COL_B // CONV_CH)),
            pl.BlockSpec((bsz, CONV_CH), lambda i: (0, COL_GC // CONV_CH)),
            pl.BlockSpec(part.shape, lambda i: (0, 0, 0)),
            pl.BlockSpec((CONV_WIDTH, CONV_CH), lambda i: (0, 0)),
            vec(), vec(), vec(),
        ],
        out_specs=[row(), row()],
        out_shape=[
            jax.ShapeDtypeStruct((bsz, CONV_CH), BF16),
            jax.ShapeDtypeStruct((bsz, CONV_CH), F32),
        ],
        compiler_params=_params("arbitrary"),
        name="conv_sample",
    )(proj, proj, proj, part, conv_w, conv_b.reshape(1, -1), ln_g.reshape(1, -1), ln_b.reshape(1, -1))


def _mla_common(ql_ref, kvl_ref, kpe_ref, cos_ref, sn_ref, sp_ref, qg_ref, kvg_ref, wqn_ref, wqp_ref):
    qn = _rms(ql_ref[...], qg_ref[...]).astype(BF16)
    q_nope = _dot(qn, wqn_ref[...]) * (MLA_SCALE * LOG2E)
    q_rope_raw = _dot(qn, wqp_ref[...]) * (MLA_SCALE * LOG2E)
    cos_t, sin_n, sin_p = cos_ref[...], sn_ref[...], sp_ref[...]
    q_rope = [_rope128(q_rope_raw[:, h * LANE:(h + 1) * LANE], cos_t, sin_n, sin_p)
              for h in range(MLA_HEADS)]
    c_kv = _rms(kvl_ref[...], kvg_ref[...])
    k_pe = _rope128(kpe_ref[...], cos_t, sin_n, sin_p)
    return q_nope, q_rope, c_kv, k_pe


def _mla_prompt_kernel(ql_ref, kvl_ref, kpe_ref, cos_ref, sn_ref, sp_ref, qg_ref, kvg_ref,
                       wqn_ref, wqp_ref, wuk_ref, wuv_ref,
                       q_ref, k_ref, v_ref, ckv_ref, kpeo_ref):
    q_nope, q_rope, c_kv, k_pe = _mla_common(ql_ref, kvl_ref, kpe_ref, cos_ref, sn_ref, sp_ref,
                                             qg_ref, kvg_ref, wqn_ref, wqp_ref)
    ckv_ref[...] = c_kv
    kpeo_ref[...] = k_pe[:, :QK_ROPE]
    c_bf = c_kv.astype(BF16)
    k_nope = _dot(c_bf, wuk_ref[...])
    v = _dot(c_bf, wuv_ref[...])
    for h in range(MLA_HEADS):
        v_ref[h * V_DIM:(h + 1) * V_DIM, :] = v[:, h * V_DIM:(h + 1) * V_DIM].T.astype(BF16)
    k_pe_bf = k_pe.astype(BF16)
    for h in range(MLA_HEADS):
        lo = h * QK_PAD
        q_ref[:, lo:lo + LANE] = q_nope[:, h * LANE:(h + 1) * LANE].astype(BF16)
        q_ref[:, lo + LANE:lo + QK_PAD] = q_rope[h].astype(BF16)
        k_ref[:, lo:lo + LANE] = k_nope[:, h * LANE:(h + 1) * LANE].astype(BF16)
        k_ref[:, lo + LANE:lo + QK_PAD] = k_pe_bf


def _proj_specs(tm, n_tab):
    return [
        pl.BlockSpec((tm, Q_LORA), lambda i: (i, COL_QL // Q_LORA)),
        pl.BlockSpec((tm, KV_LORA), lambda i: (i, COL_KV // KV_LORA)),
        pl.BlockSpec((tm, LANE), lambda i: (i, COL_KPE // LANE)),
        pl.BlockSpec((tm, LANE), lambda i: (i % n_tab, 0)),
        pl.BlockSpec((tm, LANE), lambda i: (i % n_tab, 0)),
        pl.BlockSpec((tm, LANE), lambda i: (i % n_tab, 0)),
    ]


def _full(shape):
    return pl.BlockSpec(shape, lambda i: (0,) * len(shape))


def mla_prompt(proj, tabs, q_g, kv_g, wqn, wqp, wuk, wuv, *, tm):
    t = proj.shape[0]
    n_tab = tabs[0].shape[0] // tm
    h = MLA_HEADS
    return pl.pallas_call(
        _mla_prompt_kernel,
        grid=(t // tm,),
        in_specs=_proj_specs(tm, n_tab) + [
            _full((1, Q_LORA)), _full((1, KV_LORA)),
            _full((Q_LORA, h * LANE)), _full((Q_LORA, h * LANE)),
            _full((KV_LORA, h * LANE)), _full((KV_LORA, h * LANE)),
        ],
        out_specs=[
            pl.BlockSpec((tm, h * QK_PAD), lambda i: (i, 0)),
            pl.BlockSpec((tm, h * QK_PAD), lambda i: (i, 0)),
            pl.BlockSpec((h * V_DIM, tm), lambda i: (0, i)),
            pl.BlockSpec((tm, KV_LORA), lambda i: (i, 0)),
            pl.BlockSpec((tm, QK_ROPE), lambda i: (i, 0)),
        ],
        out_shape=[
            jax.ShapeDtypeStruct((t, h * QK_PAD), BF16),
            jax.ShapeDtypeStruct((t, h * QK_PAD), BF16),
            jax.ShapeDtypeStruct((h * V_DIM, t), BF16),
            jax.ShapeDtypeStruct((t, KV_LORA), F32),
            jax.ShapeDtypeStruct((t, QK_ROPE), F32),
        ],
        compiler_params=_params("parallel"),
        name="mla_prompt",
    )(proj, proj, proj, *tabs, q_g.reshape(1, -1), kv_g.reshape(1, -1), wqn, wqp, wuk, wuv)


def _mla_sample_kernel(ql_ref, kvl_ref, kpe_ref, cos_ref, sn_ref, sp_ref, qg_ref, kvg_ref,
                       wqn_ref, wqp_ref, wukt_ref,
                       qa_ref, qp_ref, ckv_ref, kpeo_ref):
    q_nope, q_rope, c_kv, k_pe = _mla_common(ql_ref, kvl_ref, kpe_ref, cos_ref, sn_ref, sp_ref,
                                             qg_ref, kvg_ref, wqn_ref, wqp_ref)
    ckv_ref[...] = c_kv
    kpeo_ref[...] = k_pe[:, :QK_ROPE]
    for h in range(MLA_HEADS):
        qa_ref[h] = _dot(q_nope[:, h * LANE:(h + 1) * LANE].astype(BF16), wukt_ref[h])
        qp_ref[h] = q_rope[h]


def mla_sample(proj, tabs, q_g, kv_g, wqn, wqp, wukt):
    t = proj.shape[0]
    h = MLA_HEADS
    return pl.pallas_call(
        _mla_sample_kernel,
        grid=(1,),
        in_specs=_proj_specs(t, 1) + [
            _full((1, Q_LORA)), _full((1, KV_LORA)),
            _full((Q_LORA, h * LANE)), _full((Q_LORA, h * LANE)),
            _full((h, QK_NOPE, KV_LORA)),
        ],
        out_specs=[
            _full((h, t, KV_LORA)), _full((h, t, LANE)),
            _full((t, KV_LORA)), _full((t, QK_ROPE)),
        ],
        out_shape=[
            jax.ShapeDtypeStruct((h, t, KV_LORA), F32),
            jax.ShapeDtypeStruct((h, t, LANE), F32),
            jax.ShapeDtypeStruct((t, KV_LORA), F32),
            jax.ShapeDtypeStruct((t, QK_ROPE), F32),
        ],
        compiler_params=_params("arbitrary"),
        name="mla_sample",
    )(proj, proj, proj, *tabs, q_g.reshape(1, -1), kv_g.reshape(1, -1), wqn, wqp, wukt)


FLASH_TQ = 512


def _flash_kernel(q_ref, k_ref, vt_ref, g_ref, o_ref, *, seq):
    tq = FLASH_TQ
    causal_t = (lax.broadcasted_iota(jnp.int32, (tq, tq), 0)
                <= lax.broadcasted_iota(jnp.int32, (tq, tq), 1))
    n_q = seq // tq

    def scores(i):
        kv = i * tq
        return _dot_nt(k_ref[0:kv + tq, :], q_ref[kv:kv + tq, :])

    s_next = scores(0)
    for i in range(n_q):
        kv = i * tq
        s_t = s_next
        if i + 1 < n_q:
            s_next = scores(i + 1)
        s_d = jnp.where(causal_t, s_t[kv:kv + tq], NEG_BIG)
        m = jnp.max(s_d, axis=0, keepdims=True)
        if i > 0:
            s_o = s_t[0:kv]
            m = jnp.maximum(m, jnp.max(s_o, axis=0, keepdims=True))
            p_o = jnp.exp2(s_o - m)
            l = jnp.sum(p_o, axis=0, keepdims=True)
            acc = _dot(vt_ref[:, 0:kv], p_o.astype(BF16))
        else:
            l = jnp.zeros((1, tq), F32)
            acc = jnp.zeros((V_DIM, tq), F32)
        p_d = jnp.exp2(s_d - m)
        l = l + jnp.sum(p_d, axis=0, keepdims=True)
        acc = acc + _dot(vt_ref[:, kv:kv + tq], p_d.astype(BF16))
        o = (acc / l).T
        o_ref[kv:kv + tq, :] = (o * _silu(g_ref[kv:kv + tq, :])).astype(o_ref.dtype)


def flash_prompt(q, k, v_t, proj, *, bsz, seq):
    h = MLA_HEADS
    return pl.pallas_call(
        functools.partial(_flash_kernel, seq=seq),
        grid=(bsz, h),
        in_specs=[
            pl.BlockSpec((seq, QK_PAD), lambda b, hh: (b, hh)),
            pl.BlockSpec((seq, QK_PAD), lambda b, hh: (b, hh)),
            pl.BlockSpec((V_DIM, seq), lambda b, hh: (hh, b)),
            pl.BlockSpec((seq, V_DIM), lambda b, hh: (b, COL_GM // V_DIM + hh)),
        ],
        out_specs=pl.BlockSpec((seq, V_DIM), lambda b, hh: (b, hh)),
        out_shape=jax.ShapeDtypeStruct((bsz * seq, h * V_DIM), BF16),
        compiler_params=_params("parallel", "parallel"),
        name="flash_prompt",
    )(q, k, v_t, proj)


PAGED_CHUNK = 1024
PAGED_GROUPS = 4


def _paged_kernel(pt_ref, qa_ref, qp_ref, cn_ref, kn_ref, ckv_hbm, krt_hbm, o_ref,
                  cbuf, kbuf, cbf, s_ref, sems, *, layer, n_pages):
    b = pl.program_id(0)
    nb = pl.num_programs(0)
    slot = b % 2

    def copies(bb, sl, p):
        page = pt_ref[bb * n_pages + p]
        toks = pl.ds(p * PAGE_SIZE, PAGE_SIZE)
        return (pltpu.make_async_copy(ckv_hbm.at[page, layer], cbuf.at[sl, toks, :], sems.at[0, sl]),
                pltpu.make_async_copy(krt_hbm.at[page, layer], kbuf.at[sl, :, toks], sems.at[1, sl]))

    def start_all(bb, sl):
        for p in range(n_pages):
            c1, c2 = copies(bb, sl, p)
            c1.start()
            c2.start()

    @pl.when(b == 0)
    def _():
        start_all(b, slot)

    @pl.when(b + 1 < nb)
    def _():
        start_all(b + 1, 1 - slot)

    for p in range(n_pages):
        c1, c2 = copies(b, slot, p)
        c1.wait()
        c2.wait()

    past = n_pages * PAGE_SIZE
    n_chunks = past // PAGED_CHUNK
    per_group = n_chunks // PAGED_GROUPS
    qa = qa_ref[0].astype(BF16)
    qp = qp_ref[0][:, :QK_ROPE].astype(BF16)

    def qk_chunk(c):
        toks = slice(c * PAGED_CHUNK, (c + 1) * PAGED_CHUNK)
        cc = cbuf[slot, toks, :].astype(BF16)
        kk = kbuf[slot, :, toks].astype(BF16)
        cbf[toks, :] = cc
        s_ref[:, toks] = _dot_nt(qa, cc) + _dot(qp, kk)

    def pv_chunk(c, m_g, l_g, acc_g):
        toks = slice(c * PAGED_CHUNK, (c + 1) * PAGED_CHUNK)
        p = jnp.exp2(s_ref[:, toks] - m_g)
        return l_g + jnp.sum(p, axis=-1, keepdims=True), acc_g + _dot(p.astype(BF16), cbf[toks, :])

    def group_max(g):
        g_toks = slice(g * per_group * PAGED_CHUNK, (g + 1) * per_group * PAGED_CHUNK)
        return jnp.max(s_ref[:, g_toks], axis=-1, keepdims=True)

    stats = []
    for c in range(per_group):
        qk_chunk(c)
    for g in range(PAGED_GROUPS):
        m_g = group_max(g)
        l_g = jnp.zeros((MLA_HEADS, 1), F32)
        acc_g = jnp.zeros((MLA_HEADS, KV_LORA), F32)
        for c in range(per_group):
            if g + 1 < PAGED_GROUPS:
                qk_chunk((g + 1) * per_group + c)
            l_g, acc_g = pv_chunk(g * per_group + c, m_g, l_g, acc_g)
        stats.append((m_g, l_g, acc_g))

    c_new = cn_ref[0].astype(BF16).astype(F32)
    k_new = kn_ref[0].astype(BF16).astype(F32)
    s_new = (jnp.sum(qa.astype(F32) * c_new, axis=-1, keepdims=True)
             + jnp.sum(qp.astype(F32) * k_new, axis=-1, keepdims=True))
    m = s_new
    for m_g, _, _ in stats:
        m = jnp.maximum(m, m_g)
    p_new = jnp.exp2(s_new - m)
    l = p_new
    acc = p_new.astype(BF16).astype(F32) * c_new
    for m_g, l_g, acc_g in stats:
        w_g = jnp.exp2(m_g - m)
        l = l + w_g * l_g
        acc = acc + w_g * acc_g
    o_ref[0] = acc / l


def paged_attention(page_table, qa, qp, c_new, k_new, cache_kv, cache_krt, *, layer):
    bsz, n_pages = page_table.shape
    past = n_pages * PAGE_SIZE
    h = MLA_HEADS
    grid_spec = pltpu.PrefetchScalarGridSpec(
        num_scalar_prefetch=1,
        grid=(bsz,),
        in_specs=[
            pl.BlockSpec((1, h, KV_LORA), lambda b, pt: (b, 0, 0)),
            pl.BlockSpec((1, h, LANE), lambda b, pt: (b, 0, 0)),
            pl.BlockSpec((1, 1, KV_LORA), lambda b, pt: (b, 0, 0)),
            pl.BlockSpec((1, 1, QK_ROPE), lambda b, pt: (b, 0, 0)),
            pl.BlockSpec(memory_space=pl.ANY),
            pl.BlockSpec(memory_space=pl.ANY),
        ],
        out_specs=pl.BlockSpec((1, h, KV_LORA), lambda b, pt: (b, 0, 0)),
        scratch_shapes=[
            pltpu.VMEM((2, past, KV_LORA), F32),
            pltpu.VMEM((2, QK_ROPE, past), F32),
            pltpu.VMEM((past, KV_LORA), BF16),
            pltpu.VMEM((h, past), F32),
            pltpu.SemaphoreType.DMA((2, 2)),
        ],
    )
    return pl.pallas_call(
        functools.partial(_paged_kernel, layer=layer, n_pages=n_pages),
        grid_spec=grid_spec,
        out_shape=jax.ShapeDtypeStruct((bsz, h, KV_LORA), F32),
        compiler_params=_params("arbitrary"),
        name="paged_attention",
    )(page_table.reshape(-1), qa, qp, c_new, k_new, cache_kv, cache_krt)


def _uv_kernel(ol_ref, wuv_ref, g_ref, o_ref):
    for h in range(MLA_HEADS):
        o = _dot(ol_ref[h].astype(BF16), wuv_ref[h])
        sl = slice(h * V_DIM, (h + 1) * V_DIM)
        o_ref[:, sl] = (o * _silu(g_ref[:, sl])).astype(o_ref.dtype)


def uv_project(o_lat, wuv3, proj):
    h, t, _ = o_lat.shape
    return pl.pallas_call(
        _uv_kernel,
        grid=(1,),
        in_specs=[
            _full((h, t, KV_LORA)), _full((h, KV_LORA, V_DIM)),
            pl.BlockSpec((t, MLA_WIDTH), lambda i: (0, COL_GM // MLA_WIDTH)),
        ],
        out_specs=_full((t, MLA_WIDTH)),
        out_shape=jax.ShapeDtypeStruct((t, MLA_WIDTH), BF16),
        compiler_params=_params("arbitrary"),
        name="uv_project",
    )(o_lat, wuv3, proj)


def _proj_norm_res_kernel(*refs, n_in):
    ins, ws = refs[:n_in], refs[n_in:2 * n_in]
    x_ref, g_ref, o_ref = refs[2 * n_in:]
    y = _dot(ins[0][...].astype(BF16), ws[0][...])
    for a, w in zip(ins[1:], ws[1:]):
        y = y + _dot(a[...].astype(BF16), w[...])
    o_ref[...] = x_ref[...] + _rms(y, g_ref[...])


def proj_norm_res(ins, ws, x, g, *, tm, name):
    t, d = x.shape
    n_in = len(ins)
    return pl.pallas_call(
        functools.partial(_proj_norm_res_kernel, n_in=n_in),
        grid=(t // tm,),
        in_specs=([pl.BlockSpec((tm, a.shape[1]), lambda i: (i, 0)) for a in ins]
                  + [_full(w.shape) for w in ws]
                  + [pl.BlockSpec((tm, d), lambda i: (i, 0)), _full((1, d))]),
        out_specs=pl.BlockSpec((tm, d), lambda i: (i, 0)),
        out_shape=jax.ShapeDtypeStruct((t, d), F32),
        compiler_params=_params("parallel"),
        name=name,
    )(*ins, *ws, x, g.reshape(1, d))


def _mix_cross_kernel(conv_ref, attn_ref, x_ref, wc_ref, wm_ref, gmix_ref,
                      gpre_ref, wq_ref, mk_ref, mv_ref, wo_ref, gpost_ref, o_ref):
    y = _dot(conv_ref[...].astype(BF16), wc_ref[...]) + _dot(attn_ref[...], wm_ref[...])
    x = x_ref[...] + _rms(y, gmix_ref[...])
    hn = _rms(x, gpre_ref[...]).astype(BF16)
    q = (_dot(hn, wq_ref[...]) * MEM_SCALE).astype(BF16)
    outs = []
    for h in range(MEM_HEADS):
        sl = slice(h * MEM_HEAD_DIM, (h + 1) * MEM_HEAD_DIM)
        s = _dot_nt(q[:, sl], mk_ref[:, sl].astype(BF16))
        m = jnp.max(s, axis=-1, keepdims=True)
        p = jnp.exp(s - m)
        l = jnp.sum(p, axis=-1, keepdims=True)
        outs.append((_dot(p.astype(BF16), mv_ref[:, sl].astype(BF16)) / l).astype(BF16))
    y2 = _dot(jnp.concatenate(outs, axis=-1), wo_ref[...])
    o_ref[...] = x + _rms(y2, gpost_ref[...])


def mix_cross_prompt(conv_out, attn, x, wc, wm, g_mix, g_pre, wq, mem_kv, wo, g_post, *, seq, tm):
    t, d = x.shape
    per_seq = seq // tm
    const = lambda shape: pl.BlockSpec(shape, lambda i: (0,) * len(shape), pipeline_mode=pl.Buffered(1))
    rows = lambda width: pl.BlockSpec((tm, width), lambda i: (i, 0))
    return pl.pallas_call(
        _mix_cross_kernel,
        grid=(t // tm,),
        in_specs=[
            rows(CONV_CH), rows(MLA_WIDTH), rows(d),
            const(wc.shape), const(wm.shape), const((1, d)),
            const((1, d)), const(wq.shape),
            pl.BlockSpec((N_MEM, MEM_WIDTH), lambda i: (i // per_seq, 0)),
            pl.BlockSpec((N_MEM, MEM_WIDTH), lambda i: (i // per_seq, 1)),
            const(wo.shape), const((1, d)),
        ],
        out_specs=rows(d),
        out_shape=jax.ShapeDtypeStruct((t, d), F32),
        compiler_params=_params("parallel"),
        name="mix_cross_prompt",
    )(conv_out, attn, x, wc, wm, g_mix.reshape(1, d), g_pre.reshape(1, d), wq, mem_kv, mem_kv, wo,
      g_post.reshape(1, d))


def _cross_sample_kernel(q_ref, mk_ref, mv_ref, o_ref):
    bt = q_ref.shape[0]
    q = (q_ref[...] * MEM_SCALE).astype(BF16)
    for h in range(MEM_HEADS):
        sl = slice(h * MEM_HEAD_DIM, (h + 1) * MEM_HEAD_DIM)
        rows = pl.ds(h, N_MEM, stride=MEM_HEADS)
        qh = jnp.broadcast_to(q[:, :, sl], (bt, 8, MEM_HEAD_DIM))
        kh = mk_ref[:, 0, rows, :].astype(BF16)
        vh = mv_ref[:, 0, rows, :].astype(BF16)
        s = jnp.einsum("bqe,bme->bqm", qh, kh, preferred_element_type=F32)
        m = jnp.max(s, axis=-1, keepdims=True)
        p = jnp.exp(s - m)
        l = jnp.sum(p, axis=-1, keepdims=True)
        o = jnp.einsum("bqm,bme->bqe", p.astype(BF16), vh, preferred_element_type=F32) / l
        o_ref[:, :, sl] = o[:, 0:1, :].astype(o_ref.dtype)


def cross_sample_core(q3, mem_k, mem_v, *, layer, bt):
    bsz = q3.shape[0]
    mem = lambda: pl.BlockSpec((bt, 1, N_MEM * MEM_HEADS, MEM_HEAD_DIM), lambda i: (i, layer, 0, 0))
    return pl.pallas_call(
        _cross_sample_kernel,
        grid=(bsz // bt,),
        in_specs=[pl.BlockSpec((bt, 1, MEM_WIDTH), lambda i: (i, 0, 0)), mem(), mem()],
        out_specs=pl.BlockSpec((bt, 1, MEM_WIDTH), lambda i: (i, 0, 0)),
        out_shape=jax.ShapeDtypeStruct((bsz, 1, MEM_WIDTH), BF16),
        compiler_params=_params("parallel"),
        name="cross_sample",
    )(q3, mem_k, mem_v)


def _rope_tables(pos):
    half = QK_ROPE // 2
    inv_freq = ROPE_BASE ** (-jnp.arange(half, dtype=F32) / half)
    ang = pos.astype(F32)[:, None] * inv_freq[None, :]
    cos, sin = jnp.cos(ang), jnp.sin(ang)
    z = jnp.zeros_like(cos)
    return (jnp.concatenate([cos, cos, z, z], axis=-1),
            jnp.concatenate([-sin, z, z, z], axis=-1),
            jnp.concatenate([z, sin, z, z], axis=-1))


def _prep_layer(l, w_uq, w_uk, w_uv, w_out, w_xq, w_mk, w_mv, w_xo):
    d = w_out.shape[2]
    uq = w_uq[l]
    wqn = uq[:, :, :QK_NOPE].reshape(Q_LORA, -1).astype(BF16)
    wqp = jnp.pad(uq[:, :, QK_NOPE:], ((0, 0), (0, 0), (0, LANE - QK_ROPE))).reshape(Q_LORA, -1).astype(BF16)
    return dict(
        wqn=wqn, wqp=wqp,
        wuk=w_uk[l].reshape(KV_LORA, -1).astype(BF16),
        wuv=w_uv[l].reshape(KV_LORA, -1).astype(BF16),
        wukt=w_uk[l].transpose(1, 2, 0).astype(BF16),
        wuv3=w_uv[l].transpose(1, 0, 2).astype(BF16),
        w_out_c=w_out[l][:CONV_CH].astype(BF16),
        w_out_m=w_out[l][CONV_CH:].astype(BF16),
        w_xq=w_xq[l].reshape(d, -1).astype(BF16),
        w_mkv=jnp.concatenate([w_mk[l].reshape(d, -1), w_mv[l].reshape(d, -1)], axis=1).astype(BF16),
        w_xo=w_xo[l].reshape(-1, d).astype(BF16),
    )


def kernel(x_prompt, x_sample, mem_prompt, cache_kv_latent, cache_k_rope, state_conv, cache_mem_k, cache_mem_v, page_table, norm_mix_pre, w_in, conv_w, conv_b, conv_ln_g, conv_ln_b, q_norm_g, w_uq, kv_norm_g, w_uk, w_uv, w_out, norm_mix_post, norm_x_pre, norm_mem, w_xq, w_mk, w_mv, w_xo, norm_x_post):
    b_p, s_p, d = x_prompt.shape
    b_s = x_sample.shape[0]
    depth = w_in.shape[0]
    n_pages = page_table.shape[1]
    t_p = b_p * s_p
    tm_p = min(512, s_p)

    tabs_p = _rope_tables(jnp.arange(s_p, dtype=jnp.int32))
    tabs_s = _rope_tables(jnp.full((b_s,), n_pages * PAGE_SIZE, jnp.int32))
    mem2 = mem_prompt.reshape(b_p * N_MEM, d)
    mem_k_s = cache_mem_k.reshape(b_s, depth, N_MEM * MEM_HEADS, MEM_HEAD_DIM)
    mem_v_s = cache_mem_v.reshape(b_s, depth, N_MEM * MEM_HEADS, MEM_HEAD_DIM)
    cache_krt = cache_k_rope.transpose(0, 1, 3, 2)
    state_t = state_conv.transpose(0, 2, 1, 3)
    conv_w_t = conv_w.transpose(1, 0, 2)
    new_state_t, conv_part = conv_state_shift(state_t, conv_w_t, bt=8)
    w_in_t = w_in_prep(w_in.transpose(0, 2, 1))

    xp = x_prompt.reshape(t_p, d)
    xs = x_sample.reshape(b_s, d)
    lat_p, kpe_p, conv_p, mk_p, mv_p, lat_s, kpe_s = [], [], [], [], [], [], []
    u_s = []
    for l in range(depth):
        w = _prep_layer(l, w_uq, w_uk, w_uv, w_out, w_xq, w_mk, w_mv, w_xo)

        proj = rms_matmul(xp, norm_mix_pre[l], w_in_t, tm=min(1024, s_p), tn=1024, name="in_proj_p",
                          w_transposed=True, layer=l)
        conv_out, conv_state = conv_prompt(proj, conv_w[l], conv_b[l], conv_ln_g[l], conv_ln_b[l],
                                           bsz=b_p, seq=s_p)
        q, k, v, c_kv, k_pe = mla_prompt(proj, tabs_p, q_norm_g[l], kv_norm_g[l],
                                         w["wqn"], w["wqp"], w["wuk"], w["wuv"], tm=tm_p)
        attn = flash_prompt(q, k, v, proj, bsz=b_p, seq=s_p)
        mem_kv = rms_matmul(mem2, norm_mem[l], w["w_mkv"], tm=N_MEM, tn=2 * MEM_WIDTH, name="mem_kv")
        xp = mix_cross_prompt(conv_out, attn, xp, w["w_out_c"], w["w_out_m"], norm_mix_post[l],
                              norm_x_pre[l], w["w_xq"], mem_kv, w["w_xo"], norm_x_post[l], seq=s_p, tm=tm_p)
        lat_p.append(c_kv.reshape(b_p, s_p, KV_LORA))
        kpe_p.append(k_pe.reshape(b_p, s_p, QK_ROPE))
        conv_p.append(conv_state)
        mk_p.append(mem_kv[:, :MEM_WIDTH].reshape(b_p, N_MEM, MEM_HEADS, MEM_HEAD_DIM))
        mv_p.append(mem_kv[:, MEM_WIDTH:].reshape(b_p, N_MEM, MEM_HEADS, MEM_HEAD_DIM))

        proj_s = rms_matmul(xs, norm_mix_pre[l], w_in_t, tm=b_s, tn=1024, name="in_proj_s",
                            w_transposed=True, layer=l)
        conv_out_s, u_l = conv_sample(proj_s, conv_part, l, conv_w[l], conv_b[l], conv_ln_g[l], conv_ln_b[l])
        u_s.append(u_l)
        qa, qp, c_new, k_new = mla_sample(proj_s, tabs_s, q_norm_g[l], kv_norm_g[l],
                                          w["wqn"], w["wqp"], w["wukt"])
        o_lat = paged_attention(page_table, qa.transpose(1, 0, 2), qp.transpose(1, 0, 2),
                                c_new.reshape(b_s, 1, KV_LORA), k_new.reshape(b_s, 1, QK_ROPE),
                                cache_kv_latent, cache_krt, layer=l)
        attn_s = uv_project(o_lat.transpose(1, 0, 2), w["wuv3"], proj_s)
        xs = proj_norm_res([conv_out_s, attn_s], [w["w_out_c"], w["w_out_m"]],
                           xs, norm_mix_post[l], tm=b_s, name="out_proj_s")
        q_x = rms_matmul(xs, norm_x_pre[l], w["w_xq"], tm=b_s, tn=MEM_WIDTH, name="cross_q_s")
        o_x = cross_sample_core(q_x.reshape(b_s, 1, MEM_WIDTH), mem_k_s, mem_v_s, layer=l, bt=8)
        xs = proj_norm_res([o_x.reshape(b_s, MEM_WIDTH)], [w["w_xo"]], xs, norm_x_post[l],
                           tm=b_s, name="cross_out_s")
        lat_s.append(c_new.reshape(b_s, 1, KV_LORA))
        kpe_s.append(k_new.reshape(b_s, 1, QK_ROPE))

    return (xp.reshape(b_p, s_p, d), xs.reshape(b_s, 1, d),
            jnp.stack(lat_p, axis=1), jnp.stack(kpe_p, axis=1), jnp.stack(conv_p, axis=1),
            jnp.stack(mk_p, axis=1), jnp.stack(mv_p, axis=1),
            jnp.stack(lat_s, axis=1), jnp.stack(kpe_s, axis=1),
            conv_state_finish(new_state_t, jnp.stack(u_s, axis=1)[:, None]).transpose(0, 2, 1, 3))
```

```python
import functools

import jax
import jax.numpy as jnp
from jax import lax
from jax.experimental import pallas as pl
from jax.experimental.pallas import tpu as pltpu

F32 = jnp.float32
BF16 = jnp.bfloat16

D_MODEL = 2048
CONV_CH = 1024
CONV_WIDTH = 31
CONV_STATE = CONV_WIDTH - 1
MLA_HEADS = 8
QK_NOPE = 128
QK_ROPE = 64
V_DIM = 128
MLA_WIDTH = MLA_HEADS * V_DIM
Q_LORA = 512
KV_LORA = 256
MLA_SCALE = (QK_NOPE + QK_ROPE) ** -0.5
ROPE_BASE = 10000.0
PAGE_SIZE = 128
N_MEM = 256
MEM_HEADS = 4
MEM_HEAD_DIM = 128
MEM_WIDTH = MEM_HEADS * MEM_HEAD_DIM
MEM_SCALE = MEM_HEAD_DIM ** -0.5
EPS = 1e-6
LOG2E = 1.4426950408889634

LANE = 128
QK_PAD = 256
NEG_BIG = -1e30

PROJ_COLS = 5120
COL_A, COL_B, COL_GC, COL_GM, COL_QL, COL_KV, COL_KPE = 0, 1024, 2048, 3072, 4096, 4608, 4864

VMEM_LIMIT = 52 * 1024 * 1024


def _params(*sem):
    return pltpu.CompilerParams(dimension_semantics=sem, vmem_limit_bytes=VMEM_LIMIT)


def _rms(x, g):
    return x * lax.rsqrt(jnp.mean(x * x, axis=-1, keepdims=True) + EPS) * g


def _silu(x):
    return x * jax.nn.sigmoid(x)


def _dot(a, b):
    return jnp.dot(a, b, preferred_element_type=F32)


def _dot_nt(a, b):
    return lax.dot_general(a, b, (((1,), (1,)), ((), ())), preferred_element_type=F32)


def _rope128(x, cos_t, sin_n, sin_p):
    return x * cos_t + pltpu.roll(x, 96, 1) * sin_n + pltpu.roll(x, 32, 1) * sin_p


def _rms_matmul_kernel(x_ref, g_ref, w_ref, o_ref, hn_ref, *, w_transposed):
    @pl.when(pl.program_id(1) == 0)
    def _():
        hn_ref[...] = _rms(x_ref[...], g_ref[...]).astype(BF16)

    dot = _dot_nt if w_transposed else _dot
    o_ref[...] = dot(hn_ref[...], w_ref[...]).astype(o_ref.dtype)


def rms_matmul(x, g, w, *, tm, tn, name, w_transposed=False, layer=None):
    t, k = x.shape
    n = w.shape[-2] if w_transposed else w.shape[-1]
    lead = () if layer is None else (None,)
    pick = () if layer is None else (layer,)
    w_spec = (pl.BlockSpec(lead + (tn, k), lambda i, j: pick + (j, 0)) if w_transposed
              else pl.BlockSpec(lead + (k, tn), lambda i, j: pick + (0, j)))
    return pl.pallas_call(
        functools.partial(_rms_matmul_kernel, w_transposed=w_transposed),
        grid=(t // tm, n // tn),
        in_specs=[
            pl.BlockSpec((tm, k), lambda i, j: (i, 0)),
            pl.BlockSpec((1, k), lambda i, j: (0, 0)),
            w_spec,
        ],
        out_specs=pl.BlockSpec((tm, tn), lambda i, j: (i, j)),
        out_shape=jax.ShapeDtypeStruct((t, n), F32),
        scratch_shapes=[pltpu.VMEM((tm, k), BF16)],
        compiler_params=_params("parallel", "arbitrary"),
        name=name,
    )(x, g.reshape(1, k), w)


_W_IN_GROUPS = (
    (COL_A, 0, 3 * CONV_CH),
    (COL_GM, 3 * CONV_CH + Q_LORA + KV_LORA + QK_ROPE, MLA_WIDTH),
    (COL_QL, 3 * CONV_CH, Q_LORA),
    (COL_KV, 3 * CONV_CH + Q_LORA, KV_LORA),
    (COL_KPE, 3 * CONV_CH + Q_LORA + KV_LORA, QK_ROPE),
)
W_PREP_TK = 512


def _w_in_prep_kernel(w_ref, o_ref):
    for dst, src, rows in _W_IN_GROUPS:
        o_ref[0, dst:dst + rows, :] = w_ref[0, src:src + rows, :].astype(BF16)
    pad0 = COL_KPE + QK_ROPE
    o_ref[0, pad0:, :] = jnp.zeros((PROJ_COLS - pad0, o_ref.shape[2]), BF16)


def w_in_prep(w_in_t):
    depth, n_in, k = w_in_t.shape
    return pl.pallas_call(
        _w_in_prep_kernel,
        grid=(depth, k // W_PREP_TK),
        in_specs=[pl.BlockSpec((1, n_in, W_PREP_TK), lambda l, j: (l, 0, j))],
        out_specs=pl.BlockSpec((1, PROJ_COLS, W_PREP_TK), lambda l, j: (l, 0, j)),
        out_shape=jax.ShapeDtypeStruct((depth, PROJ_COLS, k), BF16),
        compiler_params=_params("parallel", "parallel"),
        name="w_in_prep",
    )(w_in_t)


CONV_SEGS = 8
CONV_SEG_ROWS = 128
CONV_HALO = 32
CONV_BLOCK = 16
CONV_TAPS_A = 16
CONV_UNROLL = 16
CONV_PAD = 32
CONV_TILE = CONV_SEGS * CONV_SEG_ROWS


def _conv_prompt_kernel(proj_hbm, w_ref, cb_ref, lg_ref, lb_ref, out_hbm, state_ref,
                        xa, xb, xg, y_ref, o_ref, wb_ref, lgb_ref, in_sems, out_sem, *, tiles_per_seq):
    n = pl.program_id(0)
    n_tiles = pl.num_programs(0)
    slot = n % 2
    lrows, halo = CONV_SEG_ROWS, CONV_HALO

    def in_copies(tile, sl, seg, first):
        row = tile * CONV_TILE + seg * lrows
        if first:
            src_rows, dst_rows = pl.ds(row, lrows), pl.ds(halo, lrows)
        else:
            src_rows, dst_rows = pl.ds(row - halo, lrows + halo), pl.ds(0, lrows + halo)
        return (
            pltpu.make_async_copy(proj_hbm.at[src_rows, pl.ds(COL_A, CONV_CH)],
                                  xa.at[sl, dst_rows, seg, :], in_sems.at[sl, 0]),
            pltpu.make_async_copy(proj_hbm.at[src_rows, pl.ds(COL_B, CONV_CH)],
                                  xb.at[sl, dst_rows, seg, :], in_sems.at[sl, 1]),
            pltpu.make_async_copy(proj_hbm.at[pl.ds(row, lrows), pl.ds(COL_GC, CONV_CH)],
                                  xg.at[sl, :, seg, :], in_sems.at[sl, 2]),
        )

    def for_each_in_copy(tile, sl, fn):
        seq_start = tile % tiles_per_seq == 0

        @pl.when(seq_start)
        def _():
            for c in in_copies(tile, sl, 0, True):
                fn(c)

        @pl.when(jnp.logical_not(seq_start))
        def _():
            for c in in_copies(tile, sl, 0, False):
                fn(c)

        for seg in range(1, CONV_SEGS):
            for c in in_copies(tile, sl, seg, False):
                fn(c)

    def out_copies(tile):
        return [pltpu.make_async_copy(o_ref.at[:, seg, :],
                                      out_hbm.at[pl.ds(tile * CONV_TILE + seg * lrows, lrows), :], out_sem.at[0])
                for seg in range(CONV_SEGS)]

    @pl.when(n == 0)
    def _():
        for k in range(CONV_WIDTH):
            wb_ref[k] = jnp.broadcast_to(w_ref[k:k + 1, :], (CONV_SEGS, CONV_CH))
        lgb_ref[0] = jnp.broadcast_to(lg_ref[...], (CONV_SEGS, CONV_CH))
        lgb_ref[1] = jnp.broadcast_to(lb_ref[...], (CONV_SEGS, CONV_CH))
        for_each_in_copy(n, slot, lambda c: c.start())

    @pl.when(n + 1 < n_tiles)
    def _():
        for_each_in_copy(n + 1, 1 - slot, lambda c: c.start())

    for_each_in_copy(n, slot, lambda c: c.wait())

    def glu(i, carry):
        xa[slot, i] = xa[slot, i] * jax.nn.sigmoid(xb[slot, i])
        return carry

    lax.fori_loop(0, lrows + halo, glu, 0, unroll=4)

    @pl.when(n % tiles_per_seq == 0)
    def _():
        xa[slot, 0:halo, 0:1, :] = jnp.zeros((halo, 1, CONV_CH), F32)

    @pl.when(n % tiles_per_seq == tiles_per_seq - 1)
    def _():
        state_ref[0] = xa[slot, halo + lrows - CONV_STATE:halo + lrows, CONV_SEGS - 1, :]

    n_zero = CONV_PAD - (CONV_TAPS_A - 1)
    y_ref[0:n_zero] = jnp.zeros((n_zero, CONV_SEGS, CONV_CH), F32)
    shift = halo - CONV_STATE
    zero = jnp.zeros((CONV_SEGS, LANE), F32)
    for k0, n_taps in ((0, CONV_TAPS_A), (CONV_TAPS_A, CONV_WIDTH - CONV_TAPS_A)):
        first = k0 == 0
        n_iter = -(-(lrows + n_taps - 1) // CONV_UNROLL)
        j_start = min(shift + k0, lrows + halo - n_iter * CONV_UNROLL)
        for c0 in range(0, CONV_CH, LANE):
            cols = slice(c0, c0 + LANE)
            w = [wb_ref[k0 + t, :, cols] for t in range(n_taps)]
            fresh = (cb_ref[:, cols] + zero) if first else zero
            row_off = CONV_PAD - shift - k0 - (n_taps - 1)

            def body(it, carry, w=w, fresh=fresh, cols=cols, row_off=row_off, first=first, j_start=j_start):
                acc = list(carry)
                j0 = it * CONV_UNROLL + j_start
                for jj in range(CONV_UNROLL):
                    u = xa[slot, j0 + jj, :, cols]
                    acc = [a + wt * u for wt, a in zip(w, [fresh] + acc)]
                    done = acc.pop()
                    if first:
                        y_ref[j0 + (jj + row_off), :, cols] = done
                    else:
                        y_ref[j0 + (jj + row_off), :, cols] = y_ref[j0 + (jj + row_off), :, cols] + done
                return tuple(acc)

            lax.fori_loop(0, n_iter, body, tuple(zero for _ in range(n_taps - 1)))

    @pl.when(n > 0)
    def _():
        for c in out_copies(n - 1):
            c.wait()

    def norm_gate(r, carry):
        for ii in range(CONV_BLOCK):
            i = r * CONV_BLOCK + ii
            acc = y_ref[i + CONV_PAD]
            mu = jnp.mean(acc, axis=-1, keepdims=True)
            d = acc - mu
            var = jnp.mean(d * d, axis=-1, keepdims=True)
            z = d * lax.rsqrt(var + EPS) * lgb_ref[0] + lgb_ref[1]
            o_ref[i] = _silu(z) * _silu(xg[slot, i])
        return carry

    lax.fori_loop(0, lrows // CONV_BLOCK, norm_gate, 0)

    for c in out_copies(n):
        c.start()

    @pl.when(n == n_tiles - 1)
    def _():
        for c in out_copies(n):
            c.wait()


def conv_prompt(proj, conv_w, conv_b, ln_g, ln_b, *, bsz, seq):
    tiles_per_seq = seq // CONV_TILE
    vec = lambda: pl.BlockSpec((1, CONV_CH), lambda n: (0, 0))
    seg_buf = lambda rows: pltpu.VMEM((2, rows, CONV_SEGS, CONV_CH), F32)
    return pl.pallas_call(
        functools.partial(_conv_prompt_kernel, tiles_per_seq=tiles_per_seq),
        grid=(bsz * tiles_per_seq,),
        in_specs=[
            pl.BlockSpec(memory_space=pl.ANY),
            pl.BlockSpec((CONV_WIDTH, CONV_CH), lambda n: (0, 0)),
            vec(), vec(), vec(),
        ],
        out_specs=[
            pl.BlockSpec(memory_space=pl.ANY),
            pl.BlockSpec((1, CONV_STATE, CONV_CH), lambda n: (n // tiles_per_seq, 0, 0)),
        ],
        out_shape=[
            jax.ShapeDtypeStruct((bsz * seq, CONV_CH), F32),
            jax.ShapeDtypeStruct((bsz, CONV_STATE, CONV_CH), F32),
        ],
        scratch_shapes=[
            seg_buf(CONV_SEG_ROWS + CONV_HALO), seg_buf(CONV_SEG_ROWS + CONV_HALO), seg_buf(CONV_SEG_ROWS),
            pltpu.VMEM((CONV_SEG_ROWS + 2 * CONV_PAD, CONV_SEGS, CONV_CH), F32),
            pltpu.VMEM((CONV_SEG_ROWS, CONV_SEGS, CONV_CH), F32),
            pltpu.VMEM((CONV_WIDTH, CONV_SEGS, CONV_CH), F32),
            pltpu.VMEM((2, CONV_SEGS, CONV_CH), F32),
            pltpu.SemaphoreType.DMA((2, 3)), pltpu.SemaphoreType.DMA((1,)),
        ],
        compiler_params=_params("arbitrary"),
        name="conv_prompt",
    )(proj, conv_w, conv_b.reshape(1, -1), ln_g.reshape(1, -1), ln_b.reshape(1, -1))


def _conv_state_kernel(st_ref, w_ref, newst_ref, part_ref):
    st = st_ref[...]
    newst_ref[:, 0:CONV_STATE - 1] = st[:, 1:CONV_STATE]
    newst_ref[:, CONV_STATE - 1:CONV_STATE] = jnp.zeros_like(st[:, 0:1])
    part_ref[...] = jnp.sum(st * w_ref[0:CONV_STATE][None], axis=1)


def conv_state_shift(state_t, conv_w_t, *, bt):
    bsz, _, depth, _ = state_t.shape
    blk = pl.BlockSpec((bt, CONV_STATE, depth, CONV_CH), lambda i: (i, 0, 0, 0))
    return pl.pallas_call(
        _conv_state_kernel,
        grid=(bsz // bt,),
        in_specs=[blk, pl.BlockSpec((CONV_WIDTH, depth, CONV_CH), lambda i: (0, 0, 0))],
        out_specs=[blk, pl.BlockSpec((bt, depth, CONV_CH), lambda i: (i, 0, 0))],
        out_shape=[jax.ShapeDtypeStruct(state_t.shape, F32),
                   jax.ShapeDtypeStruct((bsz, depth, CONV_CH), F32)],
        compiler_params=_params("parallel"),
        name="conv_state_shift",
    )(state_t, conv_w_t)


def _conv_state_finish_kernel(u_ref, st_hbm, o_ref):
    del st_hbm
    o_ref[...] = u_ref[...]


def conv_state_finish(new_state, u_all):
    bsz, _, depth, _ = new_state.shape
    return pl.pallas_call(
        _conv_state_finish_kernel,
        grid=(1,),
        in_specs=[pl.BlockSpec((bsz, 1, depth, CONV_CH), lambda i: (0, 0, 0, 0)),
                  pl.BlockSpec(memory_space=pl.ANY)],
        out_specs=pl.BlockSpec((bsz, 1, depth, CONV_CH), lambda i: (0, CONV_STATE - 1, 0, 0)),
        out_shape=jax.ShapeDtypeStruct(new_state.shape, F32),
        input_output_aliases={1: 0},
        compiler_params=_params("arbitrary"),
        name="conv_state_finish",
    )(u_all, new_state)


def _conv_sample_kernel(a_ref, b_ref, gc_ref, part_ref, w_ref, cb_ref, lg_ref, lb_ref, out_ref, u_ref, *, layer):
    u = a_ref[...] * jax.nn.sigmoid(b_ref[...])
    u_ref[...] = u
    y = part_ref[:, layer, :] + u * w_ref[CONV_STATE:CONV_WIDTH, :] + cb_ref[...]
    mu = jnp.mean(y, axis=-1, keepdims=True)
    d = y - mu
    var = jnp.mean(d * d, axis=-1, keepdims=True)
    z = d * lax.rsqrt(var + EPS) * lg_ref[...] + lb_ref[...]
    out_ref[...] = (_silu(z) * _silu(gc_ref[...])).astype(out_ref.dtype)


def conv_sample(proj, part, layer, conv_w, conv_b, ln_g, ln_b):
    bsz = proj.shape[0]
    vec = lambda: pl.BlockSpec((1, CONV_CH), lambda i: (0, 0))
    row = lambda: pl.BlockSpec((bsz, CONV_CH), lambda i: (0, 0))
    return pl.pallas_call(
        functools.partial(_conv_sample_kernel, layer=layer),
        grid=(1,),
        in_specs=[
            pl.BlockSpec((bsz, CONV_CH), lambda i: (0, COL_A // CONV_CH)),
            pl.BlockSpec((bsz, CONV_CH), lambda i: (0, COL_B // CONV_CH)),
            pl.BlockSpec((bsz, CONV_CH), lambda i: (0, COL_GC // CONV_CH)),
            pl.BlockSpec(part.shape, lambda i: (0, 0, 0)),
            pl.BlockSpec((CONV_WIDTH, CONV_CH), lambda i: (0, 0)),
            vec(), vec(), vec(),
        ],
        out_specs=[row(), row()],
        out_shape=[
            jax.ShapeDtypeStruct((bsz, CONV_CH), BF16),
            jax.ShapeDtypeStruct((bsz, CONV_CH), F32),
        ],
        compiler_params=_params("arbitrary"),
        name="conv_sample",
    )(proj, proj, proj, part, conv_w, conv_b.reshape(1, -1), ln_g.reshape(1, -1), ln_b.reshape(1, -1))


def _mla_common(ql_ref, kvl_ref, kpe_ref, cos_ref, sn_ref, sp_ref, qg_ref, kvg_ref, wqn_ref, wqp_ref):
    qn = _rms(ql_ref[...], qg_ref[...]).astype(BF16)
    q_nope = _dot(qn, wqn_ref[...]) * (MLA_SCALE * LOG2E)
    q_rope_raw = _dot(qn, wqp_ref[...]) * (MLA_SCALE * LOG2E)
    cos_t, sin_n, sin_p = cos_ref[...], sn_ref[...], sp_ref[...]
    q_rope = [_rope128(q_rope_raw[:, h * LANE:(h + 1) * LANE], cos_t, sin_n, sin_p)
              for h in range(MLA_HEADS)]
    c_kv = _rms(kvl_ref[...], kvg_ref[...])
    k_pe = _rope128(kpe_ref[...], cos_t, sin_n, sin_p)
    return q_nope, q_rope, c_kv, k_pe


def _mla_prompt_kernel(ql_ref, kvl_ref, kpe_ref, cos_ref, sn_ref, sp_ref, qg_ref, kvg_ref,
                       wqn_ref, wqp_ref, wuk_ref, wuv_ref,
                       q_ref, k_ref, v_ref, ckv_ref, kpeo_ref):
    q_nope, q_rope, c_kv, k_pe = _mla_common(ql_ref, kvl_ref, kpe_ref, cos_ref, sn_ref, sp_ref,
                                             qg_ref, kvg_ref, wqn_ref, wqp_ref)
    ckv_ref[...] = c_kv
    kpeo_ref[...] = k_pe[:, :QK_ROPE]
    c_bf = c_kv.astype(BF16)
    k_nope = _dot(c_bf, wuk_ref[...])
    v = _dot(c_bf, wuv_ref[...])
    for h in range(MLA_HEADS):
        v_ref[h * V_DIM:(h + 1) * V_DIM, :] = v[:, h * V_DIM:(h + 1) * V_DIM].T.astype(BF16)
    k_pe_bf = k_pe.astype(BF16)
    for h in range(MLA_HEADS):
        lo = h * QK_PAD
        q_ref[:, lo:lo + LANE] = q_nope[:, h * LANE:(h + 1) * LANE].astype(BF16)
        q_ref[:, lo + LANE:lo + QK_PAD] = q_rope[h].astype(BF16)
        k_ref[:, lo:lo + LANE] = k_nope[:, h * LANE:(h + 1) * LANE].astype(BF16)
        k_ref[:, lo + LANE:lo + QK_PAD] = k_pe_bf


def _proj_specs(tm, n_tab):
    return [
        pl.BlockSpec((tm, Q_LORA), lambda i: (i, COL_QL // Q_LORA)),
        pl.BlockSpec((tm, KV_LORA), lambda i: (i, COL_KV // KV_LORA)),
        pl.BlockSpec((tm, LANE), lambda i: (i, COL_KPE // LANE)),
        pl.BlockSpec((tm, LANE), lambda i: (i % n_tab, 0)),
        pl.BlockSpec((tm, LANE), lambda i: (i % n_tab, 0)),
        pl.BlockSpec((tm, LANE), lambda i: (i % n_tab, 0)),
    ]


def _full(shape):
    return pl.BlockSpec(shape, lambda i: (0,) * len(shape))


def mla_prompt(proj, tabs, q_g, kv_g, wqn, wqp, wuk, wuv, *, tm):
    t = proj.shape[0]
    n_tab = tabs[0].shape[0] // tm
    h = MLA_HEADS
    return pl.pallas_call(
        _mla_prompt_kernel,
        grid=(t // tm,),
        in_specs=_proj_specs(tm, n_tab) + [
            _full((1, Q_LORA)), _full((1, KV_LORA)),
            _full((Q_LORA, h * LANE)), _full((Q_LORA, h * LANE)),
            _full((KV_LORA, h * LANE)), _full((KV_LORA, h * LANE)),
        ],
        out_specs=[
            pl.BlockSpec((tm, h * QK_PAD), lambda i: (i, 0)),
            pl.BlockSpec((tm, h * QK_PAD), lambda i: (i, 0)),
            pl.BlockSpec((h * V_DIM, tm), lambda i: (0, i)),
            pl.BlockSpec((tm, KV_LORA), lambda i: (i, 0)),
            pl.BlockSpec((tm, QK_ROPE), lambda i: (i, 0)),
        ],
        out_shape=[
            jax.ShapeDtypeStruct((t, h * QK_PAD), BF16),
            jax.ShapeDtypeStruct((t, h * QK_PAD), BF16),
            jax.ShapeDtypeStruct((h * V_DIM, t), BF16),
            jax.ShapeDtypeStruct((t, KV_LORA), F32),
            jax.ShapeDtypeStruct((t, QK_ROPE), F32),
        ],
        compiler_params=_params("parallel"),
        name="mla_prompt",
    )(proj, proj, proj, *tabs, q_g.reshape(1, -1), kv_g.reshape(1, -1), wqn, wqp, wuk, wuv)


def _mla_sample_kernel(ql_ref, kvl_ref, kpe_ref, cos_ref, sn_ref, sp_ref, qg_ref, kvg_ref,
                       wqn_ref, wqp_ref, wukt_ref,
                       qa_ref, qp_ref, ckv_ref, kpeo_ref):
    q_nope, q_rope, c_kv, k_pe = _mla_common(ql_ref, kvl_ref, kpe_ref, cos_ref, sn_ref, sp_ref,
                                             qg_ref, kvg_ref, wqn_ref, wqp_ref)
    ckv_ref[...] = c_kv
    kpeo_ref[...] = k_pe[:, :QK_ROPE]
    for h in range(MLA_HEADS):
        qa_ref[h] = _dot(q_nope[:, h * LANE:(h + 1) * LANE].astype(BF16), wukt_ref[h])
        qp_ref[h] = q_rope[h]


def mla_sample(proj, tabs, q_g, kv_g, wqn, wqp, wukt):
    t = proj.shape[0]
    h = MLA_HEADS
    return pl.pallas_call(
        _mla_sample_kernel,
        grid=(1,),
        in_specs=_proj_specs(t, 1) + [
            _full((1, Q_LORA)), _full((1, KV_LORA)),
            _full((Q_LORA, h * LANE)), _full((Q_LORA, h * LANE)),
            _full((h, QK_NOPE, KV_LORA)),
        ],
        out_specs=[
            _full((h, t, KV_LORA)), _full((h, t, LANE)),
            _full((t, KV_LORA)), _full((t, QK_ROPE)),
        ],
        out_shape=[
            jax.ShapeDtypeStruct((h, t, KV_LORA), F32),
            jax.ShapeDtypeStruct((h, t, LANE), F32),
            jax.ShapeDtypeStruct((t, KV_LORA), F32),
            jax.ShapeDtypeStruct((t, QK_ROPE), F32),
        ],
        compiler_params=_params("arbitrary"),
        name="mla_sample",
    )(proj, proj, proj, *tabs, q_g.reshape(1, -1), kv_g.reshape(1, -1), wqn, wqp, wukt)


FLASH_TQ = 512


def _flash_kernel(q_ref, k_ref, vt_ref, g_ref, o_ref, *, seq):
    tq = FLASH_TQ
    causal_t = (lax.broadcasted_iota(jnp.int32, (tq, tq), 0)
                <= lax.broadcasted_iota(jnp.int32, (tq, tq), 1))
    n_q = seq // tq

    def scores(i):
        kv = i * tq
        return _dot_nt(k_ref[0:kv + tq, :], q_ref[kv:kv + tq, :])

    s_next = scores(0)
    for i in range(n_q):
        kv = i * tq
        s_t = s_next
        if i + 1 < n_q:
            s_next = scores(i + 1)
        s_d = jnp.where(causal_t, s_t[kv:kv + tq], NEG_BIG)
        m = jnp.max(s_d, axis=0, keepdims=True)
        if i > 0:
            s_o = s_t[0:kv]
            m = jnp.maximum(m, jnp.max(s_o, axis=0, keepdims=True))
            p_o = jnp.exp2(s_o - m)
            l = jnp.sum(p_o, axis=0, keepdims=True)
            acc = _dot(vt_ref[:, 0:kv], p_o.astype(BF16))
        else:
            l = jnp.zeros((1, tq), F32)
            acc = jnp.zeros((V_DIM, tq), F32)
        p_d = jnp.exp2(s_d - m)
        l = l + jnp.sum(p_d, axis=0, keepdims=True)
        acc = acc + _dot(vt_ref[:, kv:kv + tq], p_d.astype(BF16))
        o = (acc / l).T
        o_ref[kv:kv + tq, :] = (o * _silu(g_ref[kv:kv + tq, :])).astype(o_ref.dtype)


def flash_prompt(q, k, v_t, proj, *, bsz, seq):
    h = MLA_HEADS
    return pl.pallas_call(
        functools.partial(_flash_kernel, seq=seq),
        grid=(bsz, h),
        in_specs=[
            pl.BlockSpec((seq, QK_PAD), lambda b, hh: (b, hh)),
            pl.BlockSpec((seq, QK_PAD), lambda b, hh: (b, hh)),
            pl.BlockSpec((V_DIM, seq), lambda b, hh: (hh, b)),
            pl.BlockSpec((seq, V_DIM), lambda b, hh: (b, COL_GM // V_DIM + hh)),
        ],
        out_specs=pl.BlockSpec((seq, V_DIM), lambda b, hh: (b, hh)),
        out_shape=jax.ShapeDtypeStruct((bsz * seq, h * V_DIM), BF16),
        compiler_params=_params("parallel", "parallel"),
        name="flash_prompt",
    )(q, k, v_t, proj)


PAGED_CHUNK = 1024
PAGED_GROUPS = 4


def _paged_kernel(pt_ref, qa_ref, qp_ref, cn_ref, kn_ref, ckv_hbm, krt_hbm, o_ref,
                  cbuf, kbuf, cbf, s_ref, sems, *, layer, n_pages):
    b = pl.program_id(0)
    nb = pl.num_programs(0)
    slot = b % 2

    def copies(bb, sl, p):
        page = pt_ref[bb * n_pages + p]
        toks = pl.ds(p * PAGE_SIZE, PAGE_SIZE)
        return (pltpu.make_async_copy(ckv_hbm.at[page, layer], cbuf.at[sl, toks, :], sems.at[0, sl]),
                pltpu.make_async_copy(krt_hbm.at[page, layer], kbuf.at[sl, :, toks], sems.at[1, sl]))

    def start_all(bb, sl):
        for p in range(n_pages):
            c1, c2 = copies(bb, sl, p)
            c1.start(priority=p % 2)
            c2.start(priority=(p + 1) % 2)

    @pl.when(b == 0)
    def _():
        start_all(b, slot)

    @pl.when(b + 1 < nb)
    def _():
        start_all(b + 1, 1 - slot)

    for p in range(n_pages):
        c1, c2 = copies(b, slot, p)
        c1.wait()
        c2.wait()

    past = n_pages * PAGE_SIZE
    n_chunks = past // PAGED_CHUNK
    per_group = n_chunks // PAGED_GROUPS
    qa = qa_ref[0].astype(BF16)
    qp = qp_ref[0][:, :QK_ROPE].astype(BF16)

    def qk_chunk(c):
        toks = slice(c * PAGED_CHUNK, (c + 1) * PAGED_CHUNK)
        cc = cbuf[slot, toks, :].astype(BF16)
        kk = kbuf[slot, :, toks].astype(BF16)
        cbf[toks, :] = cc
        s_ref[:, toks] = _dot_nt(qa, cc) + _dot(qp, kk)

    def pv_chunk(c, m_g, l_g, acc_g):
        toks = slice(c * PAGED_CHUNK, (c + 1) * PAGED_CHUNK)
        p = jnp.exp2(s_ref[:, toks] - m_g)
        return l_g + jnp.sum(p, axis=-1, keepdims=True), acc_g + _dot(p.astype(BF16), cbf[toks, :])

    def group_max(g):
        g_toks = slice(g * per_group * PAGED_CHUNK, (g + 1) * per_group * PAGED_CHUNK)
        return jnp.max(s_ref[:, g_toks], axis=-1, keepdims=True)

    stats = []
    for c in range(per_group):
        qk_chunk(c)
    for g in range(PAGED_GROUPS):
        m_g = group_max(g)
        l_g = jnp.zeros((MLA_HEADS, 1), F32)
        acc_g = jnp.zeros((MLA_HEADS, KV_LORA), F32)
        for c in range(per_group):
            if g + 1 < PAGED_GROUPS:
                qk_chunk((g + 1) * per_group + c)
            l_g, acc_g = pv_chunk(g * per_group + c, m_g, l_g, acc_g)
        stats.append((m_g, l_g, acc_g))

    c_new = cn_ref[0].astype(BF16).astype(F32)
    k_new = kn_ref[0].astype(BF16).astype(F32)
    s_new = (jnp.sum(qa.astype(F32) * c_new, axis=-1, keepdims=True)
             + jnp.sum(qp.astype(F32) * k_new, axis=-1, keepdims=True))
    m = s_new
    for m_g, _, _ in stats:
        m = jnp.maximum(m, m_g)
    p_new = jnp.exp2(s_new - m)
    l = p_new
    acc = p_new.astype(BF16).astype(F32) * c_new
    for m_g, l_g, acc_g in stats:
        w_g = jnp.exp2(m_g - m)
        l = l + w_g * l_g
        acc = acc + w_g * acc_g
    o_ref[0] = acc / l


def paged_attention(page_table, qa, qp, c_new, k_new, cache_kv, cache_krt, *, layer):
    bsz, n_pages = page_table.shape
    past = n_pages * PAGE_SIZE
    h = MLA_HEADS
    grid_spec = pltpu.PrefetchScalarGridSpec(
        num_scalar_prefetch=1,
        grid=(bsz,),
        in_specs=[
            pl.BlockSpec((1, h, KV_LORA), lambda b, pt: (b, 0, 0)),
            pl.BlockSpec((1, h, LANE), lambda b, pt: (b, 0, 0)),
            pl.BlockSpec((1, 1, KV_LORA), lambda b, pt: (b, 0, 0)),
            pl.BlockSpec((1, 1, QK_ROPE), lambda b, pt: (b, 0, 0)),
            pl.BlockSpec(memory_space=pl.ANY),
            pl.BlockSpec(memory_space=pl.ANY),
        ],
        out_specs=pl.BlockSpec((1, h, KV_LORA), lambda b, pt: (b, 0, 0)),
        scratch_shapes=[
            pltpu.VMEM((2, past, KV_LORA), F32),
            pltpu.VMEM((2, QK_ROPE, past), F32),
            pltpu.VMEM((past, KV_LORA), BF16),
            pltpu.VMEM((h, past), F32),
            pltpu.SemaphoreType.DMA((2, 2)),
        ],
    )
    return pl.pallas_call(
        functools.partial(_paged_kernel, layer=layer, n_pages=n_pages),
        grid_spec=grid_spec,
        out_shape=jax.ShapeDtypeStruct((bsz, h, KV_LORA), F32),
        compiler_params=_params("arbitrary"),
        name="paged_attention",
    )(page_table.reshape(-1), qa, qp, c_new, k_new, cache_kv, cache_krt)


def _uv_kernel(ol_ref, wuv_ref, g_ref, o_ref):
    for h in range(MLA_HEADS):
        o = _dot(ol_ref[h].astype(BF16), wuv_ref[h])
        sl = slice(h * V_DIM, (h + 1) * V_DIM)
        o_ref[:, sl] = (o * _silu(g_ref[:, sl])).astype(o_ref.dtype)


def uv_project(o_lat, wuv3, proj):
    h, t, _ = o_lat.shape
    return pl.pallas_call(
        _uv_kernel,
        grid=(1,),
        in_specs=[
            _full((h, t, KV_LORA)), _full((h, KV_LORA, V_DIM)),
            pl.BlockSpec((t, MLA_WIDTH), lambda i: (0, COL_GM // MLA_WIDTH)),
        ],
        out_specs=_full((t, MLA_WIDTH)),
        out_shape=jax.ShapeDtypeStruct((t, MLA_WIDTH), BF16),
        compiler_params=_params("arbitrary"),
        name="uv_project",
    )(o_lat, wuv3, proj)


def _proj_norm_res_kernel(*refs, n_in):
    ins, ws = refs[:n_in], refs[n_in:2 * n_in]
    x_ref, g_ref, o_ref = refs[2 * n_in:]
    y = _dot(ins[0][...].astype(BF16), ws[0][...])
    for a, w in zip(ins[1:], ws[1:]):
        y = y + _dot(a[...].astype(BF16), w[...])
    o_ref[...] = x_ref[...] + _rms(y, g_ref[...])


def proj_norm_res(ins, ws, x, g, *, tm, name):
    t, d = x.shape
    n_in = len(ins)
    return pl.pallas_call(
        functools.partial(_proj_norm_res_kernel, n_in=n_in),
        grid=(t // tm,),
        in_specs=([pl.BlockSpec((tm, a.shape[1]), lambda i: (i, 0)) for a in ins]
                  + [_full(w.shape) for w in ws]
                  + [pl.BlockSpec((tm, d), lambda i: (i, 0)), _full((1, d))]),
        out_specs=pl.BlockSpec((tm, d), lambda i: (i, 0)),
        out_shape=jax.ShapeDtypeStruct((t, d), F32),
        compiler_params=_params("parallel"),
        name=name,
    )(*ins, *ws, x, g.reshape(1, d))


def _mix_cross_kernel(conv_ref, attn_ref, x_ref, wc_ref, wm_ref, gmix_ref,
                      gpre_ref, wq_ref, mk_ref, mv_ref, wo_ref, gpost_ref, o_ref):
    y = _dot(conv_ref[...].astype(BF16), wc_ref[...]) + _dot(attn_ref[...], wm_ref[...])
    x = x_ref[...] + _rms(y, gmix_ref[...])
    hn = _rms(x, gpre_ref[...]).astype(BF16)
    q = (_dot(hn, wq_ref[...]) * MEM_SCALE).astype(BF16)
    outs = []
    for h in range(MEM_HEADS):
        sl = slice(h * MEM_HEAD_DIM, (h + 1) * MEM_HEAD_DIM)
        s = _dot_nt(q[:, sl], mk_ref[:, sl].astype(BF16))
        m = jnp.max(s, axis=-1, keepdims=True)
        p = jnp.exp(s - m)
        l = jnp.sum(p, axis=-1, keepdims=True)
        outs.append((_dot(p.astype(BF16), mv_ref[:, sl].astype(BF16)) / l).astype(BF16))
    y2 = _dot(jnp.concatenate(outs, axis=-1), wo_ref[...])
    o_ref[...] = x + _rms(y2, gpost_ref[...])


def mix_cross_prompt(conv_out, attn, x, wc, wm, g_mix, g_pre, wq, mem_kv, wo, g_post, *, seq, tm):
    t, d = x.shape
    per_seq = seq // tm
    const = lambda shape: pl.BlockSpec(shape, lambda i: (0,) * len(shape), pipeline_mode=pl.Buffered(1))
    rows = lambda width: pl.BlockSpec((tm, width), lambda i: (i, 0))
    return pl.pallas_call(
        _mix_cross_kernel,
        grid=(t // tm,),
        in_specs=[
            rows(CONV_CH), rows(MLA_WIDTH), rows(d),
            const(wc.shape), const(wm.shape), const((1, d)),
            const((1, d)), const(wq.shape),
            pl.BlockSpec((N_MEM, MEM_WIDTH), lambda i: (i // per_seq, 0)),
            pl.BlockSpec((N_MEM, MEM_WIDTH), lambda i: (i // per_seq, 1)),
            const(wo.shape), const((1, d)),
        ],
        out_specs=rows(d),
        out_shape=jax.ShapeDtypeStruct((t, d), F32),
        compiler_params=_params("parallel"),
        name="mix_cross_prompt",
    )(conv_out, attn, x, wc, wm, g_mix.reshape(1, d), g_pre.reshape(1, d), wq, mem_kv, mem_kv, wo,
      g_post.reshape(1, d))


def _cross_sample_kernel(q_ref, mk_ref, mv_ref, o_ref):
    bt = q_ref.shape[0]
    q = (q_ref[...] * MEM_SCALE).astype(BF16)
    for h in range(MEM_HEADS):
        sl = slice(h * MEM_HEAD_DIM, (h + 1) * MEM_HEAD_DIM)
        rows = pl.ds(h, N_MEM, stride=MEM_HEADS)
        qh = jnp.broadcast_to(q[:, :, sl], (bt, 8, MEM_HEAD_DIM))
        kh = mk_ref[:, 0, rows, :].astype(BF16)
        vh = mv_ref[:, 0, rows, :].astype(BF16)
        s = jnp.einsum("bqe,bme->bqm", qh, kh, preferred_element_type=F32)
        m = jnp.max(s, axis=-1, keepdims=True)
        p = jnp.exp(s - m)
        l = jnp.sum(p, axis=-1, keepdims=True)
        o = jnp.einsum("bqm,bme->bqe", p.astype(BF16), vh, preferred_element_type=F32) / l
        o_ref[:, :, sl] = o[:, 0:1, :].astype(o_ref.dtype)


def cross_sample_core(q3, mem_k, mem_v, *, layer, bt):
    bsz = q3.shape[0]
    mem = lambda: pl.BlockSpec((bt, 1, N_MEM * MEM_HEADS, MEM_HEAD_DIM), lambda i: (i, layer, 0, 0))
    return pl.pallas_call(
        _cross_sample_kernel,
        grid=(bsz // bt,),
        in_specs=[pl.BlockSpec((bt, 1, MEM_WIDTH), lambda i: (i, 0, 0)), mem(), mem()],
        out_specs=pl.BlockSpec((bt, 1, MEM_WIDTH), lambda i: (i, 0, 0)),
        out_shape=jax.ShapeDtypeStruct((bsz, 1, MEM_WIDTH), BF16),
        compiler_params=_params("parallel"),
        name="cross_sample",
    )(q3, mem_k, mem_v)


def _rope_tables(pos):
    half = QK_ROPE // 2
    inv_freq = ROPE_BASE ** (-jnp.arange(half, dtype=F32) / half)
    ang = pos.astype(F32)[:, None] * inv_freq[None, :]
    cos, sin = jnp.cos(ang), jnp.sin(ang)
    z = jnp.zeros_like(cos)
    return (jnp.concatenate([cos, cos, z, z], axis=-1),
            jnp.concatenate([-sin, z, z, z], axis=-1),
            jnp.concatenate([z, sin, z, z], axis=-1))


def _prep_layer(l, w_uq, w_uk, w_uv, w_out, w_xq, w_mk, w_mv, w_xo):
    d = w_out.shape[2]
    uq = w_uq[l]
    wqn = uq[:, :, :QK_NOPE].reshape(Q_LORA, -1).astype(BF16)
    wqp = jnp.pad(uq[:, :, QK_NOPE:], ((0, 0), (0, 0), (0, LANE - QK_ROPE))).reshape(Q_LORA, -1).astype(BF16)
    return dict(
        wqn=wqn, wqp=wqp,
        wuk=w_uk[l].reshape(KV_LORA, -1).astype(BF16),
        wuv=w_uv[l].reshape(KV_LORA, -1).astype(BF16),
        wukt=w_uk[l].transpose(1, 2, 0).astype(BF16),
        wuv3=w_uv[l].transpose(1, 0, 2).astype(BF16),
        w_out_c=w_out[l][:CONV_CH].astype(BF16),
        w_out_m=w_out[l][CONV_CH:].astype(BF16),
        w_xq=w_xq[l].reshape(d, -1).astype(BF16),
        w_mkv=jnp.concatenate([w_mk[l].reshape(d, -1), w_mv[l].reshape(d, -1)], axis=1).astype(BF16),
        w_xo=w_xo[l].reshape(-1, d).astype(BF16),
    )


def kernel(x_prompt, x_sample, mem_prompt, cache_kv_latent, cache_k_rope, state_conv, cache_mem_k, cache_mem_v, page_table, norm_mix_pre, w_in, conv_w, conv_b, conv_ln_g, conv_ln_b, q_norm_g, w_uq, kv_norm_g, w_uk, w_uv, w_out, norm_mix_post, norm_x_pre, norm_mem, w_xq, w_mk, w_mv, w_xo, norm_x_post):
    b_p, s_p, d = x_prompt.shape
    b_s = x_sample.shape[0]
    depth = w_in.shape[0]
    n_pages = page_table.shape[1]
    t_p = b_p * s_p
    tm_p = min(512, s_p)

    tabs_p = _rope_tables(jnp.arange(s_p, dtype=jnp.int32))
    tabs_s = _rope_tables(jnp.full((b_s,), n_pages * PAGE_SIZE, jnp.int32))
    mem2 = mem_prompt.reshape(b_p * N_MEM, d)
    mem_k_s = cache_mem_k.reshape(b_s, depth, N_MEM * MEM_HEADS, MEM_HEAD_DIM)
    mem_v_s = cache_mem_v.reshape(b_s, depth, N_MEM * MEM_HEADS, MEM_HEAD_DIM)
    cache_krt = cache_k_rope.transpose(0, 1, 3, 2)
    state_t = state_conv.transpose(0, 2, 1, 3)
    conv_w_t = conv_w.transpose(1, 0, 2)
    new_state_t, conv_part = conv_state_shift(state_t, conv_w_t, bt=8)
    w_in_t = w_in_prep(w_in.transpose(0, 2, 1))

    xp = x_prompt.reshape(t_p, d)
    xs = x_sample.reshape(b_s, d)
    lat_p, kpe_p, conv_p, mk_p, mv_p, lat_s, kpe_s = [], [], [], [], [], [], []
    u_s = []
    for l in range(depth):
        w = _prep_layer(l, w_uq, w_uk, w_uv, w_out, w_xq, w_mk, w_mv, w_xo)

        proj = rms_matmul(xp, norm_mix_pre[l], w_in_t, tm=min(1024, s_p), tn=1024, name="in_proj_p",
                          w_transposed=True, layer=l)
        conv_out, conv_state = conv_prompt(proj, conv_w[l], conv_b[l], conv_ln_g[l], conv_ln_b[l],
                                           bsz=b_p, seq=s_p)
        q, k, v, c_kv, k_pe = mla_prompt(proj, tabs_p, q_norm_g[l], kv_norm_g[l],
                                         w["wqn"], w["wqp"], w["wuk"], w["wuv"], tm=tm_p)
        attn = flash_prompt(q, k, v, proj, bsz=b_p, seq=s_p)
        mem_kv = rms_matmul(mem2, norm_mem[l], w["w_mkv"], tm=N_MEM, tn=2 * MEM_WIDTH, name="mem_kv")
        xp = mix_cross_prompt(conv_out, attn, xp, w["w_out_c"], w["w_out_m"], norm_mix_post[l],
                              norm_x_pre[l], w["w_xq"], mem_kv, w["w_xo"], norm_x_post[l], seq=s_p, tm=tm_p)
        lat_p.append(c_kv.reshape(b_p, s_p, KV_LORA))
        kpe_p.append(k_pe.reshape(b_p, s_p, QK_ROPE))
        conv_p.append(conv_state)
        mk_p.append(mem_kv[:, :MEM_WIDTH].reshape(b_p, N_MEM, MEM_HEADS, MEM_HEAD_DIM))
        mv_p.append(mem_kv[:, MEM_WIDTH:].reshape(b_p, N_MEM, MEM_HEADS, MEM_HEAD_DIM))

        proj_s = rms_matmul(xs, norm_mix_pre[l], w_in_t, tm=b_s, tn=1024, name="in_proj_s",
                            w_transposed=True, layer=l)
        conv_out_s, u_l = conv_sample(proj_s, conv_part, l, conv_w[l], conv_b[l], conv_ln_g[l], conv_ln_b[l])
        u_s.append(u_l)
        qa, qp, c_new, k_new = mla_sample(proj_s, tabs_s, q_norm_g[l], kv_norm_g[l],
                                          w["wqn"], w["wqp"], w["wukt"])
        o_lat = paged_attention(page_table, qa.transpose(1, 0, 2), qp.transpose(1, 0, 2),
                                c_new.reshape(b_s, 1, KV_LORA), k_new.reshape(b_s, 1, QK_ROPE),
                                cache_kv_latent, cache_krt, layer=l)
        attn_s = uv_project(o_lat.transpose(1, 0, 2), w["wuv3"], proj_s)
        xs = proj_norm_res([conv_out_s, attn_s], [w["w_out_c"], w["w_out_m"]],
                           xs, norm_mix_post[l], tm=b_s, name="out_proj_s")
        q_x = rms_matmul(xs, norm_x_pre[l], w["w_xq"], tm=b_s, tn=MEM_WIDTH, name="cross_q_s")
        o_x = cross_sample_core(q_x.reshape(b_s, 1, MEM_WIDTH), mem_k_s, mem_v_s, layer=l, bt=8)
        xs = proj_norm_res([o_x.reshape(b_s, MEM_WIDTH)], [w["w_xo"]], xs, norm_x_post[l],
                           tm=b_s, name="cross_out_s")
        lat_s.append(c_new.reshape(b_s, 1, KV_LORA))
        kpe_s.append(k_new.reshape(b_s, 1, QK_ROPE))

    return (xp.reshape(b_p, s_p, d), xs.reshape(b_s, 1, d),
            jnp.stack(lat_p, axis=1), jnp.stack(kpe_p, axis=1), jnp.stack(conv_p, axis=1),
            jnp.stack(mk_p, axis=1), jnp.stack(mv_p, axis=1),
            jnp.stack(lat_s, axis=1), jnp.stack(kpe_s, axis=1),
            conv_state_finish(new_state_t, jnp.stack(u_s, axis=1)[:, None]).transpose(0, 2, 1, 3))
```

```python
import functools

import jax
import jax.numpy as jnp
from jax import lax
from jax.experimental import pallas as pl
from jax.experimental.pallas import tpu as pltpu

F32 = jnp.float32
BF16 = jnp.bfloat16

D_MODEL = 2048
CONV_CH = 1024
CONV_WIDTH = 31
CONV_STATE = CONV_WIDTH - 1
MLA_HEADS = 8
QK_NOPE = 128
QK_ROPE = 64
V_DIM = 128
MLA_WIDTH = MLA_HEADS * V_DIM
Q_LORA = 512
KV_LORA = 256
MLA_SCALE = (QK_NOPE + QK_ROPE) ** -0.5
ROPE_BASE = 10000.0
PAGE_SIZE = 128
N_MEM = 256
MEM_HEADS = 4
MEM_HEAD_DIM = 128
MEM_WIDTH = MEM_HEADS * MEM_HEAD_DIM
MEM_SCALE = MEM_HEAD_DIM ** -0.5
EPS = 1e-6
LOG2E = 1.4426950408889634

LANE = 128
QK_PAD = 256
NEG_BIG = -1e30

PROJ_COLS = 5120
COL_A, COL_B, COL_GC, COL_GM, COL_QL, COL_KV, COL_KPE = 0, 1024, 2048, 3072, 4096, 4608, 4864

VMEM_LIMIT = 52 * 1024 * 1024


def _params(*sem):
    return pltpu.CompilerParams(dimension_semantics=sem, vmem_limit_bytes=VMEM_LIMIT)


def _rms(x, g):
    return x * lax.rsqrt(jnp.mean(x * x, axis=-1, keepdims=True) + EPS) * g


def _silu(x):
    return x * jax.nn.sigmoid(x)


def _dot(a, b):
    return jnp.dot(a, b, preferred_element_type=F32)


def _dot_nt(a, b):
    return lax.dot_general(a, b, (((1,), (1,)), ((), ())), preferred_element_type=F32)


def _rope128(x, cos_t, sin_n, sin_p):
    return x * cos_t + pltpu.roll(x, 96, 1) * sin_n + pltpu.roll(x, 32, 1) * sin_p


def _rms_matmul_kernel(x_ref, g_ref, w_ref, o_ref, hn_ref, *, w_transposed):
    @pl.when(pl.program_id(1) == 0)
    def _():
        hn_ref[...] = _rms(x_ref[...], g_ref[...]).astype(BF16)

    dot = _dot_nt if w_transposed else _dot
    o_ref[...] = dot(hn_ref[...], w_ref[...]).astype(o_ref.dtype)


def rms_matmul(x, g, w, *, tm, tn, name, w_transposed=False, layer=None):
    t, k = x.shape
    n = w.shape[-2] if w_transposed else w.shape[-1]
    lead = () if layer is None else (None,)
    pick = () if layer is None else (layer,)
    w_spec = (pl.BlockSpec(lead + (tn, k), lambda i, j: pick + (j, 0)) if w_transposed
              else pl.BlockSpec(lead + (k, tn), lambda i, j: pick + (0, j)))
    return pl.pallas_call(
        functools.partial(_rms_matmul_kernel, w_transposed=w_transposed),
        grid=(t // tm, n // tn),
        in_specs=[
            pl.BlockSpec((tm, k), lambda i, j: (i, 0)),
            pl.BlockSpec((1, k), lambda i, j: (0, 0)),
            w_spec,
        ],
        out_specs=pl.BlockSpec((tm, tn), lambda i, j: (i, j)),
        out_shape=jax.ShapeDtypeStruct((t, n), F32),
        scratch_shapes=[pltpu.VMEM((tm, k), BF16)],
        compiler_params=_params("parallel", "arbitrary"),
        name=name,
    )(x, g.reshape(1, k), w)


_W_IN_GROUPS = (
    (COL_A, 0, 3 * CONV_CH),
    (COL_GM, 3 * CONV_CH + Q_LORA + KV_LORA + QK_ROPE, MLA_WIDTH),
    (COL_QL, 3 * CONV_CH, Q_LORA),
    (COL_KV, 3 * CONV_CH + Q_LORA, KV_LORA),
    (COL_KPE, 3 * CONV_CH + Q_LORA + KV_LORA, QK_ROPE),
)
W_PREP_TK = 512


def _w_in_prep_kernel(w_ref, o_ref):
    for dst, src, rows in _W_IN_GROUPS:
        o_ref[0, dst:dst + rows, :] = w_ref[0, src:src + rows, :].astype(BF16)
    pad0 = COL_KPE + QK_ROPE
    o_ref[0, pad0:, :] = jnp.zeros((PROJ_COLS - pad0, o_ref.shape[2]), BF16)


def w_in_prep(w_in_t):
    depth, n_in, k = w_in_t.shape
    return pl.pallas_call(
        _w_in_prep_kernel,
        grid=(depth, k // W_PREP_TK),
        in_specs=[pl.BlockSpec((1, n_in, W_PREP_TK), lambda l, j: (l, 0, j))],
        out_specs=pl.BlockSpec((1, PROJ_COLS, W_PREP_TK), lambda l, j: (l, 0, j)),
        out_shape=jax.ShapeDtypeStruct((depth, PROJ_COLS, k), BF16),
        compiler_params=_params("parallel", "parallel"),
        name="w_in_prep",
    )(w_in_t)


CONV_SEGS = 8
CONV_SEG_ROWS = 128
CONV_HALO = 32
CONV_BLOCK = 16
CONV_TAPS_A = 16
CONV_UNROLL = 16
CONV_PAD = 32
CONV_TILE = CONV_SEGS * CONV_SEG_ROWS


def _conv_prompt_kernel(proj_hbm, w_ref, cb_ref, lg_ref, lb_ref, out_hbm, state_ref,
                        xa, xb, xg, y_ref, o_ref, wb_ref, lgb_ref, in_sems, out_sem, *, tiles_per_seq):
    n = pl.program_id(0)
    n_tiles = pl.num_programs(0)
    slot = n % 2
    lrows, halo = CONV_SEG_ROWS, CONV_HALO

    def in_copies(tile, sl, seg, first):
        row = tile * CONV_TILE + seg * lrows
        if first:
            src_rows, dst_rows = pl.ds(row, lrows), pl.ds(halo, lrows)
        else:
            src_rows, dst_rows = pl.ds(row - halo, lrows + halo), pl.ds(0, lrows + halo)
        return (
            pltpu.make_async_copy(proj_hbm.at[src_rows, pl.ds(COL_A, CONV_CH)],
                                  xa.at[sl, dst_rows, seg, :], in_sems.at[sl, 0]),
            pltpu.make_async_copy(proj_hbm.at[src_rows, pl.ds(COL_B, CONV_CH)],
                                  xb.at[sl, dst_rows, seg, :], in_sems.at[sl, 1]),
            pltpu.make_async_copy(proj_hbm.at[pl.ds(row, lrows), pl.ds(COL_GC, CONV_CH)],
                                  xg.at[sl, :, seg, :], in_sems.at[sl, 2]),
        )

    def for_each_in_copy(tile, sl, fn):
        seq_start = tile % tiles_per_seq == 0

        @pl.when(seq_start)
        def _():
            for c in in_copies(tile, sl, 0, True):
                fn(c)

        @pl.when(jnp.logical_not(seq_start))
        def _():
            for c in in_copies(tile, sl, 0, False):
                fn(c)

        for seg in range(1, CONV_SEGS):
            for c in in_copies(tile, sl, seg, False):
                fn(c)

    def out_copies(tile):
        return [pltpu.make_async_copy(o_ref.at[:, seg, :],
                                      out_hbm.at[pl.ds(tile * CONV_TILE + seg * lrows, lrows), :], out_sem.at[0])
                for seg in range(CONV_SEGS)]

    @pl.when(n == 0)
    def _():
        for k in range(CONV_WIDTH):
            wb_ref[k] = jnp.broadcast_to(w_ref[k:k + 1, :], (CONV_SEGS, CONV_CH))
        lgb_ref[0] = jnp.broadcast_to(lg_ref[...], (CONV_SEGS, CONV_CH))
        lgb_ref[1] = jnp.broadcast_to(lb_ref[...], (CONV_SEGS, CONV_CH))
        for_each_in_copy(n, slot, lambda c: c.start())

    @pl.when(n + 1 < n_tiles)
    def _():
        for_each_in_copy(n + 1, 1 - slot, lambda c: c.start())

    for_each_in_copy(n, slot, lambda c: c.wait())

    def glu(i, carry):
        xa[slot, i] = xa[slot, i] * jax.nn.sigmoid(xb[slot, i])
        return carry

    lax.fori_loop(0, lrows + halo, glu, 0, unroll=4)

    @pl.when(n % tiles_per_seq == 0)
    def _():
        xa[slot, 0:halo, 0:1, :] = jnp.zeros((halo, 1, CONV_CH), F32)

    @pl.when(n % tiles_per_seq == tiles_per_seq - 1)
    def _():
        state_ref[0] = xa[slot, halo + lrows - CONV_STATE:halo + lrows, CONV_SEGS - 1, :]

    n_zero = CONV_PAD - (CONV_TAPS_A - 1)
    y_ref[0:n_zero] = jnp.zeros((n_zero, CONV_SEGS, CONV_CH), F32)
    shift = halo - CONV_STATE
    zero = jnp.zeros((CONV_SEGS, LANE), F32)
    for k0, n_taps in ((0, CONV_TAPS_A), (CONV_TAPS_A, CONV_WIDTH - CONV_TAPS_A)):
        first = k0 == 0
        n_iter = -(-(lrows + n_taps - 1) // CONV_UNROLL)
        j_start = min(shift + k0, lrows + halo - n_iter * CONV_UNROLL)
        for c0 in range(0, CONV_CH, LANE):
            cols = slice(c0, c0 + LANE)
            w = [wb_ref[k0 + t, :, cols] for t in range(n_taps)]
            fresh = (cb_ref[:, cols] + zero) if first else zero
            row_off = CONV_PAD - shift - k0 - (n_taps - 1)

            def body(it, carry, w=w, fresh=fresh, cols=cols, row_off=row_off, first=first, j_start=j_start):
                acc = list(carry)
                j0 = it * CONV_UNROLL + j_start
                for jj in range(CONV_UNROLL):
                    u = xa[slot, j0 + jj, :, cols]
                    acc = [a + wt * u for wt, a in zip(w, [fresh] + acc)]
                    done = acc.pop()
                    if first:
                        y_ref[j0 + (jj + row_off), :, cols] = done
                    else:
                        y_ref[j0 + (jj + row_off), :, cols] = y_ref[j0 + (jj + row_off), :, cols] + done
                return tuple(acc)

            lax.fori_loop(0, n_iter, body, tuple(zero for _ in range(n_taps - 1)))

    @pl.when(n > 0)
    def _():
        for c in out_copies(n - 1):
            c.wait()

    def norm_gate(r, carry):
        for ii in range(CONV_BLOCK):
            i = r * CONV_BLOCK + ii
            acc = y_ref[i + CONV_PAD]
            mu = jnp.mean(acc, axis=-1, keepdims=True)
            d = acc - mu
            var = jnp.mean(d * d, axis=-1, keepdims=True)
            z = d * lax.rsqrt(var + EPS) * lgb_ref[0] + lgb_ref[1]
            o_ref[i] = _silu(z) * _silu(xg[slot, i])
        return carry

    lax.fori_loop(0, lrows // CONV_BLOCK, norm_gate, 0)

    for c in out_copies(n):
        c.start()

    @pl.when(n == n_tiles - 1)
    def _():
        for c in out_copies(n):
            c.wait()


def conv_prompt(proj, conv_w, conv_b, ln_g, ln_b, *, bsz, seq):
    tiles_per_seq = seq // CONV_TILE
    vec = lambda: pl.BlockSpec((1, CONV_CH), lambda n: (0, 0))
    seg_buf = lambda rows: pltpu.VMEM((2, rows, CONV_SEGS, CONV_CH), F32)
    return pl.pallas_call(
        functools.partial(_conv_prompt_kernel, tiles_per_seq=tiles_per_seq),
        grid=(bsz * tiles_per_seq,),
        in_specs=[
            pl.BlockSpec(memory_space=pl.ANY),
            pl.BlockSpec((CONV_WIDTH, CONV_CH), lambda n: (0, 0)),
            vec(), vec(), vec(),
        ],
        out_specs=[
            pl.BlockSpec(memory_space=pl.ANY),
            pl.BlockSpec((1, CONV_STATE, CONV_CH), lambda n: (n // tiles_per_seq, 0, 0)),
        ],
        out_shape=[
            jax.ShapeDtypeStruct((bsz * seq, CONV_CH), F32),
            jax.ShapeDtypeStruct((bsz, CONV_STATE, CONV_CH), F32),
        ],
        scratch_shapes=[
            seg_buf(CONV_SEG_ROWS + CONV_HALO), seg_buf(CONV_SEG_ROWS + CONV_HALO), seg_buf(CONV_SEG_ROWS),
            pltpu.VMEM((CONV_SEG_ROWS + 2 * CONV_PAD, CONV_SEGS, CONV_CH), F32),
            pltpu.VMEM((CONV_SEG_ROWS, CONV_SEGS, CONV_CH), F32),
            pltpu.VMEM((CONV_WIDTH, CONV_SEGS, CONV_CH), F32),
            pltpu.VMEM((2, CONV_SEGS, CONV_CH), F32),
            pltpu.SemaphoreType.DMA((2, 3)), pltpu.SemaphoreType.DMA((1,)),
        ],
        compiler_params=_params("arbitrary"),
        name="conv_prompt",
    )(proj, conv_w, conv_b.reshape(1, -1), ln_g.reshape(1, -1), ln_b.reshape(1, -1))


def _conv_state_kernel(st_ref, w_ref, newst_ref, part_ref):
    st = st_ref[...]
    newst_ref[:, 0:CONV_STATE - 1] = st[:, 1:CONV_STATE]
    newst_ref[:, CONV_STATE - 1:CONV_STATE] = jnp.zeros_like(st[:, 0:1])
    part_ref[...] = jnp.sum(st * w_ref[0:CONV_STATE][None], axis=1)


def conv_state_shift(state_t, conv_w_t, *, bt):
    bsz, _, depth, _ = state_t.shape
    blk = pl.BlockSpec((bt, CONV_STATE, depth, CONV_CH), lambda i: (i, 0, 0, 0))
    return pl.pallas_call(
        _conv_state_kernel,
        grid=(bsz // bt,),
        in_specs=[blk, pl.BlockSpec((CONV_WIDTH, depth, CONV_CH), lambda i: (0, 0, 0))],
        out_specs=[blk, pl.BlockSpec((bt, depth, CONV_CH), lambda i: (i, 0, 0))],
        out_shape=[jax.ShapeDtypeStruct(state_t.shape, F32),
                   jax.ShapeDtypeStruct((bsz, depth, CONV_CH), F32)],
        compiler_params=_params("parallel"),
        name="conv_state_shift",
    )(state_t, conv_w_t)


def _conv_state_finish_kernel(u_ref, st_hbm, o_ref):
    del st_hbm
    o_ref[...] = u_ref[...]


def conv_state_finish(new_state, u_all):
    bsz, _, depth, _ = new_state.shape
    return pl.pallas_call(
        _conv_state_finish_kernel,
        grid=(1,),
        in_specs=[pl.BlockSpec((bsz, 1, depth, CONV_CH), lambda i: (0, 0, 0, 0)),
                  pl.BlockSpec(memory_space=pl.ANY)],
        out_specs=pl.BlockSpec((bsz, 1, depth, CONV_CH), lambda i: (0, CONV_STATE - 1, 0, 0)),
        out_shape=jax.ShapeDtypeStruct(new_state.shape, F32),
        input_output_aliases={1: 0},
        compiler_params=_params("arbitrary"),
        name="conv_state_finish",
    )(u_all, new_state)


def _conv_sample_kernel(a_ref, b_ref, gc_ref, part_ref, w_ref, cb_ref, lg_ref, lb_ref, out_ref, u_ref, *, layer):
    u = a_ref[...] * jax.nn.sigmoid(b_ref[...])
    u_ref[...] = u
    y = part_ref[:, layer, :] + u * w_ref[CONV_STATE:CONV_WIDTH, :] + cb_ref[...]
    mu = jnp.mean(y, axis=-1, keepdims=True)
    d = y - mu
    var = jnp.mean(d * d, axis=-1, keepdims=True)
    z = d * lax.rsqrt(var + EPS) * lg_ref[...] + lb_ref[...]
    out_ref[...] = (_silu(z) * _silu(gc_ref[...])).astype(out_ref.dtype)


def conv_sample(proj, part, layer, conv_w, conv_b, ln_g, ln_b):
    bsz = proj.shape[0]
    vec = lambda: pl.BlockSpec((1, CONV_CH), lambda i: (0, 0))
    row = lambda: pl.BlockSpec((bsz, CONV_CH), lambda i: (0, 0))
    return pl.pallas_call(
        functools.partial(_conv_sample_kernel, layer=layer),
        grid=(1,),
        in_specs=[
            pl.BlockSpec((bsz, CONV_CH), lambda i: (0, COL_A // CONV_CH)),
            pl.BlockSpec((bsz, CONV_CH), lambda i: (0, COL_B // CONV_CH)),
            pl.BlockSpec((bsz, CONV_CH), lambda i: (0, COL_GC // CONV_CH)),
            pl.BlockSpec(part.shape, lambda i: (0, 0, 0)),
            pl.BlockSpec((CONV_WIDTH, CONV_CH), lambda i: (0, 0)),
            vec(), vec(), vec(),
        ],
        out_specs=[row(), row()],
        out_shape=[
            jax.ShapeDtypeStruct((bsz, CONV_CH), BF16),
            jax.ShapeDtypeStruct((bsz, CONV_CH), F32),
        ],
        compiler_params=_params("arbitrary"),
        name="conv_sample",
    )(proj, proj, proj, part, conv_w, conv_b.reshape(1, -1), ln_g.reshape(1, -1), ln_b.reshape(1, -1))


def _mla_common(ql_ref, kvl_ref, kpe_ref, cos_ref, sn_ref, sp_ref, qg_ref, kvg_ref, wqn_ref, wqp_ref):
    qn = _rms(ql_ref[...], qg_ref[...]).astype(BF16)
    q_nope = _dot(qn, wqn_ref[...]) * (MLA_SCALE * LOG2E)
    q_rope_raw = _dot(qn, wqp_ref[...]) * (MLA_SCALE * LOG2E)
    cos_t, sin_n, sin_p = cos_ref[...], sn_ref[...], sp_ref[...]
    q_rope = [_rope128(q_rope_raw[:, h * LANE:(h + 1) * LANE], cos_t, sin_n, sin_p)
              for h in range(MLA_HEADS)]
    c_kv = _rms(kvl_ref[...], kvg_ref[...])
    k_pe = _rope128(kpe_ref[...], cos_t, sin_n, sin_p)
    return q_nope, q_rope, c_kv, k_pe


def _mla_prompt_kernel(ql_ref, kvl_ref, kpe_ref, cos_ref, sn_ref, sp_ref, qg_ref, kvg_ref,
                       wqn_ref, wqp_ref, wuk_ref, wuv_ref,
                       q_ref, k_ref, v_ref, ckv_ref, kpeo_ref):
    q_nope, q_rope, c_kv, k_pe = _mla_common(ql_ref, kvl_ref, kpe_ref, cos_ref, sn_ref, sp_ref,
                                             qg_ref, kvg_ref, wqn_ref, wqp_ref)
    ckv_ref[...] = c_kv
    kpeo_ref[...] = k_pe[:, :QK_ROPE]
    c_bf = c_kv.astype(BF16)
    k_nope = _dot(c_bf, wuk_ref[...])
    v = _dot(c_bf, wuv_ref[...])
    for h in range(MLA_HEADS):
        v_ref[h * V_DIM:(h + 1) * V_DIM, :] = v[:, h * V_DIM:(h + 1) * V_DIM].T.astype(BF16)
    k_pe_bf = k_pe.astype(BF16)
    for h in range(MLA_HEADS):
        lo = h * QK_PAD
        q_ref[:, lo:lo + LANE] = q_nope[:, h * LANE:(h + 1) * LANE].astype(BF16)
        q_ref[:, lo + LANE:lo + QK_PAD] = q_rope[h].astype(BF16)
        k_ref[:, lo:lo + LANE] = k_nope[:, h * LANE:(h + 1) * LANE].astype(BF16)
        k_ref[:, lo + LANE:lo + QK_PAD] = k_pe_bf


def _proj_specs(tm, n_tab):
    return [
        pl.BlockSpec((tm, Q_LORA), lambda i: (i, COL_QL // Q_LORA)),
        pl.BlockSpec((tm, KV_LORA), lambda i: (i, COL_KV // KV_LORA)),
        pl.BlockSpec((tm, LANE), lambda i: (i, COL_KPE // LANE)),
        pl.BlockSpec((tm, LANE), lambda i: (i % n_tab, 0)),
        pl.BlockSpec((tm, LANE), lambda i: (i % n_tab, 0)),
        pl.BlockSpec((tm, LANE), lambda i: (i % n_tab, 0)),
    ]


def _full(shape):
    return pl.BlockSpec(shape, lambda i: (0,) * len(shape))


def mla_prompt(proj, tabs, q_g, kv_g, wqn, wqp, wuk, wuv, *, tm):
    t = proj.shape[0]
    n_tab = tabs[0].shape[0] // tm
    h = MLA_HEADS
    return pl.pallas_call(
        _mla_prompt_kernel,
        grid=(t // tm,),
        in_specs=_proj_specs(tm, n_tab) + [
            _full((1, Q_LORA)), _full((1, KV_LORA)),
            _full((Q_LORA, h * LANE)), _full((Q_LORA, h * LANE)),
            _full((KV_LORA, h * LANE)), _full((KV_LORA, h * LANE)),
        ],
        out_specs=[
            pl.BlockSpec((tm, h * QK_PAD), lambda i: (i, 0)),
            pl.BlockSpec((tm, h * QK_PAD), lambda i: (i, 0)),
            pl.BlockSpec((h * V_DIM, tm), lambda i: (0, i)),
            pl.BlockSpec((tm, KV_LORA), lambda i: (i, 0)),
            pl.BlockSpec((tm, QK_ROPE), lambda i: (i, 0)),
        ],
        out_shape=[
            jax.ShapeDtypeStruct((t, h * QK_PAD), BF16),
            jax.ShapeDtypeStruct((t, h * QK_PAD), BF16),
            jax.ShapeDtypeStruct((h * V_DIM, t), BF16),
            jax.ShapeDtypeStruct((t, KV_LORA), F32),
            jax.ShapeDtypeStruct((t, QK_ROPE), F32),
        ],
        compiler_params=_params("parallel"),
        name="mla_prompt",
    )(proj, proj, proj, *tabs, q_g.reshape(1, -1), kv_g.reshape(1, -1), wqn, wqp, wuk, wuv)


def _mla_sample_kernel(ql_ref, kvl_ref, kpe_ref, cos_ref, sn_ref, sp_ref, qg_ref, kvg_ref,
                       wqn_ref, wqp_ref, wukt_ref,
                       qa_ref, qp_ref, ckv_ref, kpeo_ref):
    q_nope, q_rope, c_kv, k_pe = _mla_common(ql_ref, kvl_ref, kpe_ref, cos_ref, sn_ref, sp_ref,
                                             qg_ref, kvg_ref, wqn_ref, wqp_ref)
    ckv_ref[...] = c_kv
    kpeo_ref[...] = k_pe[:, :QK_ROPE]
    for h in range(MLA_HEADS):
        qa_ref[h] = _dot(q_nope[:, h * LANE:(h + 1) * LANE].astype(BF16), wukt_ref[h])
        qp_ref[h] = q_rope[h]


def mla_sample(proj, tabs, q_g, kv_g, wqn, wqp, wukt):
    t = proj.shape[0]
    h = MLA_HEADS
    return pl.pallas_call(
        _mla_sample_kernel,
        grid=(1,),
        in_specs=_proj_specs(t, 1) + [
            _full((1, Q_LORA)), _full((1, KV_LORA)),
            _full((Q_LORA, h * LANE)), _full((Q_LORA, h * LANE)),
            _full((h, QK_NOPE, KV_LORA)),
        ],
        out_specs=[
            _full((h, t, KV_LORA)), _full((h, t, LANE)),
            _full((t, KV_LORA)), _full((t, QK_ROPE)),
        ],
        out_shape=[
            jax.ShapeDtypeStruct((h, t, KV_LORA), F32),
            jax.ShapeDtypeStruct((h, t, LANE), F32),
            jax.ShapeDtypeStruct((t, KV_LORA), F32),
            jax.ShapeDtypeStruct((t, QK_ROPE), F32),
        ],
        compiler_params=_params("arbitrary"),
        name="mla_sample",
    )(proj, proj, proj, *tabs, q_g.reshape(1, -1), kv_g.reshape(1, -1), wqn, wqp, wukt)


FLASH_TQ = 512


def _flash_tile(q, k_ref, vt_ref, s_t, kv):
    tq = FLASH_TQ
    causal_t = (lax.broadcasted_iota(jnp.int32, (tq, tq), 0)
                <= lax.broadcasted_iota(jnp.int32, (tq, tq), 1))
    if s_t is None:
        s_t = _dot_nt(k_ref[0:kv + tq, :], q)
    s_d = jnp.where(causal_t, s_t[kv:kv + tq], NEG_BIG)
    m = jnp.max(s_d, axis=0, keepdims=True)
    if kv > 0:
        s_o = s_t[0:kv]
        m = jnp.maximum(m, jnp.max(s_o, axis=0, keepdims=True))
        p_o = jnp.exp2(s_o - m)
        l = jnp.sum(p_o, axis=0, keepdims=True)
        acc = _dot(vt_ref[:, 0:kv], p_o.astype(BF16))
    else:
        l = jnp.zeros((1, tq), F32)
        acc = jnp.zeros((V_DIM, tq), F32)
    p_d = jnp.exp2(s_d - m)
    l = l + jnp.sum(p_d, axis=0, keepdims=True)
    acc = acc + _dot(vt_ref[:, kv:kv + tq], p_d.astype(BF16))
    return (acc / l).T


def _flash_kernel(q_ref, k_ref, vt_ref, g_ref, o_ref, *, seq):
    tq = FLASH_TQ
    n_q = seq // tq

    def scores(i):
        kv = i * tq
        return _dot_nt(k_ref[0:kv + tq, :], q_ref[kv:kv + tq, :])

    s_next = scores(0)
    for i in range(n_q):
        kv = i * tq
        s_t = s_next
        if i + 1 < n_q:
            s_next = scores(i + 1)
        o = _flash_tile(None, k_ref, vt_ref, s_t, kv)
        o_ref[kv:kv + tq, :] = (o * _silu(g_ref[kv:kv + tq, :])).astype(o_ref.dtype)


def flash_prompt(q, k, v_t, proj, *, bsz, seq):
    h = MLA_HEADS
    return pl.pallas_call(
        functools.partial(_flash_kernel, seq=seq),
        grid=(bsz, h),
        in_specs=[
            pl.BlockSpec((seq, QK_PAD), lambda b, hh: (b, hh)),
            pl.BlockSpec((seq, QK_PAD), lambda b, hh: (b, hh)),
            pl.BlockSpec((V_DIM, seq), lambda b, hh: (hh, b)),
            pl.BlockSpec((seq, V_DIM), lambda b, hh: (b, COL_GM // V_DIM + hh)),
        ],
        out_specs=pl.BlockSpec((seq, V_DIM), lambda b, hh: (b, hh)),
        out_shape=jax.ShapeDtypeStruct((bsz * seq, h * V_DIM), BF16),
        compiler_params=_params("parallel", "parallel"),
        name="flash_prompt",
    )(q, k, v_t, proj)


PAGED_CHUNK = 1024
PAGED_GROUPS = 4


def _paged_kernel(pt_ref, qa_ref, qp_ref, cn_ref, kn_ref, ckv_hbm, krt_hbm, *rest, layer, n_pages, flash_tiles):
    if flash_tiles:
        fq_ref, fk_ref, fvt_ref, fg_ref, o_ref, fo_ref, cbuf, kbuf, cbf, s_ref, sems = rest
    else:
        o_ref, cbuf, kbuf, cbf, s_ref, sems = rest
    b = pl.program_id(0)
    nb = pl.num_programs(0)
    slot = b % 2

    def copies(bb, sl, p):
        page = pt_ref[bb * n_pages + p]
        toks = pl.ds(p * PAGE_SIZE, PAGE_SIZE)
        return (pltpu.make_async_copy(ckv_hbm.at[page, layer], cbuf.at[sl, toks, :], sems.at[0, sl]),
                pltpu.make_async_copy(krt_hbm.at[page, layer], kbuf.at[sl, :, toks], sems.at[1, sl]))

    def start_all(bb, sl):
        for p in range(n_pages):
            c1, c2 = copies(bb, sl, p)
            c1.start(priority=p % 2)
            c2.start(priority=(p + 1) % 2)

    @pl.when(b == 0)
    def _():
        start_all(b, slot)

    @pl.when(b + 1 < nb)
    def _():
        start_all(b + 1, 1 - slot)

    for c in range(flash_tiles):
        @pl.when(b % flash_tiles == c)
        def _(c=c):
            o = _flash_tile(fq_ref[...], fk_ref, fvt_ref, None, c * FLASH_TQ)
            fo_ref[...] = (o * _silu(fg_ref[...])).astype(fo_ref.dtype)

    for p in range(n_pages):
        c1, c2 = copies(b, slot, p)
        c1.wait()
        c2.wait()

    past = n_pages * PAGE_SIZE
    n_chunks = past // PAGED_CHUNK
    per_group = n_chunks // PAGED_GROUPS
    qa = qa_ref[0].astype(BF16)
    qp = qp_ref[0][:, :QK_ROPE].astype(BF16)

    def qk_chunk(c):
        toks = slice(c * PAGED_CHUNK, (c + 1) * PAGED_CHUNK)
        cc = cbuf[slot, toks, :].astype(BF16)
        kk = kbuf[slot, :, toks].astype(BF16)
        cbf[toks, :] = cc
        s_ref[:, toks] = _dot_nt(qa, cc) + _dot(qp, kk)

    def pv_chunk(c, m_g, l_g, acc_g):
        toks = slice(c * PAGED_CHUNK, (c + 1) * PAGED_CHUNK)
        p = jnp.exp2(s_ref[:, toks] - m_g)
        return l_g + jnp.sum(p, axis=-1, keepdims=True), acc_g + _dot(p.astype(BF16), cbf[toks, :])

    def group_max(g):
        g_toks = slice(g * per_group * PAGED_CHUNK, (g + 1) * per_group * PAGED_CHUNK)
        return jnp.max(s_ref[:, g_toks], axis=-1, keepdims=True)

    stats = []
    for c in range(per_group):
        qk_chunk(c)
    for g in range(PAGED_GROUPS):
        m_g = group_max(g)
        l_g = jnp.zeros((MLA_HEADS, 1), F32)
        acc_g = jnp.zeros((MLA_HEADS, KV_LORA), F32)
        for c in range(per_group):
            if g + 1 < PAGED_GROUPS:
                qk_chunk((g + 1) * per_group + c)
            l_g, acc_g = pv_chunk(g * per_group + c, m_g, l_g, acc_g)
        stats.append((m_g, l_g, acc_g))

    c_new = cn_ref[0].astype(BF16).astype(F32)
    k_new = kn_ref[0].astype(BF16).astype(F32)
    s_new = (jnp.sum(qa.astype(F32) * c_new, axis=-1, keepdims=True)
             + jnp.sum(qp.astype(F32) * k_new, axis=-1, keepdims=True))
    m = s_new
    for m_g, _, _ in stats:
        m = jnp.maximum(m, m_g)
    p_new = jnp.exp2(s_new - m)
    l = p_new
    acc = p_new.astype(BF16).astype(F32) * c_new
    for m_g, l_g, acc_g in stats:
        w_g = jnp.exp2(m_g - m)
        l = l + w_g * l_g
        acc = acc + w_g * acc_g
    o_ref[0] = acc / l


def paged_attention(page_table, qa, qp, c_new, k_new, cache_kv, cache_krt, *, layer, flash=None):
    bsz, n_pages = page_table.shape
    past = n_pages * PAGE_SIZE
    h = MLA_HEADS
    in_specs = [
        pl.BlockSpec((1, h, KV_LORA), lambda b, pt: (b, 0, 0)),
        pl.BlockSpec((1, h, LANE), lambda b, pt: (b, 0, 0)),
        pl.BlockSpec((1, 1, KV_LORA), lambda b, pt: (b, 0, 0)),
        pl.BlockSpec((1, 1, QK_ROPE), lambda b, pt: (b, 0, 0)),
        pl.BlockSpec(memory_space=pl.ANY),
        pl.BlockSpec(memory_space=pl.ANY),
    ]
    args = [qa, qp, c_new, k_new, cache_kv, cache_krt]
    out_specs = [pl.BlockSpec((1, h, KV_LORA), lambda b, pt: (b, 0, 0))]
    out_shape = [jax.ShapeDtypeStruct((bsz, h, KV_LORA), F32)]
    flash_tiles = 0
    if flash is not None:
        fq, fk, fvt, proj, f_bsz, seq = flash
        flash_tiles = seq // FLASH_TQ
        assert f_bsz * h * flash_tiles == bsz
        tile = lambda b: (b // flash_tiles // h) * flash_tiles + b % flash_tiles
        head = lambda b: b // flash_tiles % h
        in_specs += [
            pl.BlockSpec((FLASH_TQ, QK_PAD), lambda b, pt: (tile(b), head(b))),
            pl.BlockSpec((seq, QK_PAD), lambda b, pt: (b // flash_tiles // h, head(b))),
            pl.BlockSpec((V_DIM, seq), lambda b, pt: (head(b), b // flash_tiles // h)),
            pl.BlockSpec((FLASH_TQ, V_DIM), lambda b, pt: (tile(b), COL_GM // V_DIM + head(b))),
        ]
        args += [fq, fk, fvt, proj]
        out_specs.append(pl.BlockSpec((FLASH_TQ, V_DIM), lambda b, pt: (tile(b), head(b))))
        out_shape.append(jax.ShapeDtypeStruct((f_bsz * seq, h * V_DIM), BF16))
    grid_spec = pltpu.PrefetchScalarGridSpec(
        num_scalar_prefetch=1,
        grid=(bsz,),
        in_specs=in_specs,
        out_specs=out_specs,
        scratch_shapes=[
            pltpu.VMEM((2, past, KV_LORA), F32),
            pltpu.VMEM((2, QK_ROPE, past), F32),
            pltpu.VMEM((past, KV_LORA), BF16),
            pltpu.VMEM((h, past), F32),
            pltpu.SemaphoreType.DMA((2, 2)),
        ],
    )
    outs = pl.pallas_call(
        functools.partial(_paged_kernel, layer=layer, n_pages=n_pages, flash_tiles=flash_tiles),
        grid_spec=grid_spec,
        out_shape=out_shape,
        compiler_params=_params("arbitrary"),
        name="paged_attention",
    )(page_table.reshape(-1), *args)
    return outs if flash is not None else outs[0]


def _uv_kernel(ol_ref, wuv_ref, g_ref, o_ref):
    for h in range(MLA_HEADS):
        o = _dot(ol_ref[h].astype(BF16), wuv_ref[h])
        sl = slice(h * V_DIM, (h + 1) * V_DIM)
        o_ref[:, sl] = (o * _silu(g_ref[:, sl])).astype(o_ref.dtype)


def uv_project(o_lat, wuv3, proj):
    h, t, _ = o_lat.shape
    return pl.pallas_call(
        _uv_kernel,
        grid=(1,),
        in_specs=[
            _full((h, t, KV_LORA)), _full((h, KV_LORA, V_DIM)),
            pl.BlockSpec((t, MLA_WIDTH), lambda i: (0, COL_GM // MLA_WIDTH)),
        ],
        out_specs=_full((t, MLA_WIDTH)),
        out_shape=jax.ShapeDtypeStruct((t, MLA_WIDTH), BF16),
        compiler_params=_params("arbitrary"),
        name="uv_project",
    )(o_lat, wuv3, proj)


def _proj_norm_res_kernel(*refs, n_in):
    ins, ws = refs[:n_in], refs[n_in:2 * n_in]
    x_ref, g_ref, o_ref = refs[2 * n_in:]
    y = _dot(ins[0][...].astype(BF16), ws[0][...])
    for a, w in zip(ins[1:], ws[1:]):
        y = y + _dot(a[...].astype(BF16), w[...])
    o_ref[...] = x_ref[...] + _rms(y, g_ref[...])


def proj_norm_res(ins, ws, x, g, *, tm, name):
    t, d = x.shape
    n_in = len(ins)
    return pl.pallas_call(
        functools.partial(_proj_norm_res_kernel, n_in=n_in),
        grid=(t // tm,),
        in_specs=([pl.BlockSpec((tm, a.shape[1]), lambda i: (i, 0)) for a in ins]
                  + [_full(w.shape) for w in ws]
                  + [pl.BlockSpec((tm, d), lambda i: (i, 0)), _full((1, d))]),
        out_specs=pl.BlockSpec((tm, d), lambda i: (i, 0)),
        out_shape=jax.ShapeDtypeStruct((t, d), F32),
        compiler_params=_params("parallel"),
        name=name,
    )(*ins, *ws, x, g.reshape(1, d))


def _mix_cross_kernel(conv_ref, attn_ref, x_ref, wc_ref, wm_ref, gmix_ref,
                      gpre_ref, wq_ref, mk_ref, mv_ref, wo_ref, gpost_ref, o_ref):
    y = _dot(conv_ref[...].astype(BF16), wc_ref[...]) + _dot(attn_ref[...], wm_ref[...])
    x = x_ref[...] + _rms(y, gmix_ref[...])
    hn = _rms(x, gpre_ref[...]).astype(BF16)
    q = (_dot(hn, wq_ref[...]) * MEM_SCALE).astype(BF16)
    outs = []
    for h in range(MEM_HEADS):
        sl = slice(h * MEM_HEAD_DIM, (h + 1) * MEM_HEAD_DIM)
        s = _dot_nt(q[:, sl], mk_ref[:, sl].astype(BF16))
        m = jnp.max(s, axis=-1, keepdims=True)
        p = jnp.exp(s - m)
        l = jnp.sum(p, axis=-1, keepdims=True)
        outs.append((_dot(p.astype(BF16), mv_ref[:, sl].astype(BF16)) / l).astype(BF16))
    y2 = _dot(jnp.concatenate(outs, axis=-1), wo_ref[...])
    o_ref[...] = x + _rms(y2, gpost_ref[...])


def mix_cross_prompt(conv_out, attn, x, wc, wm, g_mix, g_pre, wq, mem_kv, wo, g_post, *, seq, tm):
    t, d = x.shape
    per_seq = seq // tm
    const = lambda shape: pl.BlockSpec(shape, lambda i: (0,) * len(shape), pipeline_mode=pl.Buffered(1))
    rows = lambda width: pl.BlockSpec((tm, width), lambda i: (i, 0))
    return pl.pallas_call(
        _mix_cross_kernel,
        grid=(t // tm,),
        in_specs=[
            rows(CONV_CH), rows(MLA_WIDTH), rows(d),
            const(wc.shape), const(wm.shape), const((1, d)),
            const((1, d)), const(wq.shape),
            pl.BlockSpec((N_MEM, MEM_WIDTH), lambda i: (i // per_seq, 0)),
            pl.BlockSpec((N_MEM, MEM_WIDTH), lambda i: (i // per_seq, 1)),
            const(wo.shape), const((1, d)),
        ],
        out_specs=rows(d),
        out_shape=jax.ShapeDtypeStruct((t, d), F32),
        compiler_params=_params("parallel"),
        name="mix_cross_prompt",
    )(conv_out, attn, x, wc, wm, g_mix.reshape(1, d), g_pre.reshape(1, d), wq, mem_kv, mem_kv, wo,
      g_post.reshape(1, d))


def _cross_sample_kernel(q_ref, mk_ref, mv_ref, o_ref):
    bt = q_ref.shape[0]
    q = (q_ref[...] * MEM_SCALE).astype(BF16)
    for h in range(MEM_HEADS):
        sl = slice(h * MEM_HEAD_DIM, (h + 1) * MEM_HEAD_DIM)
        rows = pl.ds(h, N_MEM, stride=MEM_HEADS)
        qh = jnp.broadcast_to(q[:, :, sl], (bt, 8, MEM_HEAD_DIM))
        kh = mk_ref[:, 0, rows, :].astype(BF16)
        vh = mv_ref[:, 0, rows, :].astype(BF16)
        s = jnp.einsum("bqe,bme->bqm", qh, kh, preferred_element_type=F32)
        m = jnp.max(s, axis=-1, keepdims=True)
        p = jnp.exp(s - m)
        l = jnp.sum(p, axis=-1, keepdims=True)
        o = jnp.einsum("bqm,bme->bqe", p.astype(BF16), vh, preferred_element_type=F32) / l
        o_ref[:, :, sl] = o[:, 0:1, :].astype(o_ref.dtype)


def cross_sample_core(q3, mem_k, mem_v, *, layer, bt):
    bsz = q3.shape[0]
    mem = lambda: pl.BlockSpec((bt, 1, N_MEM * MEM_HEADS, MEM_HEAD_DIM), lambda i: (i, layer, 0, 0))
    return pl.pallas_call(
        _cross_sample_kernel,
        grid=(bsz // bt,),
        in_specs=[pl.BlockSpec((bt, 1, MEM_WIDTH), lambda i: (i, 0, 0)), mem(), mem()],
        out_specs=pl.BlockSpec((bt, 1, MEM_WIDTH), lambda i: (i, 0, 0)),
        out_shape=jax.ShapeDtypeStruct((bsz, 1, MEM_WIDTH), BF16),
        compiler_params=_params("parallel"),
        name="cross_sample",
    )(q3, mem_k, mem_v)


def _rope_tables(pos):
    half = QK_ROPE // 2
    inv_freq = ROPE_BASE ** (-jnp.arange(half, dtype=F32) / half)
    ang = pos.astype(F32)[:, None] * inv_freq[None, :]
    cos, sin = jnp.cos(ang), jnp.sin(ang)
    z = jnp.zeros_like(cos)
    return (jnp.concatenate([cos, cos, z, z], axis=-1),
            jnp.concatenate([-sin, z, z, z], axis=-1),
            jnp.concatenate([z, sin, z, z], axis=-1))


def _prep_layer(l, w_uq, w_uk, w_uv, w_out, w_xq, w_mk, w_mv, w_xo):
    d = w_out.shape[2]
    uq = w_uq[l]
    wqn = uq[:, :, :QK_NOPE].reshape(Q_LORA, -1).astype(BF16)
    wqp = jnp.pad(uq[:, :, QK_NOPE:], ((0, 0), (0, 0), (0, LANE - QK_ROPE))).reshape(Q_LORA, -1).astype(BF16)
    return dict(
        wqn=wqn, wqp=wqp,
        wuk=w_uk[l].reshape(KV_LORA, -1).astype(BF16),
        wuv=w_uv[l].reshape(KV_LORA, -1).astype(BF16),
        wukt=w_uk[l].transpose(1, 2, 0).astype(BF16),
        wuv3=w_uv[l].transpose(1, 0, 2).astype(BF16),
        w_out_c=w_out[l][:CONV_CH].astype(BF16),
        w_out_m=w_out[l][CONV_CH:].astype(BF16),
        w_xq=w_xq[l].reshape(d, -1).astype(BF16),
        w_mkv=jnp.concatenate([w_mk[l].reshape(d, -1), w_mv[l].reshape(d, -1)], axis=1).astype(BF16),
        w_xo=w_xo[l].reshape(-1, d).astype(BF16),
    )


def kernel(x_prompt, x_sample, mem_prompt, cache_kv_latent, cache_k_rope, state_conv, cache_mem_k, cache_mem_v, page_table, norm_mix_pre, w_in, conv_w, conv_b, conv_ln_g, conv_ln_b, q_norm_g, w_uq, kv_norm_g, w_uk, w_uv, w_out, norm_mix_post, norm_x_pre, norm_mem, w_xq, w_mk, w_mv, w_xo, norm_x_post):
    b_p, s_p, d = x_prompt.shape
    b_s = x_sample.shape[0]
    depth = w_in.shape[0]
    n_pages = page_table.shape[1]
    t_p = b_p * s_p
    tm_p = min(512, s_p)

    tabs_p = _rope_tables(jnp.arange(s_p, dtype=jnp.int32))
    tabs_s = _rope_tables(jnp.full((b_s,), n_pages * PAGE_SIZE, jnp.int32))
    mem2 = mem_prompt.reshape(b_p * N_MEM, d)
    mem_k_s = cache_mem_k.reshape(b_s, depth, N_MEM * MEM_HEADS, MEM_HEAD_DIM)
    mem_v_s = cache_mem_v.reshape(b_s, depth, N_MEM * MEM_HEADS, MEM_HEAD_DIM)
    cache_krt = cache_k_rope.transpose(0, 1, 3, 2)
    state_t = state_conv.transpose(0, 2, 1, 3)
    conv_w_t = conv_w.transpose(1, 0, 2)
    new_state_t, conv_part = conv_state_shift(state_t, conv_w_t, bt=8)
    w_in_t = w_in_prep(w_in.transpose(0, 2, 1))

    xp = x_prompt.reshape(t_p, d)
    xs = x_sample.reshape(b_s, d)
    lat_p, kpe_p, conv_p, mk_p, mv_p, lat_s, kpe_s = [], [], [], [], [], [], []
    u_s = []
    for l in range(depth):
        w = _prep_layer(l, w_uq, w_uk, w_uv, w_out, w_xq, w_mk, w_mv, w_xo)

        proj = rms_matmul(xp, norm_mix_pre[l], w_in_t, tm=min(1024, s_p), tn=1024, name="in_proj_p",
                          w_transposed=True, layer=l)
        conv_out, conv_state = conv_prompt(proj, conv_w[l], conv_b[l], conv_ln_g[l], conv_ln_b[l],
                                           bsz=b_p, seq=s_p)
        q, k, v, c_kv, k_pe = mla_prompt(proj, tabs_p, q_norm_g[l], kv_norm_g[l],
                                         w["wqn"], w["wqp"], w["wuk"], w["wuv"], tm=tm_p)
        fuse = b_p * MLA_HEADS * (s_p // FLASH_TQ) == b_s
        if not fuse:
            attn = flash_prompt(q, k, v, proj, bsz=b_p, seq=s_p)

        proj_s = rms_matmul(xs, norm_mix_pre[l], w_in_t, tm=b_s, tn=1024, name="in_proj_s",
                            w_transposed=True, layer=l)
        conv_out_s, u_l = conv_sample(proj_s, conv_part, l, conv_w[l], conv_b[l], conv_ln_g[l], conv_ln_b[l])
        u_s.append(u_l)
        qa, qp, c_new, k_new = mla_sample(proj_s, tabs_s, q_norm_g[l], kv_norm_g[l],
                                          w["wqn"], w["wqp"], w["wukt"])
        paged = paged_attention(page_table, qa.transpose(1, 0, 2), qp.transpose(1, 0, 2),
                                c_new.reshape(b_s, 1, KV_LORA), k_new.reshape(b_s, 1, QK_ROPE),
                                cache_kv_latent, cache_krt, layer=l,
                                flash=(q, k, v, proj, b_p, s_p) if fuse else None)
        o_lat, attn = paged if fuse else (paged, attn)

        mem_kv = rms_matmul(mem2, norm_mem[l], w["w_mkv"], tm=N_MEM, tn=2 * MEM_WIDTH, name="mem_kv")
        xp = mix_cross_prompt(conv_out, attn, xp, w["w_out_c"], w["w_out_m"], norm_mix_post[l],
                              norm_x_pre[l], w["w_xq"], mem_kv, w["w_xo"], norm_x_post[l], seq=s_p, tm=tm_p)
        lat_p.append(c_kv.reshape(b_p, s_p, KV_LORA))
        kpe_p.append(k_pe.reshape(b_p, s_p, QK_ROPE))
        conv_p.append(conv_state)
        mk_p.append(mem_kv[:, :MEM_WIDTH].reshape(b_p, N_MEM, MEM_HEADS, MEM_HEAD_DIM))
        mv_p.append(mem_kv[:, MEM_WIDTH:].reshape(b_p, N_MEM, MEM_HEADS, MEM_HEAD_DIM))

        attn_s = uv_project(o_lat.transpose(1, 0, 2), w["wuv3"], proj_s)
        xs = proj_norm_res([conv_out_s, attn_s], [w["w_out_c"], w["w_out_m"]],
                           xs, norm_mix_post[l], tm=b_s, name="out_proj_s")
        q_x = rms_matmul(xs, norm_x_pre[l], w["w_xq"], tm=b_s, tn=MEM_WIDTH, name="cross_q_s")
        o_x = cross_sample_core(q_x.reshape(b_s, 1, MEM_WIDTH), mem_k_s, mem_v_s, layer=l, bt=8)
        xs = proj_norm_res([o_x.reshape(b_s, MEM_WIDTH)], [w["w_xo"]], xs, norm_x_post[l],
                           tm=b_s, name="cross_out_s")
        lat_s.append(c_new.reshape(b_s, 1, KV_LORA))
        kpe_s.append(k_new.reshape(b_s, 1, QK_ROPE))

    return (xp.reshape(b_p, s_p, d), xs.reshape(b_s, 1, d),
            jnp.stack(lat_p, axis=1), jnp.stack(kpe_p, axis=1), jnp.stack(conv_p, axis=1),
            jnp.stack(mk_p, axis=1), jnp.stack(mv_p, axis=1),
            jnp.stack(lat_s, axis=1), jnp.stack(kpe_s, axis=1),
            conv_state_finish(new_state_t, jnp.stack(u_s, axis=1)[:, None]).transpose(0, 2, 1, 3))
```

```python
import functools

import jax
import jax.numpy as jnp
from jax import lax
from jax.experimental import pallas as pl
from jax.experimental.pallas import tpu as pltpu

F32 = jnp.float32
BF16 = jnp.bfloat16

D_MODEL = 2048
CONV_CH = 1024
CONV_WIDTH = 31
CONV_STATE = CONV_WIDTH - 1
MLA_HEADS = 8
QK_NOPE = 128
QK_ROPE = 64
V_DIM = 128
MLA_WIDTH = MLA_HEADS * V_DIM
Q_LORA = 512
KV_LORA = 256
MLA_SCALE = (QK_NOPE + QK_ROPE) ** -0.5
ROPE_BASE = 10000.0
PAGE_SIZE = 128
N_MEM = 256
MEM_HEADS = 4
MEM_HEAD_DIM = 128
MEM_WIDTH = MEM_HEADS * MEM_HEAD_DIM
MEM_SCALE = MEM_HEAD_DIM ** -0.5
EPS = 1e-6
LOG2E = 1.4426950408889634

LANE = 128
QK_PAD = 256
NEG_BIG = -1e30

PROJ_COLS = 5120
COL_A, COL_B, COL_GC, COL_GM, COL_QL, COL_KV, COL_KPE = 0, 1024, 2048, 3072, 4096, 4608, 4864

VMEM_LIMIT = 52 * 1024 * 1024


def _params(*sem):
    return pltpu.CompilerParams(dimension_semantics=sem, vmem_limit_bytes=VMEM_LIMIT)


def _rms(x, g):
    return x * lax.rsqrt(jnp.mean(x * x, axis=-1, keepdims=True) + EPS) * g


def _silu(x):
    return x * jax.nn.sigmoid(x)


def _dot(a, b):
    return jnp.dot(a, b, preferred_element_type=F32)


def _dot_nt(a, b):
    return lax.dot_general(a, b, (((1,), (1,)), ((), ())), preferred_element_type=F32)


def _rope128(x, cos_t, sin_n, sin_p):
    return x * cos_t + pltpu.roll(x, 96, 1) * sin_n + pltpu.roll(x, 32, 1) * sin_p


def _rms_matmul_kernel(x_ref, g_ref, w_ref, o_ref, hn_ref, *, w_transposed):
    @pl.when(pl.program_id(1) == 0)
    def _():
        hn_ref[...] = _rms(x_ref[...], g_ref[...]).astype(BF16)

    dot = _dot_nt if w_transposed else _dot
    o_ref[...] = dot(hn_ref[...], w_ref[...]).astype(o_ref.dtype)


def rms_matmul(x, g, w, *, tm, tn, name, w_transposed=False, layer=None):
    t, k = x.shape
    n = w.shape[-2] if w_transposed else w.shape[-1]
    lead = () if layer is None else (None,)
    pick = () if layer is None else (layer,)
    w_spec = (pl.BlockSpec(lead + (tn, k), lambda i, j: pick + (j, 0)) if w_transposed
              else pl.BlockSpec(lead + (k, tn), lambda i, j: pick + (0, j)))
    return pl.pallas_call(
        functools.partial(_rms_matmul_kernel, w_transposed=w_transposed),
        grid=(t // tm, n // tn),
        in_specs=[
            pl.BlockSpec((tm, k), lambda i, j: (i, 0)),
            pl.BlockSpec((1, k), lambda i, j: (0, 0)),
            w_spec,
        ],
        out_specs=pl.BlockSpec((tm, tn), lambda i, j: (i, j)),
        out_shape=jax.ShapeDtypeStruct((t, n), F32),
        scratch_shapes=[pltpu.VMEM((tm, k), BF16)],
        compiler_params=_params("parallel", "arbitrary"),
        name=name,
    )(x, g.reshape(1, k), w)


_W_IN_GROUPS = (
    (COL_A, 0, 3 * CONV_CH),
    (COL_GM, 3 * CONV_CH + Q_LORA + KV_LORA + QK_ROPE, MLA_WIDTH),
    (COL_QL, 3 * CONV_CH, Q_LORA),
    (COL_KV, 3 * CONV_CH + Q_LORA, KV_LORA),
    (COL_KPE, 3 * CONV_CH + Q_LORA + KV_LORA, QK_ROPE),
)
W_PREP_TK = 512


def _w_in_prep_kernel(w_ref, o_ref):
    for dst, src, rows in _W_IN_GROUPS:
        o_ref[0, dst:dst + rows, :] = w_ref[0, src:src + rows, :].astype(BF16)
    pad0 = COL_KPE + QK_ROPE
    o_ref[0, pad0:, :] = jnp.zeros((PROJ_COLS - pad0, o_ref.shape[2]), BF16)


def w_in_prep(w_in_t):
    depth, n_in, k = w_in_t.shape
    return pl.pallas_call(
        _w_in_prep_kernel,
        grid=(depth, k // W_PREP_TK),
        in_specs=[pl.BlockSpec((1, n_in, W_PREP_TK), lambda l, j: (l, 0, j))],
        out_specs=pl.BlockSpec((1, PROJ_COLS, W_PREP_TK), lambda l, j: (l, 0, j)),
        out_shape=jax.ShapeDtypeStruct((depth, PROJ_COLS, k), BF16),
        compiler_params=_params("parallel", "parallel"),
        name="w_in_prep",
    )(w_in_t)


CONV_SEGS = 8
CONV_SEG_ROWS = 128
CONV_HALO = 32
CONV_BLOCK = 16
CONV_TAPS_A = 16
CONV_UNROLL = 16
CONV_PAD = 32
CONV_TILE = CONV_SEGS * CONV_SEG_ROWS


def _conv_prompt_kernel(proj_hbm, w_ref, cb_ref, lg_ref, lb_ref, out_hbm, state_ref,
                        xa, xb, xg, y_ref, o_ref, wb_ref, lgb_ref, in_sems, out_sem, *, tiles_per_seq):
    n = pl.program_id(0)
    n_tiles = pl.num_programs(0)
    slot = n % 2
    lrows, halo = CONV_SEG_ROWS, CONV_HALO

    def in_copies(tile, sl, seg, first):
        row = tile * CONV_TILE + seg * lrows
        if first:
            src_rows, dst_rows = pl.ds(row, lrows), pl.ds(halo, lrows)
        else:
            src_rows, dst_rows = pl.ds(row - halo, lrows + halo), pl.ds(0, lrows + halo)
        return (
            pltpu.make_async_copy(proj_hbm.at[src_rows, pl.ds(COL_A, CONV_CH)],
                                  xa.at[sl, dst_rows, seg, :], in_sems.at[sl, 0]),
            pltpu.make_async_copy(proj_hbm.at[src_rows, pl.ds(COL_B, CONV_CH)],
                                  xb.at[sl, dst_rows, seg, :], in_sems.at[sl, 1]),
            pltpu.make_async_copy(proj_hbm.at[pl.ds(row, lrows), pl.ds(COL_GC, CONV_CH)],
                                  xg.at[sl, :, seg, :], in_sems.at[sl, 2]),
        )

    def for_each_in_copy(tile, sl, fn):
        seq_start = tile % tiles_per_seq == 0

        @pl.when(seq_start)
        def _():
            for c in in_copies(tile, sl, 0, True):
                fn(c)

        @pl.when(jnp.logical_not(seq_start))
        def _():
            for c in in_copies(tile, sl, 0, False):
                fn(c)

        for seg in range(1, CONV_SEGS):
            for c in in_copies(tile, sl, seg, False):
                fn(c)

    def out_copies(tile):
        return [pltpu.make_async_copy(o_ref.at[:, seg, :],
                                      out_hbm.at[pl.ds(tile * CONV_TILE + seg * lrows, lrows), :], out_sem.at[0])
                for seg in range(CONV_SEGS)]

    @pl.when(n == 0)
    def _():
        for k in range(CONV_WIDTH):
            wb_ref[k] = jnp.broadcast_to(w_ref[k:k + 1, :], (CONV_SEGS, CONV_CH))
        lgb_ref[0] = jnp.broadcast_to(lg_ref[...], (CONV_SEGS, CONV_CH))
        lgb_ref[1] = jnp.broadcast_to(lb_ref[...], (CONV_SEGS, CONV_CH))
        for_each_in_copy(n, slot, lambda c: c.start())

    @pl.when(n + 1 < n_tiles)
    def _():
        for_each_in_copy(n + 1, 1 - slot, lambda c: c.start())

    for_each_in_copy(n, slot, lambda c: c.wait())

    def glu(i, carry):
        xa[slot, i] = xa[slot, i] * jax.nn.sigmoid(xb[slot, i])
        return carry

    lax.fori_loop(0, lrows + halo, glu, 0, unroll=4)

    @pl.when(n % tiles_per_seq == 0)
    def _():
        xa[slot, 0:halo, 0:1, :] = jnp.zeros((halo, 1, CONV_CH), F32)

    @pl.when(n % tiles_per_seq == tiles_per_seq - 1)
    def _():
        state_ref[0] = xa[slot, halo + lrows - CONV_STATE:halo + lrows, CONV_SEGS - 1, :]

    n_zero = CONV_PAD - (CONV_TAPS_A - 1)
    y_ref[0:n_zero] = jnp.zeros((n_zero, CONV_SEGS, CONV_CH), F32)
    shift = halo - CONV_STATE
    zero = jnp.zeros((CONV_SEGS, LANE), F32)
    for k0, n_taps in ((0, CONV_TAPS_A), (CONV_TAPS_A, CONV_WIDTH - CONV_TAPS_A)):
        first = k0 == 0
        n_iter = -(-(lrows + n_taps - 1) // CONV_UNROLL)
        j_start = min(shift + k0, lrows + halo - n_iter * CONV_UNROLL)
        for c0 in range(0, CONV_CH, LANE):
            cols = slice(c0, c0 + LANE)
            w = [wb_ref[k0 + t, :, cols] for t in range(n_taps)]
            fresh = (cb_ref[:, cols] + zero) if first else zero
            row_off = CONV_PAD - shift - k0 - (n_taps - 1)

            def body(it, carry, w=w, fresh=fresh, cols=cols, row_off=row_off, first=first, j_start=j_start):
                acc = list(carry)
                j0 = it * CONV_UNROLL + j_start
                for jj in range(CONV_UNROLL):
                    u = xa[slot, j0 + jj, :, cols]
                    acc = [a + wt * u for wt, a in zip(w, [fresh] + acc)]
                    done = acc.pop()
                    if first:
                        y_ref[j0 + (jj + row_off), :, cols] = done
                    else:
                        y_ref[j0 + (jj + row_off), :, cols] = y_ref[j0 + (jj + row_off), :, cols] + done
                return tuple(acc)

            lax.fori_loop(0, n_iter, body, tuple(zero for _ in range(n_taps - 1)))

    @pl.when(n > 0)
    def _():
        for c in out_copies(n - 1):
            c.wait()

    def norm_gate(r, carry):
        for ii in range(CONV_BLOCK):
            i = r * CONV_BLOCK + ii
            acc = y_ref[i + CONV_PAD]
            mu = jnp.mean(acc, axis=-1, keepdims=True)
            d = acc - mu
            var = jnp.mean(d * d, axis=-1, keepdims=True)
            z = d * lax.rsqrt(var + EPS) * lgb_ref[0] + lgb_ref[1]
            o_ref[i] = _silu(z) * _silu(xg[slot, i])
        return carry

    lax.fori_loop(0, lrows // CONV_BLOCK, norm_gate, 0)

    for c in out_copies(n):
        c.start()

    @pl.when(n == n_tiles - 1)
    def _():
        for c in out_copies(n):
            c.wait()


def conv_prompt(proj, conv_w, conv_b, ln_g, ln_b, *, bsz, seq):
    tiles_per_seq = seq // CONV_TILE
    vec = lambda: pl.BlockSpec((1, CONV_CH), lambda n: (0, 0))
    seg_buf = lambda rows: pltpu.VMEM((2, rows, CONV_SEGS, CONV_CH), F32)
    return pl.pallas_call(
        functools.partial(_conv_prompt_kernel, tiles_per_seq=tiles_per_seq),
        grid=(bsz * tiles_per_seq,),
        in_specs=[
            pl.BlockSpec(memory_space=pl.ANY),
            pl.BlockSpec((CONV_WIDTH, CONV_CH), lambda n: (0, 0)),
            vec(), vec(), vec(),
        ],
        out_specs=[
            pl.BlockSpec(memory_space=pl.ANY),
            pl.BlockSpec((1, CONV_STATE, CONV_CH), lambda n: (n // tiles_per_seq, 0, 0)),
        ],
        out_shape=[
            jax.ShapeDtypeStruct((bsz * seq, CONV_CH), F32),
            jax.ShapeDtypeStruct((bsz, CONV_STATE, CONV_CH), F32),
        ],
        scratch_shapes=[
            seg_buf(CONV_SEG_ROWS + CONV_HALO), seg_buf(CONV_SEG_ROWS + CONV_HALO), seg_buf(CONV_SEG_ROWS),
            pltpu.VMEM((CONV_SEG_ROWS + 2 * CONV_PAD, CONV_SEGS, CONV_CH), F32),
            pltpu.VMEM((CONV_SEG_ROWS, CONV_SEGS, CONV_CH), F32),
            pltpu.VMEM((CONV_WIDTH, CONV_SEGS, CONV_CH), F32),
            pltpu.VMEM((2, CONV_SEGS, CONV_CH), F32),
            pltpu.SemaphoreType.DMA((2, 3)), pltpu.SemaphoreType.DMA((1,)),
        ],
        compiler_params=_params("arbitrary"),
        name="conv_prompt",
    )(proj, conv_w, conv_b.reshape(1, -1), ln_g.reshape(1, -1), ln_b.reshape(1, -1))


def _conv_state_kernel(st_ref, w_ref, newst_ref, part_ref):
    st = st_ref[...]
    newst_ref[:, 0:CONV_STATE - 1] = st[:, 1:CONV_STATE]
    newst_ref[:, CONV_STATE - 1:CONV_STATE] = jnp.zeros_like(st[:, 0:1])
    part_ref[...] = jnp.sum(st * w_ref[0:CONV_STATE][None], axis=1)


def conv_state_shift(state_t, conv_w_t, *, bt):
    bsz, _, depth, _ = state_t.shape
    blk = pl.BlockSpec((bt, CONV_STATE, depth, CONV_CH), lambda i: (i, 0, 0, 0))
    return pl.pallas_call(
        _conv_state_kernel,
        grid=(bsz // bt,),
        in_specs=[blk, pl.BlockSpec((CONV_WIDTH, depth, CONV_CH), lambda i: (0, 0, 0))],
        out_specs=[blk, pl.BlockSpec((bt, depth, CONV_CH), lambda i: (i, 0, 0))],
        out_shape=[jax.ShapeDtypeStruct(state_t.shape, F32),
                   jax.ShapeDtypeStruct((bsz, depth, CONV_CH), F32)],
        compiler_params=_params("parallel"),
        name="conv_state_shift",
    )(state_t, conv_w_t)


def _conv_state_finish_kernel(u_ref, st_hbm, o_ref):
    del st_hbm
    o_ref[...] = u_ref[...]


def conv_state_finish(new_state, u_all):
    bsz, _, depth, _ = new_state.shape
    return pl.pallas_call(
        _conv_state_finish_kernel,
        grid=(1,),
        in_specs=[pl.BlockSpec((bsz, 1, depth, CONV_CH), lambda i: (0, 0, 0, 0)),
                  pl.BlockSpec(memory_space=pl.ANY)],
        out_specs=pl.BlockSpec((bsz, 1, depth, CONV_CH), lambda i: (0, CONV_STATE - 1, 0, 0)),
        out_shape=jax.ShapeDtypeStruct(new_state.shape, F32),
        input_output_aliases={1: 0},
        compiler_params=_params("arbitrary"),
        name="conv_state_finish",
    )(u_all, new_state)


def _conv_sample_kernel(a_ref, b_ref, gc_ref, part_ref, w_ref, cb_ref, lg_ref, lb_ref, out_ref, u_ref, *, layer):
    u = a_ref[...] * jax.nn.sigmoid(b_ref[...])
    u_ref[...] = u
    y = part_ref[:, layer, :] + u * w_ref[CONV_STATE:CONV_WIDTH, :] + cb_ref[...]
    mu = jnp.mean(y, axis=-1, keepdims=True)
    d = y - mu
    var = jnp.mean(d * d, axis=-1, keepdims=True)
    z = d * lax.rsqrt(var + EPS) * lg_ref[...] + lb_ref[...]
    out_ref[...] = (_silu(z) * _silu(gc_ref[...])).astype(out_ref.dtype)


def conv_sample(proj, part, layer, conv_w, conv_b, ln_g, ln_b):
    bsz = proj.shape[0]
    vec = lambda: pl.BlockSpec((1, CONV_CH), lambda i: (0, 0))
    row = lambda: pl.BlockSpec((bsz, CONV_CH), lambda i: (0, 0))
    return pl.pallas_call(
        functools.partial(_conv_sample_kernel, layer=layer),
        grid=(1,),
        in_specs=[
            pl.BlockSpec((bsz, CONV_CH), lambda i: (0, COL_A // CONV_CH)),
            pl.BlockSpec((bsz, CONV_CH), lambda i: (0, COL_B // CONV_CH)),
            pl.BlockSpec((bsz, CONV_CH), lambda i: (0, COL_GC // CONV_CH)),
            pl.BlockSpec(part.shape, lambda i: (0, 0, 0)),
            pl.BlockSpec((CONV_WIDTH, CONV_CH), lambda i: (0, 0)),
            vec(), vec(), vec(),
        ],
        out_specs=[row(), row()],
        out_shape=[
            jax.ShapeDtypeStruct((bsz, CONV_CH), BF16),
            jax.ShapeDtypeStruct((bsz, CONV_CH), F32),
        ],
        compiler_params=_params("arbitrary"),
        name="conv_sample",
    )(proj, proj, proj, part, conv_w, conv_b.reshape(1, -1), ln_g.reshape(1, -1), ln_b.reshape(1, -1))


def _mla_common(ql_ref, kvl_ref, kpe_ref, cos_ref, sn_ref, sp_ref, qg_ref, kvg_ref, wqn_ref, wqp_ref):
    qn = _rms(ql_ref[...], qg_ref[...]).astype(BF16)
    q_nope = _dot(qn, wqn_ref[...]) * (MLA_SCALE * LOG2E)
    q_rope_raw = _dot(qn, wqp_ref[...]) * (MLA_SCALE * LOG2E)
    cos_t, sin_n, sin_p = cos_ref[...], sn_ref[...], sp_ref[...]
    q_rope = [_rope128(q_rope_raw[:, h * LANE:(h + 1) * LANE], cos_t, sin_n, sin_p)
              for h in range(MLA_HEADS)]
    c_kv = _rms(kvl_ref[...], kvg_ref[...])
    k_pe = _rope128(kpe_ref[...], cos_t, sin_n, sin_p)
    return q_nope, q_rope, c_kv, k_pe


def _mla_prompt_kernel(ql_ref, kvl_ref, kpe_ref, cos_ref, sn_ref, sp_ref, qg_ref, kvg_ref,
                       wqn_ref, wqp_ref, wuk_ref, wuv_ref,
                       q_ref, k_ref, v_ref, ckv_ref, kpeo_ref):
    q_nope, q_rope, c_kv, k_pe = _mla_common(ql_ref, kvl_ref, kpe_ref, cos_ref, sn_ref, sp_ref,
                                             qg_ref, kvg_ref, wqn_ref, wqp_ref)
    ckv_ref[...] = c_kv
    kpeo_ref[...] = k_pe[:, :QK_ROPE]
    c_bf = c_kv.astype(BF16)
    k_nope = _dot(c_bf, wuk_ref[...])
    v = _dot(c_bf, wuv_ref[...])
    for h in range(MLA_HEADS):
        v_ref[h * V_DIM:(h + 1) * V_DIM, :] = v[:, h * V_DIM:(h + 1) * V_DIM].T.astype(BF16)
    k_pe_bf = k_pe.astype(BF16)
    for h in range(MLA_HEADS):
        lo = h * QK_PAD
        q_ref[:, lo:lo + LANE] = q_nope[:, h * LANE:(h + 1) * LANE].astype(BF16)
        q_ref[:, lo + LANE:lo + QK_PAD] = q_rope[h].astype(BF16)
        k_ref[:, lo:lo + LANE] = k_nope[:, h * LANE:(h + 1) * LANE].astype(BF16)
        k_ref[:, lo + LANE:lo + QK_PAD] = k_pe_bf


def _proj_specs(tm, n_tab):
    return [
        pl.BlockSpec((tm, Q_LORA), lambda i: (i, COL_QL // Q_LORA)),
        pl.BlockSpec((tm, KV_LORA), lambda i: (i, COL_KV // KV_LORA)),
        pl.BlockSpec((tm, LANE), lambda i: (i, COL_KPE // LANE)),
        pl.BlockSpec((tm, LANE), lambda i: (i % n_tab, 0)),
        pl.BlockSpec((tm, LANE), lambda i: (i % n_tab, 0)),
        pl.BlockSpec((tm, LANE), lambda i: (i % n_tab, 0)),
    ]


def _full(shape):
    return pl.BlockSpec(shape, lambda i: (0,) * len(shape))


def mla_prompt(proj, tabs, q_g, kv_g, wqn, wqp, wuk, wuv, *, tm):
    t = proj.shape[0]
    n_tab = tabs[0].shape[0] // tm
    h = MLA_HEADS
    return pl.pallas_call(
        _mla_prompt_kernel,
        grid=(t // tm,),
        in_specs=_proj_specs(tm, n_tab) + [
            _full((1, Q_LORA)), _full((1, KV_LORA)),
            _full((Q_LORA, h * LANE)), _full((Q_LORA, h * LANE)),
            _full((KV_LORA, h * LANE)), _full((KV_LORA, h * LANE)),
        ],
        out_specs=[
            pl.BlockSpec((tm, h * QK_PAD), lambda i: (i, 0)),
            pl.BlockSpec((tm, h * QK_PAD), lambda i: (i, 0)),
            pl.BlockSpec((h * V_DIM, tm), lambda i: (0, i)),
            pl.BlockSpec((tm, KV_LORA), lambda i: (i, 0)),
            pl.BlockSpec((tm, QK_ROPE), lambda i: (i, 0)),
        ],
        out_shape=[
            jax.ShapeDtypeStruct((t, h * QK_PAD), BF16),
            jax.ShapeDtypeStruct((t, h * QK_PAD), BF16),
            jax.ShapeDtypeStruct((h * V_DIM, t), BF16),
            jax.ShapeDtypeStruct((t, KV_LORA), F32),
            jax.ShapeDtypeStruct((t, QK_ROPE), F32),
        ],
        compiler_params=_params("parallel"),
        name="mla_prompt",
    )(proj, proj, proj, *tabs, q_g.reshape(1, -1), kv_g.reshape(1, -1), wqn, wqp, wuk, wuv)


def _mla_sample_kernel(ql_ref, kvl_ref, kpe_ref, cos_ref, sn_ref, sp_ref, qg_ref, kvg_ref,
                       wqn_ref, wqp_ref, wukt_ref,
                       qa_ref, qp_ref, ckv_ref, kpeo_ref):
    q_nope, q_rope, c_kv, k_pe = _mla_common(ql_ref, kvl_ref, kpe_ref, cos_ref, sn_ref, sp_ref,
                                             qg_ref, kvg_ref, wqn_ref, wqp_ref)
    ckv_ref[...] = c_kv
    kpeo_ref[...] = k_pe[:, :QK_ROPE]
    for h in range(MLA_HEADS):
        qa_ref[h] = _dot(q_nope[:, h * LANE:(h + 1) * LANE].astype(BF16), wukt_ref[h])
        qp_ref[h] = q_rope[h]


def mla_sample(proj, tabs, q_g, kv_g, wqn, wqp, wukt):
    t = proj.shape[0]
    h = MLA_HEADS
    return pl.pallas_call(
        _mla_sample_kernel,
        grid=(1,),
        in_specs=_proj_specs(t, 1) + [
            _full((1, Q_LORA)), _full((1, KV_LORA)),
            _full((Q_LORA, h * LANE)), _full((Q_LORA, h * LANE)),
            _full((h, QK_NOPE, KV_LORA)),
        ],
        out_specs=[
            _full((h, t, KV_LORA)), _full((h, t, LANE)),
            _full((t, KV_LORA)), _full((t, QK_ROPE)),
        ],
        out_shape=[
            jax.ShapeDtypeStruct((h, t, KV_LORA), F32),
            jax.ShapeDtypeStruct((h, t, LANE), F32),
            jax.ShapeDtypeStruct((t, KV_LORA), F32),
            jax.ShapeDtypeStruct((t, QK_ROPE), F32),
        ],
        compiler_params=_params("arbitrary"),
        name="mla_sample",
    )(proj, proj, proj, *tabs, q_g.reshape(1, -1), kv_g.reshape(1, -1), wqn, wqp, wukt)


FLASH_TQ = 512


def _flash_tile(q, k_ref, vt_ref, s_t, kv):
    tq = FLASH_TQ
    causal_t = (lax.broadcasted_iota(jnp.int32, (tq, tq), 0)
                <= lax.broadcasted_iota(jnp.int32, (tq, tq), 1))
    if s_t is None:
        s_t = _dot_nt(k_ref[0:kv + tq, :], q)
    s_d = jnp.where(causal_t, s_t[kv:kv + tq], NEG_BIG)
    m = jnp.max(s_d, axis=0, keepdims=True)
    if kv > 0:
        s_o = s_t[0:kv]
        m = jnp.maximum(m, jnp.max(s_o, axis=0, keepdims=True))
        p_o = jnp.exp2(s_o - m)
        l = jnp.sum(p_o, axis=0, keepdims=True)
        acc = _dot(vt_ref[:, 0:kv], p_o.astype(BF16))
    else:
        l = jnp.zeros((1, tq), F32)
        acc = jnp.zeros((V_DIM, tq), F32)
    p_d = jnp.exp2(s_d - m)
    l = l + jnp.sum(p_d, axis=0, keepdims=True)
    acc = acc + _dot(vt_ref[:, kv:kv + tq], p_d.astype(BF16))
    return (acc / l).T


def _flash_kernel(q_ref, k_ref, vt_ref, g_ref, o_ref, *, seq):
    tq = FLASH_TQ
    n_q = seq // tq

    def scores(i):
        kv = i * tq
        return _dot_nt(k_ref[0:kv + tq, :], q_ref[kv:kv + tq, :])

    s_next = scores(0)
    for i in range(n_q):
        kv = i * tq
        s_t = s_next
        if i + 1 < n_q:
            s_next = scores(i + 1)
        o = _flash_tile(None, k_ref, vt_ref, s_t, kv)
        o_ref[kv:kv + tq, :] = (o * _silu(g_ref[kv:kv + tq, :])).astype(o_ref.dtype)


def flash_prompt(q, k, v_t, proj, *, bsz, seq):
    h = MLA_HEADS
    return pl.pallas_call(
        functools.partial(_flash_kernel, seq=seq),
        grid=(bsz, h),
        in_specs=[
            pl.BlockSpec((seq, QK_PAD), lambda b, hh: (b, hh)),
            pl.BlockSpec((seq, QK_PAD), lambda b, hh: (b, hh)),
            pl.BlockSpec((V_DIM, seq), lambda b, hh: (hh, b)),
            pl.BlockSpec((seq, V_DIM), lambda b, hh: (b, COL_GM // V_DIM + hh)),
        ],
        out_specs=pl.BlockSpec((seq, V_DIM), lambda b, hh: (b, hh)),
        out_shape=jax.ShapeDtypeStruct((bsz * seq, h * V_DIM), BF16),
        compiler_params=_params("parallel", "parallel"),
        name="flash_prompt",
    )(q, k, v_t, proj)


PAGED_CHUNK = 1024
PAGED_GROUPS = 4


def _paged_kernel(pt_ref, qa_ref, qp_ref, cn_ref, kn_ref, ckv_hbm, krt_hbm, *rest, layer, n_pages, flash_tiles):
    if flash_tiles:
        fq_ref, fk_ref, fvt_ref, fg_ref, o_ref, fo_ref, cbuf, kbuf, cbf, s_ref, sems, fs_ref = rest
    else:
        o_ref, cbuf, kbuf, cbf, s_ref, sems = rest
    b = pl.program_id(0)
    nb = pl.num_programs(0)
    slot = b % 2

    def copies(bb, sl, p):
        page = pt_ref[bb * n_pages + p]
        toks = pl.ds(p * PAGE_SIZE, PAGE_SIZE)
        return (pltpu.make_async_copy(ckv_hbm.at[page, layer], cbuf.at[sl, toks, :], sems.at[0, sl]),
                pltpu.make_async_copy(krt_hbm.at[page, layer], kbuf.at[sl, :, toks], sems.at[1, sl]))

    def start_all(bb, sl):
        for p in range(n_pages):
            c1, c2 = copies(bb, sl, p)
            c1.start(priority=p % 2)
            c2.start(priority=(p + 1) % 2)

    @pl.when(b == 0)
    def _():
        start_all(b, slot)

    @pl.when(b + 1 < nb)
    def _():
        start_all(b + 1, 1 - slot)

    for c in range(flash_tiles):
        @pl.when(b % flash_tiles == c)
        def _(c=c):
            tq = FLASH_TQ
            kv = c * tq
            if c + 1 < flash_tiles:
                nxt = slice(kv + tq, kv + 2 * tq)
                fs_ref[(c + 1) % 2, 0:kv + 2 * tq, :] = _dot_nt(fk_ref[0:kv + 2 * tq, :], fq_ref[nxt, :])
            s_t = fs_ref[c % 2, 0:kv + tq, :] if c > 0 else None
            rows = slice(kv, kv + tq)
            o = _flash_tile(fq_ref[rows, :], fk_ref, fvt_ref, s_t, kv)
            fo_ref[rows, :] = (o * _silu(fg_ref[rows, :])).astype(fo_ref.dtype)

    for p in range(n_pages):
        c1, c2 = copies(b, slot, p)
        c1.wait()
        c2.wait()

    past = n_pages * PAGE_SIZE
    n_chunks = past // PAGED_CHUNK
    per_group = n_chunks // PAGED_GROUPS
    qa = qa_ref[0].astype(BF16)
    qp = qp_ref[0][:, :QK_ROPE].astype(BF16)

    def qk_chunk(c):
        toks = slice(c * PAGED_CHUNK, (c + 1) * PAGED_CHUNK)
        cc = cbuf[slot, toks, :].astype(BF16)
        kk = kbuf[slot, :, toks].astype(BF16)
        cbf[toks, :] = cc
        s_ref[:, toks] = _dot_nt(qa, cc) + _dot(qp, kk)

    def pv_chunk(c, m_g, l_g, acc_g):
        toks = slice(c * PAGED_CHUNK, (c + 1) * PAGED_CHUNK)
        p = jnp.exp2(s_ref[:, toks] - m_g)
        return l_g + jnp.sum(p, axis=-1, keepdims=True), acc_g + _dot(p.astype(BF16), cbf[toks, :])

    def group_max(g):
        g_toks = slice(g * per_group * PAGED_CHUNK, (g + 1) * per_group * PAGED_CHUNK)
        return jnp.max(s_ref[:, g_toks], axis=-1, keepdims=True)

    stats = []
    for c in range(per_group):
        qk_chunk(c)
    for g in range(PAGED_GROUPS):
        m_g = group_max(g)
        l_g = jnp.zeros((MLA_HEADS, 1), F32)
        acc_g = jnp.zeros((MLA_HEADS, KV_LORA), F32)
        for c in range(per_group):
            if g + 1 < PAGED_GROUPS:
                qk_chunk((g + 1) * per_group + c)
            l_g, acc_g = pv_chunk(g * per_group + c, m_g, l_g, acc_g)
        stats.append((m_g, l_g, acc_g))

    c_new = cn_ref[0].astype(BF16).astype(F32)
    k_new = kn_ref[0].astype(BF16).astype(F32)
    s_new = (jnp.sum(qa.astype(F32) * c_new, axis=-1, keepdims=True)
             + jnp.sum(qp.astype(F32) * k_new, axis=-1, keepdims=True))
    m = s_new
    for m_g, _, _ in stats:
        m = jnp.maximum(m, m_g)
    p_new = jnp.exp2(s_new - m)
    l = p_new
    acc = p_new.astype(BF16).astype(F32) * c_new
    for m_g, l_g, acc_g in stats:
        w_g = jnp.exp2(m_g - m)
        l = l + w_g * l_g
        acc = acc + w_g * acc_g
    o_ref[0] = acc / l


def paged_attention(page_table, qa, qp, c_new, k_new, cache_kv, cache_krt, *, layer, flash=None):
    bsz, n_pages = page_table.shape
    past = n_pages * PAGE_SIZE
    h = MLA_HEADS
    in_specs = [
        pl.BlockSpec((1, h, KV_LORA), lambda b, pt: (b, 0, 0)),
        pl.BlockSpec((1, h, LANE), lambda b, pt: (b, 0, 0)),
        pl.BlockSpec((1, 1, KV_LORA), lambda b, pt: (b, 0, 0)),
        pl.BlockSpec((1, 1, QK_ROPE), lambda b, pt: (b, 0, 0)),
        pl.BlockSpec(memory_space=pl.ANY),
        pl.BlockSpec(memory_space=pl.ANY),
    ]
    args = [qa, qp, c_new, k_new, cache_kv, cache_krt]
    out_specs = [pl.BlockSpec((1, h, KV_LORA), lambda b, pt: (b, 0, 0))]
    out_shape = [jax.ShapeDtypeStruct((bsz, h, KV_LORA), F32)]
    flash_tiles, flash_scratch = 0, []
    if flash is not None:
        fq, fk, fvt, proj, f_bsz, seq = flash
        flash_tiles = seq // FLASH_TQ
        assert f_bsz * h * flash_tiles == bsz
        sq = lambda b: b // flash_tiles // h
        head = lambda b: b // flash_tiles % h
        in_specs += [
            pl.BlockSpec((seq, QK_PAD), lambda b, pt: (sq(b), head(b))),
            pl.BlockSpec((seq, QK_PAD), lambda b, pt: (sq(b), head(b))),
            pl.BlockSpec((V_DIM, seq), lambda b, pt: (head(b), sq(b))),
            pl.BlockSpec((seq, V_DIM), lambda b, pt: (sq(b), COL_GM // V_DIM + head(b))),
        ]
        args += [fq, fk, fvt, proj]
        out_specs.append(pl.BlockSpec((seq, V_DIM), lambda b, pt: (sq(b), head(b))))
        out_shape.append(jax.ShapeDtypeStruct((f_bsz * seq, h * V_DIM), BF16))
        flash_scratch = [pltpu.VMEM((2, seq, FLASH_TQ), F32)]
    grid_spec = pltpu.PrefetchScalarGridSpec(
        num_scalar_prefetch=1,
        grid=(bsz,),
        in_specs=in_specs,
        out_specs=out_specs,
        scratch_shapes=[
            pltpu.VMEM((2, past, KV_LORA), F32),
            pltpu.VMEM((2, QK_ROPE, past), F32),
            pltpu.VMEM((past, KV_LORA), BF16),
            pltpu.VMEM((h, past), F32),
            pltpu.SemaphoreType.DMA((2, 2)),
        ] + flash_scratch,
    )
    outs = pl.pallas_call(
        functools.partial(_paged_kernel, layer=layer, n_pages=n_pages, flash_tiles=flash_tiles),
        grid_spec=grid_spec,
        out_shape=out_shape,
        compiler_params=_params("arbitrary"),
        name="paged_attention",
    )(page_table.reshape(-1), *args)
    return outs if flash is not None else outs[0]


def _uv_kernel(ol_ref, wuv_ref, g_ref, o_ref):
    for h in range(MLA_HEADS):
        o = _dot(ol_ref[h].astype(BF16), wuv_ref[h])
        sl = slice(h * V_DIM, (h + 1) * V_DIM)
        o_ref[:, sl] = (o * _silu(g_ref[:, sl])).astype(o_ref.dtype)


def uv_project(o_lat, wuv3, proj):
    h, t, _ = o_lat.shape
    return pl.pallas_call(
        _uv_kernel,
        grid=(1,),
        in_specs=[
            _full((h, t, KV_LORA)), _full((h, KV_LORA, V_DIM)),
            pl.BlockSpec((t, MLA_WIDTH), lambda i: (0, COL_GM // MLA_WIDTH)),
        ],
        out_specs=_full((t, MLA_WIDTH)),
        out_shape=jax.ShapeDtypeStruct((t, MLA_WIDTH), BF16),
        compiler_params=_params("arbitrary"),
        name="uv_project",
    )(o_lat, wuv3, proj)


def _proj_norm_res_kernel(*refs, n_in):
    ins, ws = refs[:n_in], refs[n_in:2 * n_in]
    x_ref, g_ref, o_ref = refs[2 * n_in:]
    y = _dot(ins[0][...].astype(BF16), ws[0][...])
    for a, w in zip(ins[1:], ws[1:]):
        y = y + _dot(a[...].astype(BF16), w[...])
    o_ref[...] = x_ref[...] + _rms(y, g_ref[...])


def proj_norm_res(ins, ws, x, g, *, tm, name):
    t, d = x.shape
    n_in = len(ins)
    return pl.pallas_call(
        functools.partial(_proj_norm_res_kernel, n_in=n_in),
        grid=(t // tm,),
        in_specs=([pl.BlockSpec((tm, a.shape[1]), lambda i: (i, 0)) for a in ins]
                  + [_full(w.shape) for w in ws]
                  + [pl.BlockSpec((tm, d), lambda i: (i, 0)), _full((1, d))]),
        out_specs=pl.BlockSpec((tm, d), lambda i: (i, 0)),
        out_shape=jax.ShapeDtypeStruct((t, d), F32),
        compiler_params=_params("parallel"),
        name=name,
    )(*ins, *ws, x, g.reshape(1, d))


def _mix_cross_kernel(conv_ref, attn_ref, x_ref, wc_ref, wm_ref, gmix_ref,
                      gpre_ref, wq_ref, mk_ref, mv_ref, wo_ref, gpost_ref, o_ref):
    y = _dot(conv_ref[...].astype(BF16), wc_ref[...]) + _dot(attn_ref[...], wm_ref[...])
    x = x_ref[...] + _rms(y, gmix_ref[...])
    hn = _rms(x, gpre_ref[...]).astype(BF16)
    q = (_dot(hn, wq_ref[...]) * MEM_SCALE).astype(BF16)
    outs = []
    for h in range(MEM_HEADS):
        sl = slice(h * MEM_HEAD_DIM, (h + 1) * MEM_HEAD_DIM)
        s = _dot_nt(q[:, sl], mk_ref[:, sl].astype(BF16))
        m = jnp.max(s, axis=-1, keepdims=True)
        p = jnp.exp(s - m)
        l = jnp.sum(p, axis=-1, keepdims=True)
        outs.append((_dot(p.astype(BF16), mv_ref[:, sl].astype(BF16)) / l).astype(BF16))
    y2 = _dot(jnp.concatenate(outs, axis=-1), wo_ref[...])
    o_ref[...] = x + _rms(y2, gpost_ref[...])


def mix_cross_prompt(conv_out, attn, x, wc, wm, g_mix, g_pre, wq, mem_kv, wo, g_post, *, seq, tm):
    t, d = x.shape
    per_seq = seq // tm
    const = lambda shape: pl.BlockSpec(shape, lambda i: (0,) * len(shape), pipeline_mode=pl.Buffered(1))
    rows = lambda width: pl.BlockSpec((tm, width), lambda i: (i, 0))
    return pl.pallas_call(
        _mix_cross_kernel,
        grid=(t // tm,),
        in_specs=[
            rows(CONV_CH), rows(MLA_WIDTH), rows(d),
            const(wc.shape), const(wm.shape), const((1, d)),
            const((1, d)), const(wq.shape),
            pl.BlockSpec((N_MEM, MEM_WIDTH), lambda i: (i // per_seq, 0)),
            pl.BlockSpec((N_MEM, MEM_WIDTH), lambda i: (i // per_seq, 1)),
            const(wo.shape), const((1, d)),
        ],
        out_specs=rows(d),
        out_shape=jax.ShapeDtypeStruct((t, d), F32),
        compiler_params=_params("parallel"),
        name="mix_cross_prompt",
    )(conv_out, attn, x, wc, wm, g_mix.reshape(1, d), g_pre.reshape(1, d), wq, mem_kv, mem_kv, wo,
      g_post.reshape(1, d))


def _cross_sample_kernel(q_ref, mk_ref, mv_ref, o_ref):
    bt = q_ref.shape[0]
    q = (q_ref[...] * MEM_SCALE).astype(BF16)
    for h in range(MEM_HEADS):
        sl = slice(h * MEM_HEAD_DIM, (h + 1) * MEM_HEAD_DIM)
        rows = pl.ds(h, N_MEM, stride=MEM_HEADS)
        qh = jnp.broadcast_to(q[:, :, sl], (bt, 8, MEM_HEAD_DIM))
        kh = mk_ref[:, 0, rows, :].astype(BF16)
        vh = mv_ref[:, 0, rows, :].astype(BF16)
        s = jnp.einsum("bqe,bme->bqm", qh, kh, preferred_element_type=F32)
        m = jnp.max(s, axis=-1, keepdims=True)
        p = jnp.exp(s - m)
        l = jnp.sum(p, axis=-1, keepdims=True)
        o = jnp.einsum("bqm,bme->bqe", p.astype(BF16), vh, preferred_element_type=F32) / l
        o_ref[:, :, sl] = o[:, 0:1, :].astype(o_ref.dtype)


def cross_sample_core(q3, mem_k, mem_v, *, layer, bt):
    bsz = q3.shape[0]
    mem = lambda: pl.BlockSpec((bt, 1, N_MEM * MEM_HEADS, MEM_HEAD_DIM), lambda i: (i, layer, 0, 0))
    return pl.pallas_call(
        _cross_sample_kernel,
        grid=(bsz // bt,),
        in_specs=[pl.BlockSpec((bt, 1, MEM_WIDTH), lambda i: (i, 0, 0)), mem(), mem()],
        out_specs=pl.BlockSpec((bt, 1, MEM_WIDTH), lambda i: (i, 0, 0)),
        out_shape=jax.ShapeDtypeStruct((bsz, 1, MEM_WIDTH), BF16),
        compiler_params=_params("parallel"),
        name="cross_sample",
    )(q3, mem_k, mem_v)


def _rope_tables(pos):
    half = QK_ROPE // 2
    inv_freq = ROPE_BASE ** (-jnp.arange(half, dtype=F32) / half)
    ang = pos.astype(F32)[:, None] * inv_freq[None, :]
    cos, sin = jnp.cos(ang), jnp.sin(ang)
    z = jnp.zeros_like(cos)
    return (jnp.concatenate([cos, cos, z, z], axis=-1),
            jnp.concatenate([-sin, z, z, z], axis=-1),
            jnp.concatenate([z, sin, z, z], axis=-1))


def _prep_layer(l, w_uq, w_uk, w_uv, w_out, w_xq, w_mk, w_mv, w_xo):
    d = w_out.shape[2]
    uq = w_uq[l]
    wqn = uq[:, :, :QK_NOPE].reshape(Q_LORA, -1).astype(BF16)
    wqp = jnp.pad(uq[:, :, QK_NOPE:], ((0, 0), (0, 0), (0, LANE - QK_ROPE))).reshape(Q_LORA, -1).astype(BF16)
    return dict(
        wqn=wqn, wqp=wqp,
        wuk=w_uk[l].reshape(KV_LORA, -1).astype(BF16),
        wuv=w_uv[l].reshape(KV_LORA, -1).astype(BF16),
        wukt=w_uk[l].transpose(1, 2, 0).astype(BF16),
        wuv3=w_uv[l].transpose(1, 0, 2).astype(BF16),
        w_out_c=w_out[l][:CONV_CH].astype(BF16),
        w_out_m=w_out[l][CONV_CH:].astype(BF16),
        w_xq=w_xq[l].reshape(d, -1).astype(BF16),
        w_mkv=jnp.concatenate([w_mk[l].reshape(d, -1), w_mv[l].reshape(d, -1)], axis=1).astype(BF16),
        w_xo=w_xo[l].reshape(-1, d).astype(BF16),
    )


def kernel(x_prompt, x_sample, mem_prompt, cache_kv_latent, cache_k_rope, state_conv, cache_mem_k, cache_mem_v, page_table, norm_mix_pre, w_in, conv_w, conv_b, conv_ln_g, conv_ln_b, q_norm_g, w_uq, kv_norm_g, w_uk, w_uv, w_out, norm_mix_post, norm_x_pre, norm_mem, w_xq, w_mk, w_mv, w_xo, norm_x_post):
    b_p, s_p, d = x_prompt.shape
    b_s = x_sample.shape[0]
    depth = w_in.shape[0]
    n_pages = page_table.shape[1]
    t_p = b_p * s_p
    tm_p = min(512, s_p)

    tabs_p = _rope_tables(jnp.arange(s_p, dtype=jnp.int32))
    tabs_s = _rope_tables(jnp.full((b_s,), n_pages * PAGE_SIZE, jnp.int32))
    mem2 = mem_prompt.reshape(b_p * N_MEM, d)
    mem_k_s = cache_mem_k.reshape(b_s, depth, N_MEM * MEM_HEADS, MEM_HEAD_DIM)
    mem_v_s = cache_mem_v.reshape(b_s, depth, N_MEM * MEM_HEADS, MEM_HEAD_DIM)
    cache_krt = cache_k_rope.transpose(0, 1, 3, 2)
    state_t = state_conv.transpose(0, 2, 1, 3)
    conv_w_t = conv_w.transpose(1, 0, 2)
    new_state_t, conv_part = conv_state_shift(state_t, conv_w_t, bt=8)
    w_in_t = w_in_prep(w_in.transpose(0, 2, 1))

    xp = x_prompt.reshape(t_p, d)
    xs = x_sample.reshape(b_s, d)
    lat_p, kpe_p, conv_p, mk_p, mv_p, lat_s, kpe_s = [], [], [], [], [], [], []
    u_s = []
    for l in range(depth):
        w = _prep_layer(l, w_uq, w_uk, w_uv, w_out, w_xq, w_mk, w_mv, w_xo)

        proj = rms_matmul(xp, norm_mix_pre[l], w_in_t, tm=min(1024, s_p), tn=1024, name="in_proj_p",
                          w_transposed=True, layer=l)
        conv_out, conv_state = conv_prompt(proj, conv_w[l], conv_b[l], conv_ln_g[l], conv_ln_b[l],
                                           bsz=b_p, seq=s_p)
        q, k, v, c_kv, k_pe = mla_prompt(proj, tabs_p, q_norm_g[l], kv_norm_g[l],
                                         w["wqn"], w["wqp"], w["wuk"], w["wuv"], tm=tm_p)
        fuse = b_p * MLA_HEADS * (s_p // FLASH_TQ) == b_s
        if not fuse:
            attn = flash_prompt(q, k, v, proj, bsz=b_p, seq=s_p)

        proj_s = rms_matmul(xs, norm_mix_pre[l], w_in_t, tm=b_s, tn=1024, name="in_proj_s",
                            w_transposed=True, layer=l)
        conv_out_s, u_l = conv_sample(proj_s, conv_part, l, conv_w[l], conv_b[l], conv_ln_g[l], conv_ln_b[l])
        u_s.append(u_l)
        qa, qp, c_new, k_new = mla_sample(proj_s, tabs_s, q_norm_g[l], kv_norm_g[l],
                                          w["wqn"], w["wqp"], w["wukt"])
        paged = paged_attention(page_table, qa.transpose(1, 0, 2), qp.transpose(1, 0, 2),
                                c_new.reshape(b_s, 1, KV_LORA), k_new.reshape(b_s, 1, QK_ROPE),
                                cache_kv_latent, cache_krt, layer=l,
                                flash=(q, k, v, proj, b_p, s_p) if fuse else None)
        o_lat, attn = paged if fuse else (paged, attn)

        mem_kv = rms_matmul(mem2, norm_mem[l], w["w_mkv"], tm=N_MEM, tn=2 * MEM_WIDTH, name="mem_kv")
        xp = mix_cross_prompt(conv_out, attn, xp, w["w_out_c"], w["w_out_m"], norm_mix_post[l],
                              norm_x_pre[l], w["w_xq"], mem_kv, w["w_xo"], norm_x_post[l], seq=s_p, tm=tm_p)
        lat_p.append(c_kv.reshape(b_p, s_p, KV_LORA))
        kpe_p.append(k_pe.reshape(b_p, s_p, QK_ROPE))
        conv_p.append(conv_state)
        mk_p.append(mem_kv[:, :MEM_WIDTH].reshape(b_p, N_MEM, MEM_HEADS, MEM_HEAD_DIM))
        mv_p.append(mem_kv[:, MEM_WIDTH:].reshape(b_p, N_MEM, MEM_HEADS, MEM_HEAD_DIM))

        attn_s = uv_project(o_lat.transpose(1, 0, 2), w["wuv3"], proj_s)
        xs = proj_norm_res([conv_out_s, attn_s], [w["w_out_c"], w["w_out_m"]],
                           xs, norm_mix_post[l], tm=b_s, name="out_proj_s")
        q_x = rms_matmul(xs, norm_x_pre[l], w["w_xq"], tm=b_s, tn=MEM_WIDTH, name="cross_q_s")
        o_x = cross_sample_core(q_x.reshape(b_s, 1, MEM_WIDTH), mem_k_s, mem_v_s, layer=l, bt=8)
        xs = proj_norm_res([o_x.reshape(b_s, MEM_WIDTH)], [w["w_xo"]], xs, norm_x_post[l],
                           tm=b_s, name="cross_out_s")
        lat_s.append(c_new.reshape(b_s, 1, KV_LORA))
        kpe_s.append(k_new.reshape(b_s, 1, QK_ROPE))

    return (xp.reshape(b_p, s_p, d), xs.reshape(b_s, 1, d),
            jnp.stack(lat_p, axis=1), jnp.stack(kpe_p, axis=1), jnp.stack(conv_p, axis=1),
            jnp.stack(mk_p, axis=1), jnp.stack(mv_p, axis=1),
            jnp.stack(lat_s, axis=1), jnp.stack(kpe_s, axis=1),
            conv_state_finish(new_state_t, jnp.stack(u_s, axis=1)[:, None]).transpose(0, 2, 1, 3))
```

```python
import functools

import jax
import jax.numpy as jnp
from jax import lax
from jax.experimental import pallas as pl
from jax.experimental.pallas import tpu as pltpu

F32 = jnp.float32
BF16 = jnp.bfloat16

CONV_CH = 1024
CONV_WIDTH = 31
CONV_STATE = CONV_WIDTH - 1
MLA_HEADS = 8
QK_NOPE = 128
QK_ROPE = 64
V_DIM = 128
MLA_WIDTH = MLA_HEADS * V_DIM
Q_LORA = 512
KV_LORA = 256
MLA_SCALE = (QK_NOPE + QK_ROPE) ** -0.5
ROPE_BASE = 10000.0
PAGE_SIZE = 128
N_MEM = 256
MEM_HEADS = 4
MEM_HEAD_DIM = 128
MEM_WIDTH = MEM_HEADS * MEM_HEAD_DIM
MEM_SCALE = MEM_HEAD_DIM ** -0.5
EPS = 1e-6
LOG2E = 1.4426950408889634

LANE = 128
QK_PAD = 256
NEG_BIG = -1e30

PROJ_COLS = 5120
COL_A, COL_B, COL_GC, COL_GM, COL_QL, COL_KV, COL_KPE = 0, 1024, 2048, 3072, 4096, 4608, 4864

VMEM_LIMIT = 52 * 1024 * 1024

ROW_TILE = 512
IN_PROJ_ROWS = 1024
IN_PROJ_COLS = 1024
STATE_BATCH_TILE = 32
CROSS_BATCH_TILE = 8


def _params(*sem):
    return pltpu.CompilerParams(dimension_semantics=sem, vmem_limit_bytes=VMEM_LIMIT)


def _rms(x, g):
    return x * lax.rsqrt(jnp.mean(x * x, axis=-1, keepdims=True) + EPS) * g


def _silu(x):
    return x * jax.nn.sigmoid(x)


def _dot(a, b):
    return jnp.dot(a, b, preferred_element_type=F32)


def _dot_nt(a, b):
    return lax.dot_general(a, b, (((1,), (1,)), ((), ())), preferred_element_type=F32)


def _rope128(x, cos_t, sin_n, sin_p):
    return x * cos_t + pltpu.roll(x, 96, 1) * sin_n + pltpu.roll(x, 32, 1) * sin_p


def _rms_matmul_kernel(x_ref, g_ref, w_ref, o_ref, hn_ref, *, w_transposed):
    @pl.when(pl.program_id(1) == 0)
    def _():
        hn_ref[...] = _rms(x_ref[...], g_ref[...]).astype(BF16)

    dot = _dot_nt if w_transposed else _dot
    o_ref[...] = dot(hn_ref[...], w_ref[...]).astype(o_ref.dtype)


def rms_matmul(x, g, w, *, tm, tn, name, w_transposed=False, layer=None):
    t, k = x.shape
    n = w.shape[-2] if w_transposed else w.shape[-1]
    lead = () if layer is None else (None,)
    pick = () if layer is None else (layer,)
    w_spec = (pl.BlockSpec(lead + (tn, k), lambda i, j: pick + (j, 0)) if w_transposed
              else pl.BlockSpec(lead + (k, tn), lambda i, j: pick + (0, j)))
    return pl.pallas_call(
        functools.partial(_rms_matmul_kernel, w_transposed=w_transposed),
        grid=(t // tm, n // tn),
        in_specs=[
            pl.BlockSpec((tm, k), lambda i, j: (i, 0)),
            pl.BlockSpec((1, k), lambda i, j: (0, 0)),
            w_spec,
        ],
        out_specs=pl.BlockSpec((tm, tn), lambda i, j: (i, j)),
        out_shape=jax.ShapeDtypeStruct((t, n), F32),
        scratch_shapes=[pltpu.VMEM((tm, k), BF16)],
        compiler_params=_params("parallel", "arbitrary"),
        name=name,
    )(x, g.reshape(1, k), w)


_W_IN_GROUPS = (
    (COL_A, 0, 3 * CONV_CH),
    (COL_GM, 3 * CONV_CH + Q_LORA + KV_LORA + QK_ROPE, MLA_WIDTH),
    (COL_QL, 3 * CONV_CH, Q_LORA),
    (COL_KV, 3 * CONV_CH + Q_LORA, KV_LORA),
    (COL_KPE, 3 * CONV_CH + Q_LORA + KV_LORA, QK_ROPE),
)
W_PREP_TK = 512


def _w_in_prep_kernel(w_ref, o_ref):
    for dst, src, rows in _W_IN_GROUPS:
        o_ref[0, dst:dst + rows, :] = w_ref[0, src:src + rows, :].astype(BF16)
    pad0 = COL_KPE + QK_ROPE
    o_ref[0, pad0:, :] = jnp.zeros((PROJ_COLS - pad0, o_ref.shape[2]), BF16)


def w_in_prep(w_in_t):
    depth, n_in, k = w_in_t.shape
    return pl.pallas_call(
        _w_in_prep_kernel,
        grid=(depth, k // W_PREP_TK),
        in_specs=[pl.BlockSpec((1, n_in, W_PREP_TK), lambda l, j: (l, 0, j))],
        out_specs=pl.BlockSpec((1, PROJ_COLS, W_PREP_TK), lambda l, j: (l, 0, j)),
        out_shape=jax.ShapeDtypeStruct((depth, PROJ_COLS, k), BF16),
        compiler_params=_params("parallel", "parallel"),
        name="w_in_prep",
    )(w_in_t)


CONV_SEGS = 8
CONV_SEG_ROWS = 128
CONV_HALO = 32
CONV_BLOCK = 16
CONV_TAPS_A = 16
CONV_UNROLL = 16
CONV_PAD = 32
CONV_TILE = CONV_SEGS * CONV_SEG_ROWS


def _conv_prompt_kernel(proj_hbm, w_ref, cb_ref, lg_ref, lb_ref, out_hbm, state_ref,
                        xa, xb, xg, y_ref, o_ref, wb_ref, lgb_ref, in_sems, out_sem, *, tiles_per_seq):
    n = pl.program_id(0)
    n_tiles = pl.num_programs(0)
    slot = n % 2
    lrows, halo = CONV_SEG_ROWS, CONV_HALO

    def in_copies(tile, sl, seg, first):
        row = tile * CONV_TILE + seg * lrows
        if first:
            src_rows, dst_rows = pl.ds(row, lrows), pl.ds(halo, lrows)
        else:
            src_rows, dst_rows = pl.ds(row - halo, lrows + halo), pl.ds(0, lrows + halo)
        return (
            pltpu.make_async_copy(proj_hbm.at[src_rows, pl.ds(COL_A, CONV_CH)],
                                  xa.at[sl, dst_rows, seg, :], in_sems.at[sl, 0]),
            pltpu.make_async_copy(proj_hbm.at[src_rows, pl.ds(COL_B, CONV_CH)],
                                  xb.at[sl, dst_rows, seg, :], in_sems.at[sl, 1]),
            pltpu.make_async_copy(proj_hbm.at[pl.ds(row, lrows), pl.ds(COL_GC, CONV_CH)],
                                  xg.at[sl, :, seg, :], in_sems.at[sl, 2]),
        )

    def for_each_in_copy(tile, sl, fn):
        seq_start = tile % tiles_per_seq == 0

        @pl.when(seq_start)
        def _():
            for c in in_copies(tile, sl, 0, True):
                fn(c)

        @pl.when(jnp.logical_not(seq_start))
        def _():
            for c in in_copies(tile, sl, 0, False):
                fn(c)

        for seg in range(1, CONV_SEGS):
            for c in in_copies(tile, sl, seg, False):
                fn(c)

    def out_copies(tile):
        return [pltpu.make_async_copy(o_ref.at[:, seg, :],
                                      out_hbm.at[pl.ds(tile * CONV_TILE + seg * lrows, lrows), :], out_sem.at[0])
                for seg in range(CONV_SEGS)]

    @pl.when(n == 0)
    def _():
        for k in range(CONV_WIDTH):
            wb_ref[k] = jnp.broadcast_to(w_ref[k:k + 1, :], (CONV_SEGS, CONV_CH))
        lgb_ref[0] = jnp.broadcast_to(lg_ref[...], (CONV_SEGS, CONV_CH))
        lgb_ref[1] = jnp.broadcast_to(lb_ref[...], (CONV_SEGS, CONV_CH))
        for_each_in_copy(n, slot, lambda c: c.start())

    @pl.when(n + 1 < n_tiles)
    def _():
        for_each_in_copy(n + 1, 1 - slot, lambda c: c.start())

    for_each_in_copy(n, slot, lambda c: c.wait())

    def glu(i, carry):
        xa[slot, i] = xa[slot, i] * jax.nn.sigmoid(xb[slot, i])
        return carry

    lax.fori_loop(0, lrows + halo, glu, 0, unroll=4)

    @pl.when(n % tiles_per_seq == 0)
    def _():
        xa[slot, 0:halo, 0:1, :] = jnp.zeros((halo, 1, CONV_CH), F32)

    @pl.when(n % tiles_per_seq == tiles_per_seq - 1)
    def _():
        state_ref[0] = xa[slot, halo + lrows - CONV_STATE:halo + lrows, CONV_SEGS - 1, :]

    n_zero = CONV_PAD - (CONV_TAPS_A - 1)
    y_ref[0:n_zero] = jnp.zeros((n_zero, CONV_SEGS, CONV_CH), F32)
    shift = halo - CONV_STATE
    zero = jnp.zeros((CONV_SEGS, LANE), F32)
    for k0, n_taps in ((0, CONV_TAPS_A), (CONV_TAPS_A, CONV_WIDTH - CONV_TAPS_A)):
        first = k0 == 0
        n_iter = -(-(lrows + n_taps - 1) // CONV_UNROLL)
        j_start = min(shift + k0, lrows + halo - n_iter * CONV_UNROLL)
        for c0 in range(0, CONV_CH, LANE):
            cols = slice(c0, c0 + LANE)
            w = [wb_ref[k0 + t, :, cols] for t in range(n_taps)]
            fresh = (cb_ref[:, cols] + zero) if first else zero
            row_off = CONV_PAD - shift - k0 - (n_taps - 1)

            def body(it, carry, w=w, fresh=fresh, cols=cols, row_off=row_off, first=first, j_start=j_start):
                acc = list(carry)
                j0 = it * CONV_UNROLL + j_start
                for jj in range(CONV_UNROLL):
                    u = xa[slot, j0 + jj, :, cols]
                    acc = [a + wt * u for wt, a in zip(w, [fresh] + acc)]
                    done = acc.pop()
                    if first:
                        y_ref[j0 + (jj + row_off), :, cols] = done
                    else:
                        y_ref[j0 + (jj + row_off), :, cols] = y_ref[j0 + (jj + row_off), :, cols] + done
                return tuple(acc)

            lax.fori_loop(0, n_iter, body, tuple(zero for _ in range(n_taps - 1)))

    @pl.when(n > 0)
    def _():
        for c in out_copies(n - 1):
            c.wait()

    def norm_gate(r, carry):
        for ii in range(CONV_BLOCK):
            i = r * CONV_BLOCK + ii
            acc = y_ref[i + CONV_PAD]
            mu = jnp.mean(acc, axis=-1, keepdims=True)
            d = acc - mu
            var = jnp.mean(d * d, axis=-1, keepdims=True)
            z = d * lax.rsqrt(var + EPS) * lgb_ref[0] + lgb_ref[1]
            o_ref[i] = _silu(z) * _silu(xg[slot, i])
        return carry

    lax.fori_loop(0, lrows // CONV_BLOCK, norm_gate, 0)

    for c in out_copies(n):
        c.start()

    @pl.when(n == n_tiles - 1)
    def _():
        for c in out_copies(n):
            c.wait()


def conv_prompt(proj, conv_w, conv_b, ln_g, ln_b, *, bsz, seq):
    tiles_per_seq = seq // CONV_TILE
    vec = lambda: pl.BlockSpec((1, CONV_CH), lambda n: (0, 0))
    seg_buf = lambda rows: pltpu.VMEM((2, rows, CONV_SEGS, CONV_CH), F32)
    return pl.pallas_call(
        functools.partial(_conv_prompt_kernel, tiles_per_seq=tiles_per_seq),
        grid=(bsz * tiles_per_seq,),
        in_specs=[
            pl.BlockSpec(memory_space=pl.ANY),
            pl.BlockSpec((CONV_WIDTH, CONV_CH), lambda n: (0, 0)),
            vec(), vec(), vec(),
        ],
        out_specs=[
            pl.BlockSpec(memory_space=pl.ANY),
            pl.BlockSpec((1, CONV_STATE, CONV_CH), lambda n: (n // tiles_per_seq, 0, 0)),
        ],
        out_shape=[
            jax.ShapeDtypeStruct((bsz * seq, CONV_CH), F32),
            jax.ShapeDtypeStruct((bsz, CONV_STATE, CONV_CH), F32),
        ],
        scratch_shapes=[
            seg_buf(CONV_SEG_ROWS + CONV_HALO), seg_buf(CONV_SEG_ROWS + CONV_HALO), seg_buf(CONV_SEG_ROWS),
            pltpu.VMEM((CONV_SEG_ROWS + 2 * CONV_PAD, CONV_SEGS, CONV_CH), F32),
            pltpu.VMEM((CONV_SEG_ROWS, CONV_SEGS, CONV_CH), F32),
            pltpu.VMEM((CONV_WIDTH, CONV_SEGS, CONV_CH), F32),
            pltpu.VMEM((2, CONV_SEGS, CONV_CH), F32),
            pltpu.SemaphoreType.DMA((2, 3)), pltpu.SemaphoreType.DMA((1,)),
        ],
        compiler_params=_params("arbitrary"),
        name="conv_prompt",
    )(proj, conv_w, conv_b.reshape(1, -1), ln_g.reshape(1, -1), ln_b.reshape(1, -1))


def _conv_state_kernel(st_ref, w_ref, newst_ref, part_ref):
    st = st_ref[...]
    newst_ref[:, 0:CONV_STATE - 1] = st[:, 1:CONV_STATE]
    newst_ref[:, CONV_STATE - 1:CONV_STATE] = jnp.zeros_like(st[:, 0:1])
    part_ref[...] = jnp.sum(st * w_ref[0:CONV_STATE][None], axis=1)


def conv_state_shift(state_t, conv_w_t, *, bt):
    bsz, _, depth, _ = state_t.shape
    blk = pl.BlockSpec((bt, CONV_STATE, depth, CONV_CH), lambda i: (i, 0, 0, 0))
    return pl.pallas_call(
        _conv_state_kernel,
        grid=(bsz // bt,),
        in_specs=[blk, pl.BlockSpec((CONV_WIDTH, depth, CONV_CH), lambda i: (0, 0, 0))],
        out_specs=[blk, pl.BlockSpec((bt, depth, CONV_CH), lambda i: (i, 0, 0))],
        out_shape=[jax.ShapeDtypeStruct(state_t.shape, F32),
                   jax.ShapeDtypeStruct((bsz, depth, CONV_CH), F32)],
        compiler_params=_params("parallel"),
        name="conv_state_shift",
    )(state_t, conv_w_t)


def _conv_state_finish_kernel(u_ref, st_hbm, o_ref):
    del st_hbm
    o_ref[...] = u_ref[...]


def conv_state_finish(new_state, u_all):
    bsz, _, depth, _ = new_state.shape
    return pl.pallas_call(
        _conv_state_finish_kernel,
        grid=(1,),
        in_specs=[pl.BlockSpec((bsz, 1, depth, CONV_CH), lambda i: (0, 0, 0, 0)),
                  pl.BlockSpec(memory_space=pl.ANY)],
        out_specs=pl.BlockSpec((bsz, 1, depth, CONV_CH), lambda i: (0, CONV_STATE - 1, 0, 0)),
        out_shape=jax.ShapeDtypeStruct(new_state.shape, F32),
        input_output_aliases={1: 0},
        compiler_params=_params("arbitrary"),
        name="conv_state_finish",
    )(u_all, new_state)


def _conv_sample_kernel(a_ref, b_ref, gc_ref, part_ref, w_ref, cb_ref, lg_ref, lb_ref, out_ref, u_ref, *, layer):
    u = a_ref[...] * jax.nn.sigmoid(b_ref[...])
    u_ref[...] = u
    y = part_ref[:, layer, :] + u * w_ref[CONV_STATE:CONV_WIDTH, :] + cb_ref[...]
    mu = jnp.mean(y, axis=-1, keepdims=True)
    d = y - mu
    var = jnp.mean(d * d, axis=-1, keepdims=True)
    z = d * lax.rsqrt(var + EPS) * lg_ref[...] + lb_ref[...]
    out_ref[...] = (_silu(z) * _silu(gc_ref[...])).astype(out_ref.dtype)


def conv_sample(proj, part, layer, conv_w, conv_b, ln_g, ln_b):
    bsz = proj.shape[0]
    vec = lambda: pl.BlockSpec((1, CONV_CH), lambda i: (0, 0))
    row = lambda: pl.BlockSpec((bsz, CONV_CH), lambda i: (0, 0))
    return pl.pallas_call(
        functools.partial(_conv_sample_kernel, layer=layer),
        grid=(1,),
        in_specs=[
            pl.BlockSpec((bsz, CONV_CH), lambda i: (0, COL_A // CONV_CH)),
            pl.BlockSpec((bsz, CONV_CH), lambda i: (0, COL_B // CONV_CH)),
            pl.BlockSpec((bsz, CONV_CH), lambda i: (0, COL_GC // CONV_CH)),
            pl.BlockSpec(part.shape, lambda i: (0, 0, 0)),
            pl.BlockSpec((CONV_WIDTH, CONV_CH), lambda i: (0, 0)),
            vec(), vec(), vec(),
        ],
        out_specs=[row(), row()],
        out_shape=[
            jax.ShapeDtypeStruct((bsz, CONV_CH), BF16),
            jax.ShapeDtypeStruct((bsz, CONV_CH), F32),
        ],
        compiler_params=_params("arbitrary"),
        name="conv_sample",
    )(proj, proj, proj, part, conv_w, conv_b.reshape(1, -1), ln_g.reshape(1, -1), ln_b.reshape(1, -1))


def _mla_common(ql_ref, kvl_ref, kpe_ref, cos_ref, sn_ref, sp_ref, qg_ref, kvg_ref, wqn_ref, wqp_ref):
    qn = _rms(ql_ref[...], qg_ref[...]).astype(BF16)
    q_nope = _dot(qn, wqn_ref[...]) * (MLA_SCALE * LOG2E)
    q_rope_raw = _dot(qn, wqp_ref[...]) * (MLA_SCALE * LOG2E)
    cos_t, sin_n, sin_p = cos_ref[...], sn_ref[...], sp_ref[...]
    q_rope = [_rope128(q_rope_raw[:, h * LANE:(h + 1) * LANE], cos_t, sin_n, sin_p)
              for h in range(MLA_HEADS)]
    c_kv = _rms(kvl_ref[...], kvg_ref[...])
    k_pe = _rope128(kpe_ref[...], cos_t, sin_n, sin_p)
    return q_nope, q_rope, c_kv, k_pe


def _mla_prompt_kernel(ql_ref, kvl_ref, kpe_ref, cos_ref, sn_ref, sp_ref, qg_ref, kvg_ref,
                       wqn_ref, wqp_ref, wuk_ref, wuv_ref,
                       q_ref, k_ref, v_ref, ckv_ref, kpeo_ref):
    q_nope, q_rope, c_kv, k_pe = _mla_common(ql_ref, kvl_ref, kpe_ref, cos_ref, sn_ref, sp_ref,
                                             qg_ref, kvg_ref, wqn_ref, wqp_ref)
    ckv_ref[...] = c_kv
    kpeo_ref[...] = k_pe[:, :QK_ROPE]
    c_bf = c_kv.astype(BF16)
    k_nope = _dot(c_bf, wuk_ref[...])
    v = _dot(c_bf, wuv_ref[...])
    for h in range(MLA_HEADS):
        v_ref[h * V_DIM:(h + 1) * V_DIM, :] = v[:, h * V_DIM:(h + 1) * V_DIM].T.astype(BF16)
    k_pe_bf = k_pe.astype(BF16)
    for h in range(MLA_HEADS):
        lo = h * QK_PAD
        q_ref[:, lo:lo + LANE] = q_nope[:, h * LANE:(h + 1) * LANE].astype(BF16)
        q_ref[:, lo + LANE:lo + QK_PAD] = q_rope[h].astype(BF16)
        k_ref[:, lo:lo + LANE] = k_nope[:, h * LANE:(h + 1) * LANE].astype(BF16)
        k_ref[:, lo + LANE:lo + QK_PAD] = k_pe_bf


def _proj_specs(tm, n_tab):
    return [
        pl.BlockSpec((tm, Q_LORA), lambda i: (i, COL_QL // Q_LORA)),
        pl.BlockSpec((tm, KV_LORA), lambda i: (i, COL_KV // KV_LORA)),
        pl.BlockSpec((tm, LANE), lambda i: (i, COL_KPE // LANE)),
        pl.BlockSpec((tm, LANE), lambda i: (i % n_tab, 0)),
        pl.BlockSpec((tm, LANE), lambda i: (i % n_tab, 0)),
        pl.BlockSpec((tm, LANE), lambda i: (i % n_tab, 0)),
    ]


def _full(shape):
    return pl.BlockSpec(shape, lambda i: (0,) * len(shape))


def mla_prompt(proj, tabs, q_g, kv_g, wqn, wqp, wuk, wuv, *, tm):
    t = proj.shape[0]
    n_tab = tabs[0].shape[0] // tm
    h = MLA_HEADS
    return pl.pallas_call(
        _mla_prompt_kernel,
        grid=(t // tm,),
        in_specs=_proj_specs(tm, n_tab) + [
            _full((1, Q_LORA)), _full((1, KV_LORA)),
            _full((Q_LORA, h * LANE)), _full((Q_LORA, h * LANE)),
            _full((KV_LORA, h * LANE)), _full((KV_LORA, h * LANE)),
        ],
        out_specs=[
            pl.BlockSpec((tm, h * QK_PAD), lambda i: (i, 0)),
            pl.BlockSpec((tm, h * QK_PAD), lambda i: (i, 0)),
            pl.BlockSpec((h * V_DIM, tm), lambda i: (0, i)),
            pl.BlockSpec((tm, KV_LORA), lambda i: (i, 0)),
            pl.BlockSpec((tm, QK_ROPE), lambda i: (i, 0)),
        ],
        out_shape=[
            jax.ShapeDtypeStruct((t, h * QK_PAD), BF16),
            jax.ShapeDtypeStruct((t, h * QK_PAD), BF16),
            jax.ShapeDtypeStruct((h * V_DIM, t), BF16),
            jax.ShapeDtypeStruct((t, KV_LORA), F32),
            jax.ShapeDtypeStruct((t, QK_ROPE), F32),
        ],
        compiler_params=_params("parallel"),
        name="mla_prompt",
    )(proj, proj, proj, *tabs, q_g.reshape(1, -1), kv_g.reshape(1, -1), wqn, wqp, wuk, wuv)


def _mla_sample_kernel(ql_ref, kvl_ref, kpe_ref, cos_ref, sn_ref, sp_ref, qg_ref, kvg_ref,
                       wqn_ref, wqp_ref, wukt_ref,
                       qa_ref, qp_ref, ckv_ref, kpeo_ref):
    q_nope, q_rope, c_kv, k_pe = _mla_common(ql_ref, kvl_ref, kpe_ref, cos_ref, sn_ref, sp_ref,
                                             qg_ref, kvg_ref, wqn_ref, wqp_ref)
    ckv_ref[...] = c_kv
    kpeo_ref[...] = k_pe[:, :QK_ROPE]
    for h in range(MLA_HEADS):
        qa_ref[h] = _dot(q_nope[:, h * LANE:(h + 1) * LANE].astype(BF16), wukt_ref[h])
        qp_ref[h] = q_rope[h]


def mla_sample(proj, tabs, q_g, kv_g, wqn, wqp, wukt):
    t = proj.shape[0]
    h = MLA_HEADS
    return pl.pallas_call(
        _mla_sample_kernel,
        grid=(1,),
        in_specs=_proj_specs(t, 1) + [
            _full((1, Q_LORA)), _full((1, KV_LORA)),
            _full((Q_LORA, h * LANE)), _full((Q_LORA, h * LANE)),
            _full((h, QK_NOPE, KV_LORA)),
        ],
        out_specs=[
            _full((h, t, KV_LORA)), _full((h, t, LANE)),
            _full((t, KV_LORA)), _full((t, QK_ROPE)),
        ],
        out_shape=[
            jax.ShapeDtypeStruct((h, t, KV_LORA), F32),
            jax.ShapeDtypeStruct((h, t, LANE), F32),
            jax.ShapeDtypeStruct((t, KV_LORA), F32),
            jax.ShapeDtypeStruct((t, QK_ROPE), F32),
        ],
        compiler_params=_params("arbitrary"),
        name="mla_sample",
    )(proj, proj, proj, *tabs, q_g.reshape(1, -1), kv_g.reshape(1, -1), wqn, wqp, wukt)


FLASH_TQ = 512


def _flash_tile(q, k_ref, vt_ref, s_t, kv):
    tq = FLASH_TQ
    causal_t = (lax.broadcasted_iota(jnp.int32, (tq, tq), 0)
                <= lax.broadcasted_iota(jnp.int32, (tq, tq), 1))
    if s_t is None:
        s_t = _dot_nt(k_ref[0:kv + tq, :], q)
    s_d = jnp.where(causal_t, s_t[kv:kv + tq], NEG_BIG)
    m = jnp.max(s_d, axis=0, keepdims=True)
    if kv > 0:
        s_o = s_t[0:kv]
        m = jnp.maximum(m, jnp.max(s_o, axis=0, keepdims=True))
        p_o = jnp.exp2(s_o - m)
        l = jnp.sum(p_o, axis=0, keepdims=True)
        acc = _dot(vt_ref[:, 0:kv], p_o.astype(BF16))
    else:
        l = jnp.zeros((1, tq), F32)
        acc = jnp.zeros((V_DIM, tq), F32)
    p_d = jnp.exp2(s_d - m)
    l = l + jnp.sum(p_d, axis=0, keepdims=True)
    acc = acc + _dot(vt_ref[:, kv:kv + tq], p_d.astype(BF16))
    return (acc / l).T


def _flash_kernel(q_ref, k_ref, vt_ref, g_ref, o_ref, *, seq):
    tq = FLASH_TQ
    n_q = seq // tq

    def scores(i):
        kv = i * tq
        return _dot_nt(k_ref[0:kv + tq, :], q_ref[kv:kv + tq, :])

    s_next = scores(0)
    for i in range(n_q):
        kv = i * tq
        s_t = s_next
        if i + 1 < n_q:
            s_next = scores(i + 1)
        o = _flash_tile(None, k_ref, vt_ref, s_t, kv)
        o_ref[kv:kv + tq, :] = (o * _silu(g_ref[kv:kv + tq, :])).astype(o_ref.dtype)


def flash_prompt(q, k, v_t, proj, *, bsz, seq):
    h = MLA_HEADS
    return pl.pallas_call(
        functools.partial(_flash_kernel, seq=seq),
        grid=(bsz, h),
        in_specs=[
            pl.BlockSpec((seq, QK_PAD), lambda b, hh: (b, hh)),
            pl.BlockSpec((seq, QK_PAD), lambda b, hh: (b, hh)),
            pl.BlockSpec((V_DIM, seq), lambda b, hh: (hh, b)),
            pl.BlockSpec((seq, V_DIM), lambda b, hh: (b, COL_GM // V_DIM + hh)),
        ],
        out_specs=pl.BlockSpec((seq, V_DIM), lambda b, hh: (b, hh)),
        out_shape=jax.ShapeDtypeStruct((bsz * seq, h * V_DIM), BF16),
        compiler_params=_params("parallel", "parallel"),
        name="flash_prompt",
    )(q, k, v_t, proj)


PAGED_CHUNK = 1024
PAGED_GROUPS = 4


def _paged_kernel(pt_ref, qa_ref, qp_ref, cn_ref, kn_ref, ckv_hbm, krt_hbm, *rest, layer, n_pages, flash_tiles):
    if flash_tiles:
        fq_ref, fk_ref, fvt_ref, fg_ref, o_ref, fo_ref, cbuf, kbuf, cbf, s_ref, sems, fs_ref = rest
    else:
        o_ref, cbuf, kbuf, cbf, s_ref, sems = rest
    b = pl.program_id(0)
    nb = pl.num_programs(0)
    slot = b % 2

    def copies(bb, sl, p):
        page = pt_ref[bb * n_pages + p]
        toks = pl.ds(p * PAGE_SIZE, PAGE_SIZE)
        return (pltpu.make_async_copy(ckv_hbm.at[page, layer], cbuf.at[sl, toks, :], sems.at[0, sl]),
                pltpu.make_async_copy(krt_hbm.at[page, layer], kbuf.at[sl, :, toks], sems.at[1, sl]))

    def start_all(bb, sl):
        for p in range(n_pages):
            c1, c2 = copies(bb, sl, p)
            c1.start(priority=p % 2)
            c2.start(priority=(p + 1) % 2)

    @pl.when(b == 0)
    def _():
        start_all(b, slot)

    @pl.when(b + 1 < nb)
    def _():
        start_all(b + 1, 1 - slot)

    for c in range(flash_tiles):
        @pl.when(b % flash_tiles == c)
        def _(c=c):
            tq = FLASH_TQ
            kv = c * tq
            if c + 1 < flash_tiles:
                nxt = slice(kv + tq, kv + 2 * tq)
                fs_ref[(c + 1) % 2, 0:kv + 2 * tq, :] = _dot_nt(fk_ref[0:kv + 2 * tq, :], fq_ref[nxt, :])
            s_t = fs_ref[c % 2, 0:kv + tq, :] if c > 0 else None
            rows = slice(kv, kv + tq)
            o = _flash_tile(fq_ref[rows, :], fk_ref, fvt_ref, s_t, kv)
            fo_ref[rows, :] = (o * _silu(fg_ref[rows, :])).astype(fo_ref.dtype)

    for p in range(n_pages):
        c1, c2 = copies(b, slot, p)
        c1.wait()
        c2.wait()

    past = n_pages * PAGE_SIZE
    n_chunks = past // PAGED_CHUNK
    per_group = n_chunks // PAGED_GROUPS
    qa = qa_ref[0].astype(BF16)
    qp = qp_ref[0][:, :QK_ROPE].astype(BF16)

    def qk_chunk(c):
        toks = slice(c * PAGED_CHUNK, (c + 1) * PAGED_CHUNK)
        cc = cbuf[slot, toks, :].astype(BF16)
        kk = kbuf[slot, :, toks].astype(BF16)
        cbf[toks, :] = cc
        s_ref[:, toks] = _dot_nt(qa, cc) + _dot(qp, kk)

    def pv_chunk(c, m_g, l_g, acc_g):
        toks = slice(c * PAGED_CHUNK, (c + 1) * PAGED_CHUNK)
        p = jnp.exp2(s_ref[:, toks] - m_g)
        return l_g + jnp.sum(p, axis=-1, keepdims=True), acc_g + _dot(p.astype(BF16), cbf[toks, :])

    def group_max(g):
        g_toks = slice(g * per_group * PAGED_CHUNK, (g + 1) * per_group * PAGED_CHUNK)
        return jnp.max(s_ref[:, g_toks], axis=-1, keepdims=True)

    stats = []
    for c in range(per_group):
        qk_chunk(c)
    for g in range(PAGED_GROUPS):
        m_g = group_max(g)
        l_g = jnp.zeros((MLA_HEADS, 1), F32)
        acc_g = jnp.zeros((MLA_HEADS, KV_LORA), F32)
        for c in range(per_group):
            if g + 1 < PAGED_GROUPS:
                qk_chunk((g + 1) * per_group + c)
            l_g, acc_g = pv_chunk(g * per_group + c, m_g, l_g, acc_g)
        stats.append((m_g, l_g, acc_g))

    c_new = cn_ref[0].astype(BF16).astype(F32)
    k_new = kn_ref[0].astype(BF16).astype(F32)
    s_new = (jnp.sum(qa.astype(F32) * c_new, axis=-1, keepdims=True)
             + jnp.sum(qp.astype(F32) * k_new, axis=-1, keepdims=True))
    m = s_new
    for m_g, _, _ in stats:
        m = jnp.maximum(m, m_g)
    p_new = jnp.exp2(s_new - m)
    l = p_new
    acc = p_new.astype(BF16).astype(F32) * c_new
    for m_g, l_g, acc_g in stats:
        w_g = jnp.exp2(m_g - m)
        l = l + w_g * l_g
        acc = acc + w_g * acc_g
    o_ref[0] = acc / l


def paged_attention(page_table, qa, qp, c_new, k_new, cache_kv, cache_krt, *, layer, flash=None):
    bsz, n_pages = page_table.shape
    past = n_pages * PAGE_SIZE
    h = MLA_HEADS
    in_specs = [
        pl.BlockSpec((1, h, KV_LORA), lambda b, pt: (b, 0, 0)),
        pl.BlockSpec((1, h, LANE), lambda b, pt: (b, 0, 0)),
        pl.BlockSpec((1, 1, KV_LORA), lambda b, pt: (b, 0, 0)),
        pl.BlockSpec((1, 1, QK_ROPE), lambda b, pt: (b, 0, 0)),
        pl.BlockSpec(memory_space=pl.ANY),
        pl.BlockSpec(memory_space=pl.ANY),
    ]
    args = [qa, qp, c_new, k_new, cache_kv, cache_krt]
    out_specs = [pl.BlockSpec((1, h, KV_LORA), lambda b, pt: (b, 0, 0))]
    out_shape = [jax.ShapeDtypeStruct((bsz, h, KV_LORA), F32)]
    flash_tiles, flash_scratch = 0, []
    if flash is not None:
        fq, fk, fvt, proj, f_bsz, seq = flash
        flash_tiles = seq // FLASH_TQ
        assert f_bsz * h * flash_tiles == bsz
        sq = lambda b: b // flash_tiles // h
        head = lambda b: b // flash_tiles % h
        in_specs += [
            pl.BlockSpec((seq, QK_PAD), lambda b, pt: (sq(b), head(b))),
            pl.BlockSpec((seq, QK_PAD), lambda b, pt: (sq(b), head(b))),
            pl.BlockSpec((V_DIM, seq), lambda b, pt: (head(b), sq(b))),
            pl.BlockSpec((seq, V_DIM), lambda b, pt: (sq(b), COL_GM // V_DIM + head(b))),
        ]
        args += [fq, fk, fvt, proj]
        out_specs.append(pl.BlockSpec((seq, V_DIM), lambda b, pt: (sq(b), head(b))))
        out_shape.append(jax.ShapeDtypeStruct((f_bsz * seq, h * V_DIM), BF16))
        flash_scratch = [pltpu.VMEM((2, seq, FLASH_TQ), F32)]
    grid_spec = pltpu.PrefetchScalarGridSpec(
        num_scalar_prefetch=1,
        grid=(bsz,),
        in_specs=in_specs,
        out_specs=out_specs,
        scratch_shapes=[
            pltpu.VMEM((2, past, KV_LORA), F32),
            pltpu.VMEM((2, QK_ROPE, past), F32),
            pltpu.VMEM((past, KV_LORA), BF16),
            pltpu.VMEM((h, past), F32),
            pltpu.SemaphoreType.DMA((2, 2)),
        ] + flash_scratch,
    )
    outs = pl.pallas_call(
        functools.partial(_paged_kernel, layer=layer, n_pages=n_pages, flash_tiles=flash_tiles),
        grid_spec=grid_spec,
        out_shape=out_shape,
        compiler_params=_params("arbitrary"),
        name="paged_attention",
    )(page_table.reshape(-1), *args)
    return outs if flash is not None else outs[0]


def _uv_kernel(ol_ref, wuv_ref, g_ref, o_ref):
    for h in range(MLA_HEADS):
        o = _dot(ol_ref[h].astype(BF16), wuv_ref[h])
        sl = slice(h * V_DIM, (h + 1) * V_DIM)
        o_ref[:, sl] = (o * _silu(g_ref[:, sl])).astype(o_ref.dtype)


def uv_project(o_lat, wuv3, proj):
    h, t, _ = o_lat.shape
    return pl.pallas_call(
        _uv_kernel,
        grid=(1,),
        in_specs=[
            _full((h, t, KV_LORA)), _full((h, KV_LORA, V_DIM)),
            pl.BlockSpec((t, MLA_WIDTH), lambda i: (0, COL_GM // MLA_WIDTH)),
        ],
        out_specs=_full((t, MLA_WIDTH)),
        out_shape=jax.ShapeDtypeStruct((t, MLA_WIDTH), BF16),
        compiler_params=_params("arbitrary"),
        name="uv_project",
    )(o_lat, wuv3, proj)


def _proj_norm_res_kernel(*refs, n_in):
    ins, ws = refs[:n_in], refs[n_in:2 * n_in]
    x_ref, g_ref, o_ref = refs[2 * n_in:]
    y = _dot(ins[0][...].astype(BF16), ws[0][...])
    for a, w in zip(ins[1:], ws[1:]):
        y = y + _dot(a[...].astype(BF16), w[...])
    o_ref[...] = x_ref[...] + _rms(y, g_ref[...])


def proj_norm_res(ins, ws, x, g, *, tm, name):
    t, d = x.shape
    n_in = len(ins)
    return pl.pallas_call(
        functools.partial(_proj_norm_res_kernel, n_in=n_in),
        grid=(t // tm,),
        in_specs=([pl.BlockSpec((tm, a.shape[1]), lambda i: (i, 0)) for a in ins]
                  + [_full(w.shape) for w in ws]
                  + [pl.BlockSpec((tm, d), lambda i: (i, 0)), _full((1, d))]),
        out_specs=pl.BlockSpec((tm, d), lambda i: (i, 0)),
        out_shape=jax.ShapeDtypeStruct((t, d), F32),
        compiler_params=_params("parallel"),
        name=name,
    )(*ins, *ws, x, g.reshape(1, d))


def _mix_cross_kernel(conv_ref, attn_ref, x_ref, wc_ref, wm_ref, gmix_ref,
                      gpre_ref, wq_ref, mk_ref, mv_ref, wo_ref, gpost_ref, o_ref):
    y = _dot(conv_ref[...].astype(BF16), wc_ref[...]) + _dot(attn_ref[...], wm_ref[...])
    x = x_ref[...] + _rms(y, gmix_ref[...])
    hn = _rms(x, gpre_ref[...]).astype(BF16)
    q = (_dot(hn, wq_ref[...]) * MEM_SCALE).astype(BF16)
    outs = []
    for h in range(MEM_HEADS):
        sl = slice(h * MEM_HEAD_DIM, (h + 1) * MEM_HEAD_DIM)
        s = _dot_nt(q[:, sl], mk_ref[:, sl].astype(BF16))
        m = jnp.max(s, axis=-1, keepdims=True)
        p = jnp.exp(s - m)
        l = jnp.sum(p, axis=-1, keepdims=True)
        outs.append((_dot(p.astype(BF16), mv_ref[:, sl].astype(BF16)) / l).astype(BF16))
    y2 = _dot(jnp.concatenate(outs, axis=-1), wo_ref[...])
    o_ref[...] = x + _rms(y2, gpost_ref[...])


def mix_cross_prompt(conv_out, attn, x, wc, wm, g_mix, g_pre, wq, mem_kv, wo, g_post, *, seq, tm):
    t, d = x.shape
    per_seq = seq // tm
    const = lambda shape: pl.BlockSpec(shape, lambda i: (0,) * len(shape), pipeline_mode=pl.Buffered(1))
    rows = lambda width: pl.BlockSpec((tm, width), lambda i: (i, 0))
    return pl.pallas_call(
        _mix_cross_kernel,
        grid=(t // tm,),
        in_specs=[
            rows(CONV_CH), rows(MLA_WIDTH), rows(d),
            const(wc.shape), const(wm.shape), const((1, d)),
            const((1, d)), const(wq.shape),
            pl.BlockSpec((N_MEM, MEM_WIDTH), lambda i: (i // per_seq, 0)),
            pl.BlockSpec((N_MEM, MEM_WIDTH), lambda i: (i // per_seq, 1)),
            const(wo.shape), const((1, d)),
        ],
        out_specs=rows(d),
        out_shape=jax.ShapeDtypeStruct((t, d), F32),
        compiler_params=_params("parallel"),
        name="mix_cross_prompt",
    )(conv_out, attn, x, wc, wm, g_mix.reshape(1, d), g_pre.reshape(1, d), wq, mem_kv, mem_kv, wo,
      g_post.reshape(1, d))


def _cross_sample_kernel(q_ref, mk_ref, mv_ref, o_ref):
    bt = q_ref.shape[0]
    q = (q_ref[...] * MEM_SCALE).astype(BF16)
    for h in range(MEM_HEADS):
        sl = slice(h * MEM_HEAD_DIM, (h + 1) * MEM_HEAD_DIM)
        rows = pl.ds(h, N_MEM, stride=MEM_HEADS)
        qh = jnp.broadcast_to(q[:, :, sl], (bt, 8, MEM_HEAD_DIM))
        kh = mk_ref[:, 0, rows, :].astype(BF16)
        vh = mv_ref[:, 0, rows, :].astype(BF16)
        s = jnp.einsum("bqe,bme->bqm", qh, kh, preferred_element_type=F32)
        m = jnp.max(s, axis=-1, keepdims=True)
        p = jnp.exp(s - m)
        l = jnp.sum(p, axis=-1, keepdims=True)
        o = jnp.einsum("bqm,bme->bqe", p.astype(BF16), vh, preferred_element_type=F32) / l
        o_ref[:, :, sl] = o[:, 0:1, :].astype(o_ref.dtype)


def cross_sample_core(q3, mem_k, mem_v, *, layer, bt):
    bsz = q3.shape[0]
    mem = lambda: pl.BlockSpec((bt, 1, N_MEM * MEM_HEADS, MEM_HEAD_DIM), lambda i: (i, layer, 0, 0))
    return pl.pallas_call(
        _cross_sample_kernel,
        grid=(bsz // bt,),
        in_specs=[pl.BlockSpec((bt, 1, MEM_WIDTH), lambda i: (i, 0, 0)), mem(), mem()],
        out_specs=pl.BlockSpec((bt, 1, MEM_WIDTH), lambda i: (i, 0, 0)),
        out_shape=jax.ShapeDtypeStruct((bsz, 1, MEM_WIDTH), BF16),
        compiler_params=_params("parallel"),
        name="cross_sample",
    )(q3, mem_k, mem_v)


def _rope_tables(pos):
    half = QK_ROPE // 2
    inv_freq = ROPE_BASE ** (-jnp.arange(half, dtype=F32) / half)
    ang = pos.astype(F32)[:, None] * inv_freq[None, :]
    cos, sin = jnp.cos(ang), jnp.sin(ang)
    z = jnp.zeros_like(cos)
    return (jnp.concatenate([cos, cos, z, z], axis=-1),
            jnp.concatenate([-sin, z, z, z], axis=-1),
            jnp.concatenate([z, sin, z, z], axis=-1))


def _prep_layer(l, w_uq, w_uk, w_uv, w_out, w_xq, w_mk, w_mv, w_xo):
    d = w_out.shape[2]
    uq = w_uq[l]
    wqn = uq[:, :, :QK_NOPE].reshape(Q_LORA, -1).astype(BF16)
    wqp = jnp.pad(uq[:, :, QK_NOPE:], ((0, 0), (0, 0), (0, LANE - QK_ROPE))).reshape(Q_LORA, -1).astype(BF16)
    return dict(
        wqn=wqn, wqp=wqp,
        wuk=w_uk[l].reshape(KV_LORA, -1).astype(BF16),
        wuv=w_uv[l].reshape(KV_LORA, -1).astype(BF16),
        wukt=w_uk[l].transpose(1, 2, 0).astype(BF16),
        wuv3=w_uv[l].transpose(1, 0, 2).astype(BF16),
        w_out_c=w_out[l][:CONV_CH].astype(BF16),
        w_out_m=w_out[l][CONV_CH:].astype(BF16),
        w_xq=w_xq[l].reshape(d, -1).astype(BF16),
        w_mkv=jnp.concatenate([w_mk[l].reshape(d, -1), w_mv[l].reshape(d, -1)], axis=1).astype(BF16),
        w_xo=w_xo[l].reshape(-1, d).astype(BF16),
    )


def kernel(x_prompt, x_sample, mem_prompt, cache_kv_latent, cache_k_rope, state_conv, cache_mem_k, cache_mem_v, page_table, norm_mix_pre, w_in, conv_w, conv_b, conv_ln_g, conv_ln_b, q_norm_g, w_uq, kv_norm_g, w_uk, w_uv, w_out, norm_mix_post, norm_x_pre, norm_mem, w_xq, w_mk, w_mv, w_xo, norm_x_post):
    b_p, s_p, d = x_prompt.shape
    b_s = x_sample.shape[0]
    depth = w_in.shape[0]
    n_pages = page_table.shape[1]
    t_p = b_p * s_p
    tm_p = min(ROW_TILE, s_p)

    tabs_p = _rope_tables(jnp.arange(s_p, dtype=jnp.int32))
    tabs_s = _rope_tables(jnp.full((b_s,), n_pages * PAGE_SIZE, jnp.int32))
    mem2 = mem_prompt.reshape(b_p * N_MEM, d)
    mem_k_s = cache_mem_k.reshape(b_s, depth, N_MEM * MEM_HEADS, MEM_HEAD_DIM)
    mem_v_s = cache_mem_v.reshape(b_s, depth, N_MEM * MEM_HEADS, MEM_HEAD_DIM)
    cache_krt = cache_k_rope.transpose(0, 1, 3, 2)
    state_t = state_conv.transpose(0, 2, 1, 3)
    conv_w_t = conv_w.transpose(1, 0, 2)
    new_state_t, conv_part = conv_state_shift(state_t, conv_w_t, bt=min(STATE_BATCH_TILE, b_s))
    w_in_t = w_in_prep(w_in.transpose(0, 2, 1))

    xp = x_prompt.reshape(t_p, d)
    xs = x_sample.reshape(b_s, d)
    lat_p, kpe_p, conv_p, mk_p, mv_p, lat_s, kpe_s = [], [], [], [], [], [], []
    u_s = []
    for l in range(depth):
        w = _prep_layer(l, w_uq, w_uk, w_uv, w_out, w_xq, w_mk, w_mv, w_xo)

        proj = rms_matmul(xp, norm_mix_pre[l], w_in_t, tm=min(IN_PROJ_ROWS, s_p), tn=IN_PROJ_COLS, name="in_proj_p",
                          w_transposed=True, layer=l)
        conv_out, conv_state = conv_prompt(proj, conv_w[l], conv_b[l], conv_ln_g[l], conv_ln_b[l],
                                           bsz=b_p, seq=s_p)
        q, k, v, c_kv, k_pe = mla_prompt(proj, tabs_p, q_norm_g[l], kv_norm_g[l],
                                         w["wqn"], w["wqp"], w["wuk"], w["wuv"], tm=tm_p)
        fuse = b_p * MLA_HEADS * (s_p // FLASH_TQ) == b_s
        if not fuse:
            attn = flash_prompt(q, k, v, proj, bsz=b_p, seq=s_p)

        proj_s = rms_matmul(xs, norm_mix_pre[l], w_in_t, tm=b_s, tn=IN_PROJ_COLS, name="in_proj_s",
                            w_transposed=True, layer=l)
        conv_out_s, u_l = conv_sample(proj_s, conv_part, l, conv_w[l], conv_b[l], conv_ln_g[l], conv_ln_b[l])
        u_s.append(u_l)
        qa, qp, c_new, k_new = mla_sample(proj_s, tabs_s, q_norm_g[l], kv_norm_g[l],
                                          w["wqn"], w["wqp"], w["wukt"])
        paged = paged_attention(page_table, qa.transpose(1, 0, 2), qp.transpose(1, 0, 2),
                                c_new.reshape(b_s, 1, KV_LORA), k_new.reshape(b_s, 1, QK_ROPE),
                                cache_kv_latent, cache_krt, layer=l,
                                flash=(q, k, v, proj, b_p, s_p) if fuse else None)
        o_lat, attn = paged if fuse else (paged, attn)

        mem_kv = rms_matmul(mem2, norm_mem[l], w["w_mkv"], tm=N_MEM, tn=2 * MEM_WIDTH, name="mem_kv")
        xp = mix_cross_prompt(conv_out, attn, xp, w["w_out_c"], w["w_out_m"], norm_mix_post[l],
                              norm_x_pre[l], w["w_xq"], mem_kv, w["w_xo"], norm_x_post[l], seq=s_p, tm=tm_p)
        lat_p.append(c_kv.reshape(b_p, s_p, KV_LORA))
        kpe_p.append(k_pe.reshape(b_p, s_p, QK_ROPE))
        conv_p.append(conv_state)
        mk_p.append(mem_kv[:, :MEM_WIDTH].reshape(b_p, N_MEM, MEM_HEADS, MEM_HEAD_DIM))
        mv_p.append(mem_kv[:, MEM_WIDTH:].reshape(b_p, N_MEM, MEM_HEADS, MEM_HEAD_DIM))

        attn_s = uv_project(o_lat.transpose(1, 0, 2), w["wuv3"], proj_s)
        xs = proj_norm_res([conv_out_s, attn_s], [w["w_out_c"], w["w_out_m"]],
                           xs, norm_mix_post[l], tm=b_s, name="out_proj_s")
        q_x = rms_matmul(xs, norm_x_pre[l], w["w_xq"], tm=b_s, tn=MEM_WIDTH, name="cross_q_s")
        o_x = cross_sample_core(q_x.reshape(b_s, 1, MEM_WIDTH), mem_k_s, mem_v_s, layer=l, bt=CROSS_BATCH_TILE)
        xs = proj_norm_res([o_x.reshape(b_s, MEM_WIDTH)], [w["w_xo"]], xs, norm_x_post[l],
                           tm=b_s, name="cross_out_s")
        lat_s.append(c_new.reshape(b_s, 1, KV_LORA))
        kpe_s.append(k_new.reshape(b_s, 1, QK_ROPE))

    return (xp.reshape(b_p, s_p, d), xs.reshape(b_s, 1, d),
            jnp.stack(lat_p, axis=1), jnp.stack(kpe_p, axis=1), jnp.stack(conv_p, axis=1),
            jnp.stack(mk_p, axis=1), jnp.stack(mv_p, axis=1),
            jnp.stack(lat_s, axis=1), jnp.stack(kpe_s, axis=1),
            conv_state_finish(new_state_t, jnp.stack(u_s, axis=1)[:, None]).transpose(0, 2, 1, 3))
```

```python
import functools

import jax
import numpy as np
import jax.numpy as jnp
from jax import lax
from jax.experimental import pallas as pl
from jax.experimental.pallas import tpu as pltpu

F32 = jnp.float32
BF16 = jnp.bfloat16

CONV_CH = 1024
CONV_WIDTH = 31
CONV_STATE = CONV_WIDTH - 1
MLA_HEADS = 8
QK_NOPE = 128
QK_ROPE = 64
V_DIM = 128
MLA_WIDTH = MLA_HEADS * V_DIM
Q_LORA = 512
KV_LORA = 256
MLA_SCALE = (QK_NOPE + QK_ROPE) ** -0.5
ROPE_BASE = 10000.0
PAGE_SIZE = 128
N_MEM = 256
MEM_HEADS = 4
MEM_HEAD_DIM = 128
MEM_WIDTH = MEM_HEADS * MEM_HEAD_DIM
MEM_SCALE = MEM_HEAD_DIM ** -0.5
EPS = 1e-6
LOG2E = 1.4426950408889634

LANE = 128
QK_PAD = 256
NEG_BIG = -1e30

PROJ_COLS = 5120
COL_A, COL_B, COL_GC, COL_GM, COL_QL, COL_KV, COL_KPE = 0, 1024, 2048, 3072, 4096, 4608, 4864

VMEM_LIMIT = 52 * 1024 * 1024

ROW_TILE = 512
IN_PROJ_ROWS = 1024
IN_PROJ_COLS = 1024
STATE_BATCH_TILE = 32
CROSS_BATCH_TILE = 16


def _params(*sem):
    return pltpu.CompilerParams(dimension_semantics=sem, vmem_limit_bytes=VMEM_LIMIT)


def _rms(x, g):
    return x * lax.rsqrt(jnp.mean(x * x, axis=-1, keepdims=True) + EPS) * g


def _silu(x):
    return x * jax.nn.sigmoid(x)


def _dot(a, b):
    return jnp.dot(a, b, preferred_element_type=F32)


def _dot_nt(a, b):
    return lax.dot_general(a, b, (((1,), (1,)), ((), ())), preferred_element_type=F32)


def _rope128(x, cos_t, sin_n, sin_p):
    return x * cos_t + pltpu.roll(x, 96, 1) * sin_n + pltpu.roll(x, 32, 1) * sin_p


def _rms_matmul_kernel(x_ref, g_ref, w_ref, o_ref, hn_ref, *, w_transposed):
    @pl.when(pl.program_id(1) == 0)
    def _():
        hn_ref[...] = _rms(x_ref[...], g_ref[...]).astype(BF16)

    dot = _dot_nt if w_transposed else _dot
    o_ref[...] = dot(hn_ref[...], w_ref[...]).astype(o_ref.dtype)


def rms_matmul(x, g, w, *, tm, tn, name, w_transposed=False, layer=None):
    t, k = x.shape
    n = w.shape[-2] if w_transposed else w.shape[-1]
    lead = () if layer is None else (None,)
    pick = () if layer is None else (layer,)
    w_spec = (pl.BlockSpec(lead + (tn, k), lambda i, j: pick + (j, 0)) if w_transposed
              else pl.BlockSpec(lead + (k, tn), lambda i, j: pick + (0, j)))
    return pl.pallas_call(
        functools.partial(_rms_matmul_kernel, w_transposed=w_transposed),
        grid=(t // tm, n // tn),
        in_specs=[
            pl.BlockSpec((tm, k), lambda i, j: (i, 0)),
            pl.BlockSpec((1, k), lambda i, j: (0, 0)),
            w_spec,
        ],
        out_specs=pl.BlockSpec((tm, tn), lambda i, j: (i, j)),
        out_shape=jax.ShapeDtypeStruct((t, n), F32),
        scratch_shapes=[pltpu.VMEM((tm, k), BF16)],
        compiler_params=_params("parallel", "arbitrary"),
        name=name,
    )(x, g.reshape(1, k), w)


_W_IN_GROUPS = (
    (COL_A, 0, 3 * CONV_CH),
    (COL_GM, 3 * CONV_CH + Q_LORA + KV_LORA + QK_ROPE, MLA_WIDTH),
    (COL_QL, 3 * CONV_CH, Q_LORA),
    (COL_KV, 3 * CONV_CH + Q_LORA, KV_LORA),
    (COL_KPE, 3 * CONV_CH + Q_LORA + KV_LORA, QK_ROPE),
)
W_PREP_TK = 512


def _w_in_prep_kernel(w_ref, o_ref):
    for dst, src, rows in _W_IN_GROUPS:
        o_ref[0, dst:dst + rows, :] = w_ref[0, src:src + rows, :].astype(BF16)
    pad0 = COL_KPE + QK_ROPE
    o_ref[0, pad0:, :] = jnp.zeros((PROJ_COLS - pad0, o_ref.shape[2]), BF16)


def w_in_prep(w_in_t):
    depth, n_in, k = w_in_t.shape
    return pl.pallas_call(
        _w_in_prep_kernel,
        grid=(depth, k // W_PREP_TK),
        in_specs=[pl.BlockSpec((1, n_in, W_PREP_TK), lambda l, j: (l, 0, j))],
        out_specs=pl.BlockSpec((1, PROJ_COLS, W_PREP_TK), lambda l, j: (l, 0, j)),
        out_shape=jax.ShapeDtypeStruct((depth, PROJ_COLS, k), BF16),
        compiler_params=_params("parallel", "parallel"),
        name="w_in_prep",
    )(w_in_t)


CONV_SEGS = 8
CONV_SEG_ROWS = 128
CONV_HALO = 32
CONV_BLOCK = 16
CONV_TAPS_A = 16
CONV_UNROLL = 16
CONV_PAD = 32
CONV_TILE = CONV_SEGS * CONV_SEG_ROWS


def _conv_prompt_kernel(proj_hbm, w_ref, cb_ref, lg_ref, lb_ref, out_hbm, state_ref,
                        xa, xb, xg, y_ref, o_ref, wb_ref, lgb_ref, in_sems, out_sem, *, tiles_per_seq):
    n = pl.program_id(0)
    n_tiles = pl.num_programs(0)
    slot = n % 2
    lrows, halo = CONV_SEG_ROWS, CONV_HALO

    def in_copies(tile, sl, seg, first):
        row = tile * CONV_TILE + seg * lrows
        if first:
            src_rows, dst_rows = pl.ds(row, lrows), pl.ds(halo, lrows)
        else:
            src_rows, dst_rows = pl.ds(row - halo, lrows + halo), pl.ds(0, lrows + halo)
        return (
            pltpu.make_async_copy(proj_hbm.at[src_rows, pl.ds(COL_A, CONV_CH)],
                                  xa.at[sl, dst_rows, seg, :], in_sems.at[sl, 0]),
            pltpu.make_async_copy(proj_hbm.at[src_rows, pl.ds(COL_B, CONV_CH)],
                                  xb.at[sl, dst_rows, seg, :], in_sems.at[sl, 1]),
            pltpu.make_async_copy(proj_hbm.at[pl.ds(row, lrows), pl.ds(COL_GC, CONV_CH)],
                                  xg.at[sl, :, seg, :], in_sems.at[sl, 2]),
        )

    def for_each_in_copy(tile, sl, fn):
        seq_start = tile % tiles_per_seq == 0

        @pl.when(seq_start)
        def _():
            for c in in_copies(tile, sl, 0, True):
                fn(c)

        @pl.when(jnp.logical_not(seq_start))
        def _():
            for c in in_copies(tile, sl, 0, False):
                fn(c)

        for seg in range(1, CONV_SEGS):
            for c in in_copies(tile, sl, seg, False):
                fn(c)

    def out_copies(tile):
        return [pltpu.make_async_copy(o_ref.at[:, seg, :],
                                      out_hbm.at[pl.ds(tile * CONV_TILE + seg * lrows, lrows), :], out_sem.at[0])
                for seg in range(CONV_SEGS)]

    @pl.when(n == 0)
    def _():
        for k in range(CONV_WIDTH):
            wb_ref[k] = jnp.broadcast_to(w_ref[k:k + 1, :], (CONV_SEGS, CONV_CH))
        lgb_ref[0] = jnp.broadcast_to(lg_ref[...], (CONV_SEGS, CONV_CH))
        lgb_ref[1] = jnp.broadcast_to(lb_ref[...], (CONV_SEGS, CONV_CH))
        for_each_in_copy(n, slot, lambda c: c.start())

    @pl.when(n + 1 < n_tiles)
    def _():
        for_each_in_copy(n + 1, 1 - slot, lambda c: c.start())

    for_each_in_copy(n, slot, lambda c: c.wait())

    def glu(i, carry):
        xa[slot, i] = xa[slot, i] * jax.nn.sigmoid(xb[slot, i])
        return carry

    lax.fori_loop(0, lrows + halo, glu, 0, unroll=4)

    @pl.when(n % tiles_per_seq == 0)
    def _():
        xa[slot, 0:halo, 0:1, :] = jnp.zeros((halo, 1, CONV_CH), F32)

    @pl.when(n % tiles_per_seq == tiles_per_seq - 1)
    def _():
        state_ref[0] = xa[slot, halo + lrows - CONV_STATE:halo + lrows, CONV_SEGS - 1, :]

    n_zero = CONV_PAD - (CONV_TAPS_A - 1)
    y_ref[0:n_zero] = jnp.zeros((n_zero, CONV_SEGS, CONV_CH), F32)
    shift = halo - CONV_STATE
    zero = jnp.zeros((CONV_SEGS, LANE), F32)
    for k0, n_taps in ((0, CONV_TAPS_A), (CONV_TAPS_A, CONV_WIDTH - CONV_TAPS_A)):
        first = k0 == 0
        n_iter = -(-(lrows + n_taps - 1) // CONV_UNROLL)
        j_start = min(shift + k0, lrows + halo - n_iter * CONV_UNROLL)
        for c0 in range(0, CONV_CH, LANE):
            cols = slice(c0, c0 + LANE)
            w = [wb_ref[k0 + t, :, cols] for t in range(n_taps)]
            fresh = (cb_ref[:, cols] + zero) if first else zero
            row_off = CONV_PAD - shift - k0 - (n_taps - 1)

            def body(it, carry, w=w, fresh=fresh, cols=cols, row_off=row_off, first=first, j_start=j_start):
                acc = list(carry)
                j0 = it * CONV_UNROLL + j_start
                for jj in range(CONV_UNROLL):
                    u = xa[slot, j0 + jj, :, cols]
                    acc = [a + wt * u for wt, a in zip(w, [fresh] + acc)]
                    done = acc.pop()
                    if first:
                        y_ref[j0 + (jj + row_off), :, cols] = done
                    else:
                        y_ref[j0 + (jj + row_off), :, cols] = y_ref[j0 + (jj + row_off), :, cols] + done
                return tuple(acc)

            lax.fori_loop(0, n_iter, body, tuple(zero for _ in range(n_taps - 1)))

    @pl.when(n > 0)
    def _():
        for c in out_copies(n - 1):
            c.wait()

    def norm_gate(r, carry):
        for ii in range(CONV_BLOCK):
            i = r * CONV_BLOCK + ii
            acc = y_ref[i + CONV_PAD]
            mu = jnp.mean(acc, axis=-1, keepdims=True)
            d = acc - mu
            var = jnp.mean(d * d, axis=-1, keepdims=True)
            z = d * lax.rsqrt(var + EPS) * lgb_ref[0] + lgb_ref[1]
            o_ref[i] = _silu(z) * _silu(xg[slot, i])
        return carry

    lax.fori_loop(0, lrows // CONV_BLOCK, norm_gate, 0)

    for c in out_copies(n):
        c.start()

    @pl.when(n == n_tiles - 1)
    def _():
        for c in out_copies(n):
            c.wait()


def conv_prompt(proj, conv_w, conv_b, ln_g, ln_b, *, bsz, seq):
    tiles_per_seq = seq // CONV_TILE
    vec = lambda: pl.BlockSpec((1, CONV_CH), lambda n: (0, 0))
    seg_buf = lambda rows: pltpu.VMEM((2, rows, CONV_SEGS, CONV_CH), F32)
    return pl.pallas_call(
        functools.partial(_conv_prompt_kernel, tiles_per_seq=tiles_per_seq),
        grid=(bsz * tiles_per_seq,),
        in_specs=[
            pl.BlockSpec(memory_space=pl.ANY),
            pl.BlockSpec((CONV_WIDTH, CONV_CH), lambda n: (0, 0)),
            vec(), vec(), vec(),
        ],
        out_specs=[
            pl.BlockSpec(memory_space=pl.ANY),
            pl.BlockSpec((1, CONV_STATE, CONV_CH), lambda n: (n // tiles_per_seq, 0, 0)),
        ],
        out_shape=[
            jax.ShapeDtypeStruct((bsz * seq, CONV_CH), F32),
            jax.ShapeDtypeStruct((bsz, CONV_STATE, CONV_CH), F32),
        ],
        scratch_shapes=[
            seg_buf(CONV_SEG_ROWS + CONV_HALO), seg_buf(CONV_SEG_ROWS + CONV_HALO), seg_buf(CONV_SEG_ROWS),
            pltpu.VMEM((CONV_SEG_ROWS + 2 * CONV_PAD, CONV_SEGS, CONV_CH), F32),
            pltpu.VMEM((CONV_SEG_ROWS, CONV_SEGS, CONV_CH), F32),
            pltpu.VMEM((CONV_WIDTH, CONV_SEGS, CONV_CH), F32),
            pltpu.VMEM((2, CONV_SEGS, CONV_CH), F32),
            pltpu.SemaphoreType.DMA((2, 3)), pltpu.SemaphoreType.DMA((1,)),
        ],
        compiler_params=_params("arbitrary"),
        name="conv_prompt",
    )(proj, conv_w, conv_b.reshape(1, -1), ln_g.reshape(1, -1), ln_b.reshape(1, -1))


def _conv_state_kernel(st_ref, w_ref, newst_ref, part_ref):
    st = st_ref[...]
    newst_ref[:, 0:CONV_STATE - 1] = st[:, 1:CONV_STATE]
    newst_ref[:, CONV_STATE - 1:CONV_STATE] = jnp.zeros_like(st[:, 0:1])
    part_ref[...] = jnp.sum(st * w_ref[0:CONV_STATE][None], axis=1)


def conv_state_shift(state_t, conv_w_t, *, bt):
    bsz, _, depth, _ = state_t.shape
    blk = pl.BlockSpec((bt, CONV_STATE, depth, CONV_CH), lambda i: (i, 0, 0, 0))
    return pl.pallas_call(
        _conv_state_kernel,
        grid=(bsz // bt,),
        in_specs=[blk, pl.BlockSpec((CONV_WIDTH, depth, CONV_CH), lambda i: (0, 0, 0))],
        out_specs=[blk, pl.BlockSpec((bt, depth, CONV_CH), lambda i: (i, 0, 0))],
        out_shape=[jax.ShapeDtypeStruct(state_t.shape, F32),
                   jax.ShapeDtypeStruct((bsz, depth, CONV_CH), F32)],
        compiler_params=_params("parallel"),
        name="conv_state_shift",
    )(state_t, conv_w_t)


def _conv_state_finish_kernel(u_ref, st_hbm, o_ref):
    del st_hbm
    o_ref[...] = u_ref[...]


def conv_state_finish(new_state, u_all):
    bsz, _, depth, _ = new_state.shape
    return pl.pallas_call(
        _conv_state_finish_kernel,
        grid=(1,),
        in_specs=[pl.BlockSpec((bsz, 1, depth, CONV_CH), lambda i: (0, 0, 0, 0)),
                  pl.BlockSpec(memory_space=pl.ANY)],
        out_specs=pl.BlockSpec((bsz, 1, depth, CONV_CH), lambda i: (0, CONV_STATE - 1, 0, 0)),
        out_shape=jax.ShapeDtypeStruct(new_state.shape, F32),
        input_output_aliases={1: 0},
        compiler_params=_params("arbitrary"),
        name="conv_state_finish",
    )(u_all, new_state)


def _conv_sample_kernel(a_ref, b_ref, gc_ref, part_ref, w_ref, cb_ref, lg_ref, lb_ref, out_ref, u_ref, *, layer):
    u = a_ref[...] * jax.nn.sigmoid(b_ref[...])
    u_ref[...] = u
    y = part_ref[:, layer, :] + u * w_ref[CONV_STATE:CONV_WIDTH, :] + cb_ref[...]
    mu = jnp.mean(y, axis=-1, keepdims=True)
    d = y - mu
    var = jnp.mean(d * d, axis=-1, keepdims=True)
    z = d * lax.rsqrt(var + EPS) * lg_ref[...] + lb_ref[...]
    out_ref[...] = (_silu(z) * _silu(gc_ref[...])).astype(out_ref.dtype)


def conv_sample(proj, part, layer, conv_w, conv_b, ln_g, ln_b):
    bsz = proj.shape[0]
    vec = lambda: pl.BlockSpec((1, CONV_CH), lambda i: (0, 0))
    row = lambda: pl.BlockSpec((bsz, CONV_CH), lambda i: (0, 0))
    return pl.pallas_call(
        functools.partial(_conv_sample_kernel, layer=layer),
        grid=(1,),
        in_specs=[
            pl.BlockSpec((bsz, CONV_CH), lambda i: (0, COL_A // CONV_CH)),
            pl.BlockSpec((bsz, CONV_CH), lambda i: (0, COL_B // CONV_CH)),
            pl.BlockSpec((bsz, CONV_CH), lambda i: (0, COL_GC // CONV_CH)),
            pl.BlockSpec(part.shape, lambda i: (0, 0, 0)),
            pl.BlockSpec((CONV_WIDTH, CONV_CH), lambda i: (0, 0)),
            vec(), vec(), vec(),
        ],
        out_specs=[row(), row()],
        out_shape=[
            jax.ShapeDtypeStruct((bsz, CONV_CH), BF16),
            jax.ShapeDtypeStruct((bsz, CONV_CH), F32),
        ],
        compiler_params=_params("arbitrary"),
        name="conv_sample",
    )(proj, proj, proj, part, conv_w, conv_b.reshape(1, -1), ln_g.reshape(1, -1), ln_b.reshape(1, -1))


def _mla_common(ql_ref, kvl_ref, kpe_ref, cos_ref, sn_ref, sp_ref, qg_ref, kvg_ref, wqn_ref, wqp_ref):
    qn = _rms(ql_ref[...], qg_ref[...]).astype(BF16)
    q_nope = _dot(qn, wqn_ref[...]) * (MLA_SCALE * LOG2E)
    q_rope_raw = _dot(qn, wqp_ref[...]) * (MLA_SCALE * LOG2E)
    cos_t, sin_n, sin_p = cos_ref[...], sn_ref[...], sp_ref[...]
    q_rope = [_rope128(q_rope_raw[:, h * LANE:(h + 1) * LANE], cos_t, sin_n, sin_p)
              for h in range(MLA_HEADS)]
    c_kv = _rms(kvl_ref[...], kvg_ref[...])
    k_pe = _rope128(kpe_ref[...], cos_t, sin_n, sin_p)
    return q_nope, q_rope, c_kv, k_pe


def _mla_prompt_kernel(ql_ref, kvl_ref, kpe_ref, cos_ref, sn_ref, sp_ref, qg_ref, kvg_ref,
                       wqn_ref, wqp_ref, wuk_ref, wuv_ref,
                       q_ref, k_ref, v_ref, ckv_ref, kpeo_ref):
    q_nope, q_rope, c_kv, k_pe = _mla_common(ql_ref, kvl_ref, kpe_ref, cos_ref, sn_ref, sp_ref,
                                             qg_ref, kvg_ref, wqn_ref, wqp_ref)
    ckv_ref[...] = c_kv
    kpeo_ref[...] = k_pe[:, :QK_ROPE]
    c_bf = c_kv.astype(BF16)
    k_nope = _dot(c_bf, wuk_ref[...])
    v = _dot(c_bf, wuv_ref[...])
    for h in range(MLA_HEADS):
        v_ref[h * V_DIM:(h + 1) * V_DIM, :] = v[:, h * V_DIM:(h + 1) * V_DIM].T.astype(BF16)
    k_pe_bf = k_pe.astype(BF16)
    for h in range(MLA_HEADS):
        lo = h * QK_PAD
        q_ref[:, lo:lo + LANE] = q_nope[:, h * LANE:(h + 1) * LANE].astype(BF16)
        q_ref[:, lo + LANE:lo + QK_PAD] = q_rope[h].astype(BF16)
        k_ref[:, lo:lo + LANE] = k_nope[:, h * LANE:(h + 1) * LANE].astype(BF16)
        k_ref[:, lo + LANE:lo + QK_PAD] = k_pe_bf


def _proj_specs(tm, n_tab):
    return [
        pl.BlockSpec((tm, Q_LORA), lambda i: (i, COL_QL // Q_LORA)),
        pl.BlockSpec((tm, KV_LORA), lambda i: (i, COL_KV // KV_LORA)),
        pl.BlockSpec((tm, LANE), lambda i: (i, COL_KPE // LANE)),
        pl.BlockSpec((tm, LANE), lambda i: (i % n_tab, 0)),
        pl.BlockSpec((tm, LANE), lambda i: (i % n_tab, 0)),
        pl.BlockSpec((tm, LANE), lambda i: (i % n_tab, 0)),
    ]


def _full(shape):
    return pl.BlockSpec(shape, lambda i: (0,) * len(shape))


def mla_prompt(proj, tabs, q_g, kv_g, wqn, wqp, wuk, wuv, *, tm):
    t = proj.shape[0]
    n_tab = tabs[0].shape[0] // tm
    h = MLA_HEADS
    return pl.pallas_call(
        _mla_prompt_kernel,
        grid=(t // tm,),
        in_specs=_proj_specs(tm, n_tab) + [
            _full((1, Q_LORA)), _full((1, KV_LORA)),
            _full((Q_LORA, h * LANE)), _full((Q_LORA, h * LANE)),
            _full((KV_LORA, h * LANE)), _full((KV_LORA, h * LANE)),
        ],
        out_specs=[
            pl.BlockSpec((tm, h * QK_PAD), lambda i: (i, 0)),
            pl.BlockSpec((tm, h * QK_PAD), lambda i: (i, 0)),
            pl.BlockSpec((h * V_DIM, tm), lambda i: (0, i)),
            pl.BlockSpec((tm, KV_LORA), lambda i: (i, 0)),
            pl.BlockSpec((tm, QK_ROPE), lambda i: (i, 0)),
        ],
        out_shape=[
            jax.ShapeDtypeStruct((t, h * QK_PAD), BF16),
            jax.ShapeDtypeStruct((t, h * QK_PAD), BF16),
            jax.ShapeDtypeStruct((h * V_DIM, t), BF16),
            jax.ShapeDtypeStruct((t, KV_LORA), F32),
            jax.ShapeDtypeStruct((t, QK_ROPE), F32),
        ],
        compiler_params=_params("parallel"),
        name="mla_prompt",
    )(proj, proj, proj, *tabs, q_g.reshape(1, -1), kv_g.reshape(1, -1), wqn, wqp, wuk, wuv)


def _mla_sample_kernel(ql_ref, kvl_ref, kpe_ref, cos_ref, sn_ref, sp_ref, qg_ref, kvg_ref,
                       wqn_ref, wqp_ref, wukt_ref,
                       qa_ref, qp_ref, ckv_ref, kpeo_ref):
    q_nope, q_rope, c_kv, k_pe = _mla_common(ql_ref, kvl_ref, kpe_ref, cos_ref, sn_ref, sp_ref,
                                             qg_ref, kvg_ref, wqn_ref, wqp_ref)
    ckv_ref[...] = c_kv
    kpeo_ref[...] = k_pe[:, :QK_ROPE]
    for h in range(MLA_HEADS):
        qa_ref[h] = _dot(q_nope[:, h * LANE:(h + 1) * LANE].astype(BF16), wukt_ref[h])
        qp_ref[h] = q_rope[h]


def mla_sample(proj, tabs, q_g, kv_g, wqn, wqp, wukt):
    t = proj.shape[0]
    h = MLA_HEADS
    return pl.pallas_call(
        _mla_sample_kernel,
        grid=(1,),
        in_specs=_proj_specs(t, 1) + [
            _full((1, Q_LORA)), _full((1, KV_LORA)),
            _full((Q_LORA, h * LANE)), _full((Q_LORA, h * LANE)),
            _full((h, QK_NOPE, KV_LORA)),
        ],
        out_specs=[
            _full((h, t, KV_LORA)), _full((h, t, LANE)),
            _full((t, KV_LORA)), _full((t, QK_ROPE)),
        ],
        out_shape=[
            jax.ShapeDtypeStruct((h, t, KV_LORA), F32),
            jax.ShapeDtypeStruct((h, t, LANE), F32),
            jax.ShapeDtypeStruct((t, KV_LORA), F32),
            jax.ShapeDtypeStruct((t, QK_ROPE), F32),
        ],
        compiler_params=_params("arbitrary"),
        name="mla_sample",
    )(proj, proj, proj, *tabs, q_g.reshape(1, -1), kv_g.reshape(1, -1), wqn, wqp, wukt)


FLASH_TQ = 512


def _flash_tile(q, k_ref, vt_ref, s_t, kv):
    tq = FLASH_TQ
    causal_t = (lax.broadcasted_iota(jnp.int32, (tq, tq), 0)
                <= lax.broadcasted_iota(jnp.int32, (tq, tq), 1))
    if s_t is None:
        s_t = _dot_nt(k_ref[0:kv + tq, :], q)
    s_d = jnp.where(causal_t, s_t[kv:kv + tq], NEG_BIG)
    m = jnp.max(s_d, axis=0, keepdims=True)
    if kv > 0:
        s_o = s_t[0:kv]
        m = jnp.maximum(m, jnp.max(s_o, axis=0, keepdims=True))
        p_o = jnp.exp2(s_o - m)
        l = jnp.sum(p_o, axis=0, keepdims=True)
        acc = _dot(vt_ref[:, 0:kv], p_o.astype(BF16))
    else:
        l = jnp.zeros((1, tq), F32)
        acc = jnp.zeros((V_DIM, tq), F32)
    p_d = jnp.exp2(s_d - m)
    l = l + jnp.sum(p_d, axis=0, keepdims=True)
    acc = acc + _dot(vt_ref[:, kv:kv + tq], p_d.astype(BF16))
    return (acc / l).T


def _flash_kernel(q_ref, k_ref, vt_ref, g_ref, o_ref, *, seq):
    tq = FLASH_TQ
    n_q = seq // tq

    def scores(i):
        kv = i * tq
        return _dot_nt(k_ref[0:kv + tq, :], q_ref[kv:kv + tq, :])

    s_next = scores(0)
    for i in range(n_q):
        kv = i * tq
        s_t = s_next
        if i + 1 < n_q:
            s_next = scores(i + 1)
        o = _flash_tile(None, k_ref, vt_ref, s_t, kv)
        o_ref[kv:kv + tq, :] = (o * _silu(g_ref[kv:kv + tq, :])).astype(o_ref.dtype)


def flash_prompt(q, k, v_t, proj, *, bsz, seq):
    h = MLA_HEADS
    return pl.pallas_call(
        functools.partial(_flash_kernel, seq=seq),
        grid=(bsz, h),
        in_specs=[
            pl.BlockSpec((seq, QK_PAD), lambda b, hh: (b, hh)),
            pl.BlockSpec((seq, QK_PAD), lambda b, hh: (b, hh)),
            pl.BlockSpec((V_DIM, seq), lambda b, hh: (hh, b)),
            pl.BlockSpec((seq, V_DIM), lambda b, hh: (b, COL_GM // V_DIM + hh)),
        ],
        out_specs=pl.BlockSpec((seq, V_DIM), lambda b, hh: (b, hh)),
        out_shape=jax.ShapeDtypeStruct((bsz * seq, h * V_DIM), BF16),
        compiler_params=_params("parallel", "parallel"),
        name="flash_prompt",
    )(q, k, v_t, proj)


PAGED_CHUNK = 1024
PAGED_GROUPS = 4


def _paged_kernel(pt_ref, qa_ref, qp_ref, cn_ref, kn_ref, ckv_hbm, krt_hbm, *rest, layer, n_pages, flash_tiles):
    if flash_tiles:
        fq_ref, fk_ref, fvt_ref, fg_ref, o_ref, fo_ref, cbuf, kbuf, cbf, s_ref, sems, fs_ref = rest
    else:
        o_ref, cbuf, kbuf, cbf, s_ref, sems = rest
    b = pl.program_id(0)
    nb = pl.num_programs(0)
    slot = b % 2

    def copies(bb, sl, p):
        page = pt_ref[bb * n_pages + p]
        toks = pl.ds(p * PAGE_SIZE, PAGE_SIZE)
        return (pltpu.make_async_copy(ckv_hbm.at[page, layer], cbuf.at[sl, toks, :], sems.at[0, sl]),
                pltpu.make_async_copy(krt_hbm.at[page, layer], kbuf.at[sl, :, toks], sems.at[1, sl]))

    def start_all(bb, sl):
        for p in range(n_pages):
            c1, c2 = copies(bb, sl, p)
            c1.start(priority=p % 2)
            c2.start(priority=(p + 1) % 2)

    @pl.when(b == 0)
    def _():
        start_all(b, slot)

    @pl.when(b + 1 < nb)
    def _():
        start_all(b + 1, 1 - slot)

    for c in range(flash_tiles):
        @pl.when(b % flash_tiles == c)
        def _(c=c):
            tq = FLASH_TQ
            kv = c * tq
            if c + 1 < flash_tiles:
                nxt = slice(kv + tq, kv + 2 * tq)
                fs_ref[(c + 1) % 2, 0:kv + 2 * tq, :] = _dot_nt(fk_ref[0:kv + 2 * tq, :], fq_ref[nxt, :])
            s_t = fs_ref[c % 2, 0:kv + tq, :] if c > 0 else None
            rows = slice(kv, kv + tq)
            o = _flash_tile(fq_ref[rows, :], fk_ref, fvt_ref, s_t, kv)
            fo_ref[rows, :] = (o * _silu(fg_ref[rows, :])).astype(fo_ref.dtype)

    for p in range(n_pages):
        c1, c2 = copies(b, slot, p)
        c1.wait()
        c2.wait()

    past = n_pages * PAGE_SIZE
    n_chunks = past // PAGED_CHUNK
    per_group = n_chunks // PAGED_GROUPS
    qa = qa_ref[0].astype(BF16)
    qp = qp_ref[0][:, :QK_ROPE].astype(BF16)

    def qk_chunk(c):
        toks = slice(c * PAGED_CHUNK, (c + 1) * PAGED_CHUNK)
        cc = cbuf[slot, toks, :].astype(BF16)
        kk = kbuf[slot, :, toks].astype(BF16)
        cbf[toks, :] = cc
        s_ref[:, toks] = _dot_nt(qa, cc) + _dot(qp, kk)

    def pv_chunk(c, m_g, l_g, acc_g):
        toks = slice(c * PAGED_CHUNK, (c + 1) * PAGED_CHUNK)
        p = jnp.exp2(s_ref[:, toks] - m_g)
        return l_g + jnp.sum(p, axis=-1, keepdims=True), acc_g + _dot(p.astype(BF16), cbf[toks, :])

    def group_max(g):
        g_toks = slice(g * per_group * PAGED_CHUNK, (g + 1) * per_group * PAGED_CHUNK)
        return jnp.max(s_ref[:, g_toks], axis=-1, keepdims=True)

    stats = []
    for c in range(per_group):
        qk_chunk(c)
    for g in range(PAGED_GROUPS):
        m_g = group_max(g)
        l_g = jnp.zeros((MLA_HEADS, 1), F32)
        acc_g = jnp.zeros((MLA_HEADS, KV_LORA), F32)
        for c in range(per_group):
            if g + 1 < PAGED_GROUPS:
                qk_chunk((g + 1) * per_group + c)
            l_g, acc_g = pv_chunk(g * per_group + c, m_g, l_g, acc_g)
        stats.append((m_g, l_g, acc_g))

    c_new = cn_ref[0].astype(BF16).astype(F32)
    k_new = kn_ref[0].astype(BF16).astype(F32)
    s_new = (jnp.sum(qa.astype(F32) * c_new, axis=-1, keepdims=True)
             + jnp.sum(qp.astype(F32) * k_new, axis=-1, keepdims=True))
    m = s_new
    for m_g, _, _ in stats:
        m = jnp.maximum(m, m_g)
    p_new = jnp.exp2(s_new - m)
    l = p_new
    acc = p_new.astype(BF16).astype(F32) * c_new
    for m_g, l_g, acc_g in stats:
        w_g = jnp.exp2(m_g - m)
        l = l + w_g * l_g
        acc = acc + w_g * acc_g
    o_ref[0] = acc / l


def paged_attention(page_table, qa, qp, c_new, k_new, cache_kv, cache_krt, *, layer, flash=None):
    bsz, n_pages = page_table.shape
    past = n_pages * PAGE_SIZE
    h = MLA_HEADS
    in_specs = [
        pl.BlockSpec((1, h, KV_LORA), lambda b, pt: (b, 0, 0)),
        pl.BlockSpec((1, h, LANE), lambda b, pt: (b, 0, 0)),
        pl.BlockSpec((1, 1, KV_LORA), lambda b, pt: (b, 0, 0)),
        pl.BlockSpec((1, 1, QK_ROPE), lambda b, pt: (b, 0, 0)),
        pl.BlockSpec(memory_space=pl.ANY),
        pl.BlockSpec(memory_space=pl.ANY),
    ]
    args = [qa, qp, c_new, k_new, cache_kv, cache_krt]
    out_specs = [pl.BlockSpec((1, h, KV_LORA), lambda b, pt: (b, 0, 0))]
    out_shape = [jax.ShapeDtypeStruct((bsz, h, KV_LORA), F32)]
    flash_tiles, flash_scratch = 0, []
    if flash is not None:
        fq, fk, fvt, proj, f_bsz, seq = flash
        flash_tiles = seq // FLASH_TQ
        assert f_bsz * h * flash_tiles == bsz
        sq = lambda b: b // flash_tiles // h
        head = lambda b: b // flash_tiles % h
        in_specs += [
            pl.BlockSpec((seq, QK_PAD), lambda b, pt: (sq(b), head(b))),
            pl.BlockSpec((seq, QK_PAD), lambda b, pt: (sq(b), head(b))),
            pl.BlockSpec((V_DIM, seq), lambda b, pt: (head(b), sq(b))),
            pl.BlockSpec((seq, V_DIM), lambda b, pt: (sq(b), COL_GM // V_DIM + head(b))),
        ]
        args += [fq, fk, fvt, proj]
        out_specs.append(pl.BlockSpec((seq, V_DIM), lambda b, pt: (sq(b), head(b))))
        out_shape.append(jax.ShapeDtypeStruct((f_bsz * seq, h * V_DIM), BF16))
        flash_scratch = [pltpu.VMEM((2, seq, FLASH_TQ), F32)]
    grid_spec = pltpu.PrefetchScalarGridSpec(
        num_scalar_prefetch=1,
        grid=(bsz,),
        in_specs=in_specs,
        out_specs=out_specs,
        scratch_shapes=[
            pltpu.VMEM((2, past, KV_LORA), F32),
            pltpu.VMEM((2, QK_ROPE, past), F32),
            pltpu.VMEM((past, KV_LORA), BF16),
            pltpu.VMEM((h, past), F32),
            pltpu.SemaphoreType.DMA((2, 2)),
        ] + flash_scratch,
    )
    outs = pl.pallas_call(
        functools.partial(_paged_kernel, layer=layer, n_pages=n_pages, flash_tiles=flash_tiles),
        grid_spec=grid_spec,
        out_shape=out_shape,
        compiler_params=_params("arbitrary"),
        name="paged_attention",
    )(page_table.reshape(-1), *args)
    return outs if flash is not None else outs[0]


def _uv_kernel(ol_ref, wuv_ref, g_ref, o_ref):
    for h in range(MLA_HEADS):
        o = _dot(ol_ref[h].astype(BF16), wuv_ref[h])
        sl = slice(h * V_DIM, (h + 1) * V_DIM)
        o_ref[:, sl] = (o * _silu(g_ref[:, sl])).astype(o_ref.dtype)


def uv_project(o_lat, wuv3, proj):
    h, t, _ = o_lat.shape
    return pl.pallas_call(
        _uv_kernel,
        grid=(1,),
        in_specs=[
            _full((h, t, KV_LORA)), _full((h, KV_LORA, V_DIM)),
            pl.BlockSpec((t, MLA_WIDTH), lambda i: (0, COL_GM // MLA_WIDTH)),
        ],
        out_specs=_full((t, MLA_WIDTH)),
        out_shape=jax.ShapeDtypeStruct((t, MLA_WIDTH), BF16),
        compiler_params=_params("arbitrary"),
        name="uv_project",
    )(o_lat, wuv3, proj)


def _proj_norm_res_kernel(*refs, n_in):
    ins, ws = refs[:n_in], refs[n_in:2 * n_in]
    x_ref, g_ref, o_ref = refs[2 * n_in:]
    y = _dot(ins[0][...].astype(BF16), ws[0][...])
    for a, w in zip(ins[1:], ws[1:]):
        y = y + _dot(a[...].astype(BF16), w[...])
    o_ref[...] = x_ref[...] + _rms(y, g_ref[...])


def proj_norm_res(ins, ws, x, g, *, tm, name):
    t, d = x.shape
    n_in = len(ins)
    return pl.pallas_call(
        functools.partial(_proj_norm_res_kernel, n_in=n_in),
        grid=(t // tm,),
        in_specs=([pl.BlockSpec((tm, a.shape[1]), lambda i: (i, 0)) for a in ins]
                  + [_full(w.shape) for w in ws]
                  + [pl.BlockSpec((tm, d), lambda i: (i, 0)), _full((1, d))]),
        out_specs=pl.BlockSpec((tm, d), lambda i: (i, 0)),
        out_shape=jax.ShapeDtypeStruct((t, d), F32),
        compiler_params=_params("parallel"),
        name=name,
    )(*ins, *ws, x, g.reshape(1, d))


def _mix_cross_kernel(conv_ref, attn_ref, x_ref, wc_ref, wm_ref, gmix_ref,
                      gpre_ref, wq_ref, mk_ref, mv_ref, wo_ref, gpost_ref, o_ref):
    y = _dot(conv_ref[...].astype(BF16), wc_ref[...]) + _dot(attn_ref[...], wm_ref[...])
    x = x_ref[...] + _rms(y, gmix_ref[...])
    hn = _rms(x, gpre_ref[...]).astype(BF16)
    q = (_dot(hn, wq_ref[...]) * MEM_SCALE).astype(BF16)
    outs = []
    for h in range(MEM_HEADS):
        sl = slice(h * MEM_HEAD_DIM, (h + 1) * MEM_HEAD_DIM)
        s = _dot_nt(q[:, sl], mk_ref[:, sl].astype(BF16))
        m = jnp.max(s, axis=-1, keepdims=True)
        p = jnp.exp(s - m)
        l = jnp.sum(p, axis=-1, keepdims=True)
        outs.append((_dot(p.astype(BF16), mv_ref[:, sl].astype(BF16)) / l).astype(BF16))
    y2 = _dot(jnp.concatenate(outs, axis=-1), wo_ref[...])
    o_ref[...] = x + _rms(y2, gpost_ref[...])


def mix_cross_prompt(conv_out, attn, x, wc, wm, g_mix, g_pre, wq, mem_kv, wo, g_post, *, seq, tm):
    t, d = x.shape
    per_seq = seq // tm
    const = lambda shape: pl.BlockSpec(shape, lambda i: (0,) * len(shape), pipeline_mode=pl.Buffered(1))
    rows = lambda width: pl.BlockSpec((tm, width), lambda i: (i, 0))
    return pl.pallas_call(
        _mix_cross_kernel,
        grid=(t // tm,),
        in_specs=[
            rows(CONV_CH), rows(MLA_WIDTH), rows(d),
            const(wc.shape), const(wm.shape), const((1, d)),
            const((1, d)), const(wq.shape),
            pl.BlockSpec((N_MEM, MEM_WIDTH), lambda i: (i // per_seq, 0)),
            pl.BlockSpec((N_MEM, MEM_WIDTH), lambda i: (i // per_seq, 1)),
            const(wo.shape), const((1, d)),
        ],
        out_specs=rows(d),
        out_shape=jax.ShapeDtypeStruct((t, d), F32),
        compiler_params=_params("parallel"),
        name="mix_cross_prompt",
    )(conv_out, attn, x, wc, wm, g_mix.reshape(1, d), g_pre.reshape(1, d), wq, mem_kv, mem_kv, wo,
      g_post.reshape(1, d))


def _cross_sample_kernel(q_ref, mk_ref, mv_ref, o_ref):
    bt = q_ref.shape[0]
    q = (q_ref[...] * MEM_SCALE).astype(BF16)
    for h in range(MEM_HEADS):
        sl = slice(h * MEM_HEAD_DIM, (h + 1) * MEM_HEAD_DIM)
        rows = pl.ds(h, N_MEM, stride=MEM_HEADS)
        qh = jnp.broadcast_to(q[:, :, sl], (bt, 8, MEM_HEAD_DIM))
        kh = mk_ref[:, 0, rows, :].astype(BF16)
        vh = mv_ref[:, 0, rows, :].astype(BF16)
        s = jnp.einsum("bqe,bme->bqm", qh, kh, preferred_element_type=F32)
        m = jnp.max(s, axis=-1, keepdims=True)
        p = jnp.exp(s - m)
        l = jnp.sum(p, axis=-1, keepdims=True)
        o = jnp.einsum("bqm,bme->bqe", p.astype(BF16), vh, preferred_element_type=F32) / l
        o_ref[:, :, sl] = o[:, 0:1, :].astype(o_ref.dtype)


def cross_sample_core(q3, mem_k, mem_v, *, layer, bt):
    bsz = q3.shape[0]
    mem = lambda: pl.BlockSpec((bt, 1, N_MEM * MEM_HEADS, MEM_HEAD_DIM), lambda i: (i, layer, 0, 0))
    return pl.pallas_call(
        _cross_sample_kernel,
        grid=(bsz // bt,),
        in_specs=[pl.BlockSpec((bt, 1, MEM_WIDTH), lambda i: (i, 0, 0)), mem(), mem()],
        out_specs=pl.BlockSpec((bt, 1, MEM_WIDTH), lambda i: (i, 0, 0)),
        out_shape=jax.ShapeDtypeStruct((bsz, 1, MEM_WIDTH), BF16),
        compiler_params=_params("parallel"),
        name="cross_sample",
    )(q3, mem_k, mem_v)


def _rope_tables(pos):
    half = QK_ROPE // 2
    inv_freq = ROPE_BASE ** (-np.arange(half, dtype=np.float64) / half)
    ang = pos.astype(np.float64)[:, None] * inv_freq[None, :]
    cos, sin = np.cos(ang), np.sin(ang)
    z = np.zeros_like(cos)
    tabs = ([cos, cos, z, z], [-sin, z, z, z], [z, sin, z, z])
    return tuple(jnp.asarray(np.concatenate(t, axis=-1), F32) for t in tabs)


def _prep_layer(l, w_uq, w_uk, w_uv, w_out, w_xq, w_mk, w_mv, w_xo):
    d = w_out.shape[2]
    uq = w_uq[l]
    wqn = uq[:, :, :QK_NOPE].reshape(Q_LORA, -1).astype(BF16)
    wqp = jnp.pad(uq[:, :, QK_NOPE:], ((0, 0), (0, 0), (0, LANE - QK_ROPE))).reshape(Q_LORA, -1).astype(BF16)
    return dict(
        wqn=wqn, wqp=wqp,
        wuk=w_uk[l].reshape(KV_LORA, -1).astype(BF16),
        wuv=w_uv[l].reshape(KV_LORA, -1).astype(BF16),
        wukt=w_uk[l].transpose(1, 2, 0).astype(BF16),
        wuv3=w_uv[l].transpose(1, 0, 2).astype(BF16),
        w_out_c=w_out[l][:CONV_CH].astype(BF16),
        w_out_m=w_out[l][CONV_CH:].astype(BF16),
        w_xq=w_xq[l].reshape(d, -1).astype(BF16),
        w_mkv=jnp.concatenate([w_mk[l].reshape(d, -1), w_mv[l].reshape(d, -1)], axis=1).astype(BF16),
        w_xo=w_xo[l].reshape(-1, d).astype(BF16),
    )


def kernel(x_prompt, x_sample, mem_prompt, cache_kv_latent, cache_k_rope, state_conv, cache_mem_k, cache_mem_v, page_table, norm_mix_pre, w_in, conv_w, conv_b, conv_ln_g, conv_ln_b, q_norm_g, w_uq, kv_norm_g, w_uk, w_uv, w_out, norm_mix_post, norm_x_pre, norm_mem, w_xq, w_mk, w_mv, w_xo, norm_x_post):
    b_p, s_p, d = x_prompt.shape
    b_s = x_sample.shape[0]
    depth = w_in.shape[0]
    n_pages = page_table.shape[1]
    t_p = b_p * s_p
    tm_p = min(ROW_TILE, s_p)

    tabs_p = _rope_tables(np.arange(s_p))
    tabs_s = _rope_tables(np.full((b_s,), n_pages * PAGE_SIZE))
    mem2 = mem_prompt.reshape(b_p * N_MEM, d)
    mem_k_s = cache_mem_k.reshape(b_s, depth, N_MEM * MEM_HEADS, MEM_HEAD_DIM)
    mem_v_s = cache_mem_v.reshape(b_s, depth, N_MEM * MEM_HEADS, MEM_HEAD_DIM)
    cache_krt = cache_k_rope.transpose(0, 1, 3, 2)
    state_t = state_conv.transpose(0, 2, 1, 3)
    conv_w_t = conv_w.transpose(1, 0, 2)
    new_state_t, conv_part = conv_state_shift(state_t, conv_w_t, bt=min(STATE_BATCH_TILE, b_s))
    w_in_t = w_in_prep(w_in.transpose(0, 2, 1))

    xp = x_prompt.reshape(t_p, d)
    xs = x_sample.reshape(b_s, d)
    lat_p, kpe_p, conv_p, mk_p, mv_p, lat_s, kpe_s = [], [], [], [], [], [], []
    u_s = []
    for l in range(depth):
        w = _prep_layer(l, w_uq, w_uk, w_uv, w_out, w_xq, w_mk, w_mv, w_xo)

        proj = rms_matmul(xp, norm_mix_pre[l], w_in_t, tm=min(IN_PROJ_ROWS, s_p), tn=IN_PROJ_COLS, name="in_proj_p",
                          w_transposed=True, layer=l)
        conv_out, conv_state = conv_prompt(proj, conv_w[l], conv_b[l], conv_ln_g[l], conv_ln_b[l],
                                           bsz=b_p, seq=s_p)
        q, k, v, c_kv, k_pe = mla_prompt(proj, tabs_p, q_norm_g[l], kv_norm_g[l],
                                         w["wqn"], w["wqp"], w["wuk"], w["wuv"], tm=tm_p)
        fuse = b_p * MLA_HEADS * (s_p // FLASH_TQ) == b_s
        if not fuse:
            attn = flash_prompt(q, k, v, proj, bsz=b_p, seq=s_p)

        proj_s = rms_matmul(xs, norm_mix_pre[l], w_in_t, tm=b_s, tn=IN_PROJ_COLS, name="in_proj_s",
                            w_transposed=True, layer=l)
        conv_out_s, u_l = conv_sample(proj_s, conv_part, l, conv_w[l], conv_b[l], conv_ln_g[l], conv_ln_b[l])
        u_s.append(u_l)
        qa, qp, c_new, k_new = mla_sample(proj_s, tabs_s, q_norm_g[l], kv_norm_g[l],
                                          w["wqn"], w["wqp"], w["wukt"])
        paged = paged_attention(page_table, qa.transpose(1, 0, 2), qp.transpose(1, 0, 2),
                                c_new.reshape(b_s, 1, KV_LORA), k_new.reshape(b_s, 1, QK_ROPE),
                                cache_kv_latent, cache_krt, layer=l,
                                flash=(q, k, v, proj, b_p, s_p) if fuse else None)
        o_lat, attn = paged if fuse else (paged, attn)

        mem_kv = rms_matmul(mem2, norm_mem[l], w["w_mkv"], tm=N_MEM, tn=2 * MEM_WIDTH, name="mem_kv")
        xp = mix_cross_prompt(conv_out, attn, xp, w["w_out_c"], w["w_out_m"], norm_mix_post[l],
                              norm_x_pre[l], w["w_xq"], mem_kv, w["w_xo"], norm_x_post[l], seq=s_p, tm=tm_p)
        lat_p.append(c_kv.reshape(b_p, s_p, KV_LORA))
        kpe_p.append(k_pe.reshape(b_p, s_p, QK_ROPE))
        conv_p.append(conv_state)
        mk_p.append(mem_kv[:, :MEM_WIDTH].reshape(b_p, N_MEM, MEM_HEADS, MEM_HEAD_DIM))
        mv_p.append(mem_kv[:, MEM_WIDTH:].reshape(b_p, N_MEM, MEM_HEADS, MEM_HEAD_DIM))

        attn_s = uv_project(o_lat.transpose(1, 0, 2), w["wuv3"], proj_s)
        xs = proj_norm_res([conv_out_s, attn_s], [w["w_out_c"], w["w_out_m"]],
                           xs, norm_mix_post[l], tm=b_s, name="out_proj_s")
        q_x = rms_matmul(xs, norm_x_pre[l], w["w_xq"], tm=b_s, tn=MEM_WIDTH, name="cross_q_s")
        o_x = cross_sample_core(q_x.reshape(b_s, 1, MEM_WIDTH), mem_k_s, mem_v_s, layer=l, bt=CROSS_BATCH_TILE)
        xs = proj_norm_res([o_x.reshape(b_s, MEM_WIDTH)], [w["w_xo"]], xs, norm_x_post[l],
                           tm=b_s, name="cross_out_s")
        lat_s.append(c_new.reshape(b_s, 1, KV_LORA))
        kpe_s.append(k_new.reshape(b_s, 1, QK_ROPE))

    return (xp.reshape(b_p, s_p, d), xs.reshape(b_s, 1, d),
            jnp.stack(lat_p, axis=1), jnp.stack(kpe_p, axis=1), jnp.stack(conv_p, axis=1),
            jnp.stack(mk_p, axis=1), jnp.stack(mv_p, axis=1),
            jnp.stack(lat_s, axis=1), jnp.stack(kpe_s, axis=1),
            conv_state_finish(new_state_t, jnp.stack(u_s, axis=1)[:, None]).transpose(0, 2, 1, 3))
```
